```python
import jax, jax.numpy as jnp
from jax import lax
import numpy as np

D_MODEL = 1024
BATCH = 16
SEQ = 4096
DEPTH = 2
DEC_BATCH = 2
DEC_SEQ = 16384
PAST_LEN = 128

GRID_W = 64
EPS = 1e-6
HEAD_DIM = 64
HG_HEADS = 4
HG_DK = 64
HG_DV = 64
HG_WIDTH = HG_HEADS * HG_DV
HG_CHUNK = 64
AT_Q_HEADS = 8
AT_KV_HEADS = 2
AT_GROUP = AT_Q_HEADS // AT_KV_HEADS
AT_WIDTH = AT_Q_HEADS * HEAD_DIM
KV_WIDTH = AT_KV_HEADS * HEAD_DIM
AT_BLOCK = 128
ROPE_THETA = 10000.0
SSD_HEADS = 4
SSD_HEAD_DIM = 64
SSD_WIDTH = SSD_HEADS * SSD_HEAD_DIM
SSD_GROUPS = 2
SSD_STATE = 64
SSD_CONV = 5
SSD_CHUNK = 128
SSD_XBC = SSD_WIDTH + 2 * SSD_GROUPS * SSD_STATE
MIX_WIDTH = HG_WIDTH + AT_WIDTH + SSD_WIDTH
IN_COLS = 5 * HG_WIDTH + AT_WIDTH + 2 * KV_WIDTH + SSD_WIDTH + SSD_XBC + 2 * SSD_HEADS
N_EXPERTS = 16
EC_CAPACITY = 2
D_EXPERT = 1024

kernel_name = 'hybrid_hgrn2_gqa_ssd_expert_choice_encoder'


def _rms_norm(x, g):
    xf = x.astype(jnp.float32)
    y = xf * lax.rsqrt(jnp.mean(xf * xf, axis=-1, keepdims=True) + EPS)
    return (y * g.astype(jnp.float32)).astype(x.dtype)


def _modulate(x, g, shift, scale):
    return _rms_norm(x, g) * (1 + scale[:, None, :]) + shift[:, None, :]


def _flip(t):
    return jnp.flip(t, axis=1)


def _axial_rope(x, rows):
    half = HEAD_DIM // 2
    quarter = half // 2
    inv = ROPE_THETA ** (-jnp.arange(quarter, dtype=jnp.float32) / quarter)
    r_idx = jnp.repeat(jnp.arange(rows), GRID_W).astype(jnp.float32)
    c_idx = jnp.tile(jnp.arange(GRID_W), rows).astype(jnp.float32)
    def rot(xh, pos):
        ang = pos[:, None] * inv[None, :]
        cos = jnp.cos(ang)[None, :, None, :]
        sin = jnp.sin(ang)[None, :, None, :]
        x1, x2 = xh[..., :quarter], xh[..., quarter:]
        return jnp.concatenate([x1 * cos - x2 * sin, x2 * cos + x1 * sin], axis=-1)
    xf = x.astype(jnp.float32)
    out = jnp.concatenate([rot(xf[..., :half], r_idx), rot(xf[..., half:], c_idx)], axis=-1)
    return out.astype(x.dtype)


def _attention(q, k, v):
    bsz, L = q.shape[0], q.shape[1]
    nb = L // AT_BLOCK
    qb = q.reshape(bsz, nb, AT_BLOCK, AT_KV_HEADS, AT_GROUP, HEAD_DIM).transpose(1, 0, 2, 3, 4, 5)
    scale = HEAD_DIM ** -0.5
    def one_block(qblk):
        s = jnp.einsum('bqkgd,bskd->bkgqs', qblk, k).astype(jnp.float32) * scale
        p = jax.nn.softmax(s, axis=-1).astype(v.dtype)
        return jnp.einsum('bkgqs,bskd->bqkgd', p, v)
    o = lax.map(one_block, qb)
    return o.transpose(1, 0, 2, 3, 4, 5).reshape(bsz, L, AT_WIDTH)


def _hgrn_scan(q, k, v, logf):
    bsz, L, H, dk = q.shape
    dv = v.shape[-1]
    nc = L // HG_CHUNK
    def chunks(t):
        return t.reshape(bsz, nc, HG_CHUNK, H, t.shape[-1]).transpose(1, 0, 3, 2, 4)
    mask = jnp.tril(jnp.ones((HG_CHUNK, HG_CHUNK), dtype=bool))[:, :, None]
    def step(S, inp):
        qc, kc, vc, lfc = inp
        b = jnp.cumsum(lfc, axis=2)
        inter = jnp.einsum('bhtd,bhde->bhte', qc * jnp.exp(b), S)
        expo = jnp.where(mask, b[:, :, :, None, :] - b[:, :, None, :, :], -jnp.inf)
        att = jnp.einsum('bhtsd,bhsd->bhts', qc[:, :, :, None, :] * jnp.exp(expo), kc)
        intra = jnp.einsum('bhts,bhse->bhte', att, vc)
        b_last = b[:, :, -1:, :]
        S_new = jnp.exp(b_last[:, :, 0, :])[..., None] * S + jnp.einsum('bhsd,bhse->bhde', kc * jnp.exp(b_last - b), vc)
        return S_new, inter + intra
    S0 = jnp.zeros((bsz, H, dk, dv), jnp.float32)
    _, o = lax.scan(step, S0, (chunks(q), chunks(k), chunks(v), chunks(logf)))
    return o.transpose(1, 0, 3, 2, 4).reshape(bsz, L, H, dv)


def _hgrn_mixer(q_raw, i_raw, ff_raw, fb_raw, g_raw, lb, norm_g):
    bsz, L, _ = q_raw.shape
    def heads(t):
        return t.astype(jnp.float32).reshape(bsz, L, HG_HEADS, -1)
    q = jax.nn.silu(heads(q_raw))
    v = heads(i_raw)
    def gate(f_raw, lbd):
        fr = heads(f_raw)
        lbh = lbd.astype(jnp.float32).reshape(HG_HEADS, HG_DK)
        logf = jnp.logaddexp(jnp.log(lbh), jnp.log1p(-lbh) + jax.nn.log_sigmoid(fr))
        k = (1.0 - lbh) * jax.nn.sigmoid(-fr)
        return k, logf
    kf, lff = gate(ff_raw, lb[0])
    kb, lfb = gate(fb_raw, lb[1])
    o_f = _hgrn_scan(q, kf, v, lff)
    o_b = _flip(_hgrn_scan(_flip(q), _flip(kb), _flip(v), _flip(lfb)))
    o = _rms_norm(o_f + o_b, norm_g).reshape(bsz, L, HG_WIDTH) * jax.nn.silu(g_raw.astype(jnp.float32))
    return o.astype(q_raw.dtype)


def _segsum(a):
    T = a.shape[-1]
    cum = jnp.cumsum(a, axis=-1)
    diff = cum[..., :, None] - cum[..., None, :]
    mask = jnp.tril(jnp.ones((T, T), dtype=bool))
    return jnp.where(mask, diff, -jnp.inf)


def _ssd_scan(x, dt, a_coef, bm, cm):
    bsz, L, H, P = x.shape
    N = bm.shape[-1]
    nc = L // SSD_CHUNK
    xc = (x * dt[..., None]).reshape(bsz, nc, SSD_CHUNK, H, P)
    bc = bm.reshape(bsz, nc, SSD_CHUNK, H, N)
    cc = cm.reshape(bsz, nc, SSD_CHUNK, H, N)
    a = (dt * a_coef).reshape(bsz, nc, SSD_CHUNK, H).transpose(0, 3, 1, 2)
    a_cum = jnp.cumsum(a, axis=-1)
    scores = jnp.einsum('bclhn,bcshn->bhcls', cc, bc) * jnp.exp(_segsum(a))
    y_diag = jnp.einsum('bhcls,bcshp->bclhp', scores, xc)
    decay_states = jnp.exp(a_cum[..., -1:] - a_cum)
    states = jnp.einsum('bclhn,bhcl,bclhp->bchpn', bc, decay_states, xc)
    states = jnp.concatenate([jnp.zeros_like(states[:, :1]), states], axis=1)
    chunk_decay = jnp.exp(_segsum(jnp.pad(a_cum[..., -1], ((0, 0), (0, 0), (1, 0)))))
    states_in = jnp.einsum('bhzc,bchpn->bzhpn', chunk_decay, states)[:, :-1]
    y_off = jnp.einsum('bclhn,bchpn,bhcl->bclhp', cc, states_in, jnp.exp(a_cum))
    return (y_diag + y_off).reshape(bsz, L, H, P)


def _ssd_mixer(z, xbc, dt_raw, conv_w, conv_b, a_log, dt_bias, d_skip, norm_g):
    bsz, L, _ = z.shape
    pad = SSD_CONV // 2
    xbc = lax.conv_general_dilated(xbc, conv_w[:, None, :], window_strides=(1,), padding=[(pad, pad)],
                                   dimension_numbers=('NWC', 'WIO', 'NWC'), feature_group_count=SSD_XBC) + conv_b
    xbc = jax.nn.silu(xbc.astype(jnp.float32))
    xs = xbc[..., :SSD_WIDTH].reshape(bsz, L, SSD_HEADS, SSD_HEAD_DIM)
    rep = SSD_HEADS // SSD_GROUPS
    bm = jnp.repeat(xbc[..., SSD_WIDTH:SSD_WIDTH + SSD_GROUPS * SSD_STATE].reshape(bsz, L, SSD_GROUPS, SSD_STATE), rep, axis=2)
    cm = jnp.repeat(xbc[..., SSD_WIDTH + SSD_GROUPS * SSD_STATE:].reshape(bsz, L, SSD_GROUPS, SSD_STATE), rep, axis=2)
    dt = jax.nn.softplus(dt_raw.astype(jnp.float32).reshape(bsz, L, 2, SSD_HEADS) + dt_bias.astype(jnp.float32))
    a_coef = -jnp.exp(a_log.astype(jnp.float32))
    y_f = _ssd_scan(xs, dt[:, :, 0], a_coef[0], bm, cm)
    y_b = _flip(_ssd_scan(_flip(xs), _flip(dt[:, :, 1]), a_coef[1], _flip(bm), _flip(cm)))
    y = y_f + y_b + xs * d_skip.astype(jnp.float32)[:, None]
    zh = z.astype(jnp.float32).reshape(bsz, L, SSD_HEADS, SSD_HEAD_DIM)
    y = _rms_norm(y * jax.nn.silu(zh), norm_g)
    return y.reshape(bsz, L, SSD_WIDTH).astype(z.dtype)


def _expert_choice_ffn(h, w_router, w_gate, w_up, w_down):
    bsz, L, D = h.shape
    n_tok = bsz * L
    cap = EC_CAPACITY * n_tok // N_EXPERTS
    hf = h.reshape(n_tok, D)
    aff = jax.nn.softmax(jnp.einsum('nd,de->ne', hf, w_router).astype(jnp.float32), axis=-1)
    gate, idx = lax.top_k(aff.T, cap)
    xg = hf[idx]
    hid = jax.nn.silu(jnp.einsum('ecd,edf->ecf', xg, w_gate)) * jnp.einsum('ecd,edf->ecf', xg, w_up)
    y = jnp.einsum('ecf,efd->ecd', hid, w_down) * gate[..., None].astype(h.dtype)
    out = jnp.zeros((n_tok, D), h.dtype).at[idx.reshape(-1)].add(y.reshape(-1, D))
    return out.reshape(bsz, L, D)


def _trunk(x, c, norm1_g, norm2_g, w_mod, b_mod, w_in, hg_lb, hg_norm_g, q_norm_g, k_norm_g,
           conv_w, conv_b, a_log, dt_bias, d_skip, ssd_norm_g, w_out, w_router, w_gate, w_up, w_down, final_g):
    bsz, L, _ = x.shape
    rows = L // GRID_W
    lb_all = jnp.cumsum(jax.nn.softmax(hg_lb.astype(jnp.float32), axis=0), axis=0)
    lb_all = lb_all - lb_all[:1]
    sizes = [HG_WIDTH] * 5 + [AT_WIDTH, KV_WIDTH, KV_WIDTH, SSD_WIDTH, SSD_XBC, 2 * SSD_HEADS]
    cuts = [int(v) for v in np.cumsum(sizes)[:-1]]
    for l in range(DEPTH):
        mod = jnp.einsum('bd,de->be', jax.nn.silu(c), w_mod[l]) + b_mod[l]
        sh1, sc1, g1, sh2, sc2, g2 = jnp.split(mod, 6, axis=-1)
        h = _modulate(x, norm1_g[l], sh1, sc1)
        proj = jnp.einsum('bld,de->ble', h, w_in[l])
        hq, hi, hff, hfb, hg, aq, ak, av, sz, sxbc, sdt = jnp.split(proj, cuts, axis=-1)
        o_hg = _hgrn_mixer(hq, hi, hff, hfb, hg, lb_all[l], hg_norm_g[l])
        q = _axial_rope(_rms_norm(aq.reshape(bsz, L, AT_Q_HEADS, HEAD_DIM), q_norm_g[l]), rows)
        k = _axial_rope(_rms_norm(ak.reshape(bsz, L, AT_KV_HEADS, HEAD_DIM), k_norm_g[l]), rows)
        v = av.reshape(bsz, L, AT_KV_HEADS, HEAD_DIM)
        o_at = _attention(q, k, v)
        o_ssd = _ssd_mixer(sz, sxbc, sdt, conv_w[l], conv_b[l], a_log[l], dt_bias[l], d_skip[l], ssd_norm_g[l])
        mix = jnp.concatenate([o_hg, o_at, o_ssd], axis=-1)
        x = x + g1[:, None, :] * jnp.einsum('ble,ed->bld', mix, w_out[l])
        h = _modulate(x, norm2_g[l], sh2, sc2)
        x = x + g2[:, None, :] * _expert_choice_ffn(h, w_router[l], w_gate[l], w_up[l], w_down[l])
    return _rms_norm(x, final_g)


def setup_inputs(seed: int = 0) -> dict:
    key = jax.random.key(seed)
    ks = jax.random.split(key, 32)
    f32 = jnp.float32
    nrm = lambda k, shape, s: jax.random.normal(k, shape, f32) * s
    dt0 = jnp.exp(jax.random.uniform(ks[16], (DEPTH, 2, SSD_HEADS), f32, np.log(1e-3), np.log(1e-1)))
    return {
        'x_prompt': nrm(ks[0], (BATCH, SEQ, D_MODEL), 1.0),
        'x_sample': nrm(ks[1], (DEC_BATCH, DEC_SEQ, D_MODEL), 1.0),
        'c_prompt': nrm(ks[2], (BATCH, D_MODEL), 1.0),
        'c_sample': nrm(ks[3], (DEC_BATCH, D_MODEL), 1.0),
        'norm1_g': 1.0 + nrm(ks[4], (DEPTH, D_MODEL), 0.02),
        'norm2_g': 1.0 + nrm(ks[5], (DEPTH, D_MODEL), 0.02),
        'w_mod': nrm(ks[6], (DEPTH, D_MODEL, 6 * D_MODEL), 0.5 * D_MODEL ** -0.5),
        'b_mod': nrm(ks[7], (DEPTH, 6 * D_MODEL), 0.02),
        'w_in': nrm(ks[8], (DEPTH, D_MODEL, IN_COLS), D_MODEL ** -0.5),
        'hg_lb': nrm(ks[9], (DEPTH, 2, HG_WIDTH), 0.5),
        'hg_norm_g': 1.0 + nrm(ks[10], (DEPTH, HG_DV), 0.02),
        'q_norm_g': 1.0 + nrm(ks[11], (DEPTH, HEAD_DIM), 0.02),
        'k_norm_g': 1.0 + nrm(ks[12], (DEPTH, HEAD_DIM), 0.02),
        'conv_w': nrm(ks[13], (DEPTH, SSD_CONV, SSD_XBC), SSD_CONV ** -0.5),
        'conv_b': nrm(ks[14], (DEPTH, SSD_XBC), 0.02),
        'a_log': jnp.log(jax.random.uniform(ks[15], (DEPTH, 2, SSD_HEADS), f32, 1.0, 16.0)),
        'dt_bias': dt0 + jnp.log(-jnp.expm1(-dt0)),
        'd_skip': 1.0 + nrm(ks[17], (DEPTH, SSD_HEADS), 0.1),
        'ssd_norm_g': 1.0 + nrm(ks[18], (DEPTH, SSD_HEAD_DIM), 0.02),
        'w_out': nrm(ks[19], (DEPTH, MIX_WIDTH, D_MODEL), MIX_WIDTH ** -0.5),
        'w_router': nrm(ks[20], (DEPTH, D_MODEL, N_EXPERTS), D_MODEL ** -0.5),
        'w_gate': nrm(ks[21], (DEPTH, N_EXPERTS, D_MODEL, D_EXPERT), D_MODEL ** -0.5),
        'w_up': nrm(ks[22], (DEPTH, N_EXPERTS, D_MODEL, D_EXPERT), D_MODEL ** -0.5),
        'w_down': nrm(ks[23], (DEPTH, N_EXPERTS, D_EXPERT, D_MODEL), D_EXPERT ** -0.5),
        'final_g': 1.0 + nrm(ks[24], (D_MODEL,), 0.02),
    }


def reference(x_prompt, x_sample, c_prompt, c_sample, norm1_g, norm2_g, w_mod, b_mod, w_in, hg_lb, hg_norm_g,
              q_norm_g, k_norm_g, conv_w, conv_b, a_log, dt_bias, d_skip, ssd_norm_g, w_out, w_router,
              w_gate, w_up, w_down, final_g):
    y_prompt = _trunk(x_prompt, c_prompt, norm1_g, norm2_g, w_mod, b_mod, w_in, hg_lb, hg_norm_g, q_norm_g, k_norm_g,
                      conv_w, conv_b, a_log, dt_bias, d_skip, ssd_norm_g, w_out, w_router, w_gate, w_up, w_down, final_g)
    y_sample = _trunk(x_sample, c_sample, norm1_g, norm2_g, w_mod, b_mod, w_in, hg_lb, hg_norm_g, q_norm_g, k_norm_g,
                      conv_w, conv_b, a_log, dt_bias, d_skip, ssd_norm_g, w_out, w_router, w_gate, w_up, w_down, final_g)
    return (y_prompt, y_sample)
```

```python
import functools

import numpy as np
import jax
import jax.numpy as jnp
from jax import lax
from jax.experimental import pallas as pl
from jax.experimental.pallas import tpu as pltpu

F32 = jnp.float32
BF16 = jnp.bfloat16
I32 = jnp.int32
HI = lax.Precision.HIGHEST

D = 1024
DEPTH = 2
GRID_W = 64
EPS = 1e-6
HD = 64
HG_W = 256
HG_CHUNK = 64
HG_SUB = 16
AT_W = 512
KV_W = 128
AT_GROUP = 4
ROPE_THETA = 10000.0
SSD_W = 256
SSD_XBC = 512
SSD_CHUNK = 128
SSD_CONV = 5
N_EXPERTS = 16
EC_CAPACITY = 2
HG_COLS = 5 * HG_W
AT_COLS = AT_W + 2 * KV_W
SSD_COLS = SSD_W + SSD_XBC + 128
IN_PAD = HG_COLS + AT_COLS + SSD_COLS
LANES = 128
SLOT_TILE = 256
ROUTE_BLK = 512
ROUTE_WIN = 128
VMEM_LIMIT = 56 * 1024 * 1024


def _cp(sem, vmem=VMEM_LIMIT):
    return pltpu.CompilerParams(dimension_semantics=sem, vmem_limit_bytes=vmem)


def _dot(a, b, prec=None):
    return jnp.dot(a, b, preferred_element_type=F32, precision=prec)


def _dot_nt(a, b, prec=None):
    return lax.dot_general(a, b, (((1,), (1,)), ((), ())), preferred_element_type=F32, precision=prec)


def _dot_tn(a, b, prec=None):
    return lax.dot_general(a, b, (((0,), (0,)), ((), ())), preferred_element_type=F32, precision=prec)


def _sigmoid(x):
    return 1.0 / (1.0 + jnp.exp(-x))


def _silu(x):
    return x * _sigmoid(x)


def _softplus(x):
    return jnp.maximum(x, 0.0) + jnp.log1p(jnp.exp(-jnp.abs(x)))


def _iota(shape, dim):
    return lax.broadcasted_iota(I32, shape, dim)


def _head_mean_sq(x, width):
    bd = (_iota((width, width), 0) // HD == _iota((width, width), 1) // HD).astype(F32) * (1.0 / HD)
    return _dot(x * x, bd, HI)


def _mod_body(c_ref, w_ref, b_ref, o_ref):
    o_ref[0] = _dot(_silu(c_ref[...]), w_ref[0], HI) + b_ref[0]


def _mod_call(c_all, w_mod, b_mod):
    bp = c_all.shape[0]
    return pl.pallas_call(
        _mod_body, name="adaln_mod",
        grid=(DEPTH, 6),
        in_specs=[pl.BlockSpec((bp, D), lambda l, j: (0, 0)),
                  pl.BlockSpec((1, D, D), lambda l, j: (l, 0, j)),
                  pl.BlockSpec((1, 1, D), lambda l, j: (l, 0, j))],
        out_specs=pl.BlockSpec((1, bp, D), lambda l, j: (l, 0, j)),
        out_shape=jax.ShapeDtypeStruct((DEPTH, bp, 6 * D), F32),
        compiler_params=_cp(("arbitrary", "arbitrary")),
    )(c_all, w_mod, b_mod.reshape(DEPTH, 1, 6 * D))


def _inproj_body(has_res, *refs):
    if has_res:
        (x_ref, mo_ref, g2_ref, g_ref, sc_ref, sh_ref, w_ref, wdt_ref,
         hg_ref, at_ref, ssd_ref, dtt_ref, xo_ref) = refs
        x = x_ref[0] + g2_ref[0] * mo_ref[0]
        xo_ref[0] = x
    else:
        x_ref, g_ref, sc_ref, sh_ref, w_ref, wdt_ref, hg_ref, at_ref, ssd_ref, dtt_ref = refs
        x = x_ref[0]
    ms = jnp.mean(x * x, axis=-1, keepdims=True)
    y = x * lax.rsqrt(ms + EPS) * g_ref[...]
    h = (y * (1.0 + sc_ref[0]) + sh_ref[0]).astype(BF16)
    p = _dot(h, w_ref[...])
    hg_ref[0] = p[:, :HG_COLS]
    at_ref[0] = p[:, HG_COLS:HG_COLS + AT_COLS]
    ssd_ref[0] = p[:, HG_COLS + AT_COLS:]
    dtt_ref[0] = _dot_nt(wdt_ref[...], h)


def _inproj_call(x, res, norm_g, scale, shift, w_main, w_dtt):
    b, l, _ = x.shape
    tm = min(l, 512)
    row = lambda: pl.BlockSpec((1, tm, D), lambda bi, i: (bi, i, 0))
    per_b = lambda: pl.BlockSpec((1, 1, D), lambda bi, i: (bi, 0, 0))
    in_specs = [row()]
    args = [x]
    if res is not None:
        in_specs += [row(), per_b()]
        args += [res[0], res[1]]
    in_specs += [pl.BlockSpec((1, D), lambda bi, i: (0, 0)), per_b(), per_b(),
                 pl.BlockSpec((D, IN_PAD), lambda bi, i: (0, 0)),
                 pl.BlockSpec((8, D), lambda bi, i: (0, 0))]
    args += [norm_g.reshape(1, D), scale, shift, w_main, w_dtt]
    out_specs = [pl.BlockSpec((1, tm, HG_COLS), lambda bi, i: (bi, i, 0)),
                 pl.BlockSpec((1, tm, AT_COLS), lambda bi, i: (bi, i, 0)),
                 pl.BlockSpec((1, tm, SSD_COLS), lambda bi, i: (bi, i, 0)),
                 pl.BlockSpec((1, 8, tm), lambda bi, i: (bi, 0, i))]
    out_shape = [jax.ShapeDtypeStruct((b, l, HG_COLS), F32),
                 jax.ShapeDtypeStruct((b, l, AT_COLS), F32),
                 jax.ShapeDtypeStruct((b, l, SSD_COLS), F32),
                 jax.ShapeDtypeStruct((b, 8, l), F32)]
    if res is not None:
        out_specs.append(row())
        out_shape.append(jax.ShapeDtypeStruct((b, l, D), F32))
    return pl.pallas_call(
        functools.partial(_inproj_body, res is not None), name="norm_inproj",
        grid=(b, l // tm), in_specs=in_specs, out_specs=out_specs, out_shape=out_shape,
        compiler_params=_cp(("arbitrary", "arbitrary")),
    )(*args)


def _hgrn_body(bwd, nch, *refs):
    if bwd:
        x_ref, lb_ref, ng_ref, of_ref, o_ref, st_ref = refs
    else:
        x_ref, lb_ref, o_ref, st_ref = refs

    @pl.when(pl.program_id(1) == 0)
    def _():
        st_ref[...] = jnp.zeros_like(st_ref)

    q_chunk = HG_CHUNK
    lane_head = _iota((1, HG_W), 1) // HD
    tril = (_iota((q_chunk, q_chunk), 1) <= _iota((q_chunk, q_chunk), 0)).astype(F32)
    bd_mask = _iota((HG_W, HG_W), 0) // HD == _iota((HG_W, HG_W), 1) // HD
    row_q = _iota((q_chunk, 1), 0)
    att_t = _iota((q_chunk, q_chunk), 0) % HG_SUB
    att_s = _iota((q_chunk, q_chunk), 1)
    log_lb = lb_ref[0:1, :]
    log_1m_lb = lb_ref[1:2, :]
    one_m_lb = lb_ref[2:3, :]
    fcol = 3 * HG_W if bwd else 2 * HG_W

    def chunk(ci, carry):
        c = (nch - 1 - ci) if bwd else ci
        r0 = pl.multiple_of(c * q_chunk, q_chunk)
        rows = pl.ds(r0, q_chunk)
        q = _silu(x_ref[0, rows, 0:HG_W])
        v = x_ref[0, rows, HG_W:2 * HG_W]
        fr = x_ref[0, rows, fcol:fcol + HG_W]
        a2 = log_1m_lb - _softplus(-fr)
        mx = jnp.maximum(log_lb, a2)
        logf = mx + jnp.log1p(jnp.exp(-jnp.abs(log_lb - a2)))
        k = one_m_lb * _sigmoid(-fr)
        b_inc = _dot(tril, logf, HI)
        b_exc = b_inc - logf
        total = b_inc[q_chunk - 1:q_chunk, :]
        st = st_ref[...]
        if bwd:
            q_in = q * jnp.exp(total - b_exc)
            k_st = k * jnp.exp(b_exc)
        else:
            q_in = q * jnp.exp(b_inc)
            k_st = k * jnp.exp(total - b_inc)
        inter = _dot_nt(q_in.astype(BF16), st.astype(BF16))
        upd = _dot_tn(v.astype(BF16), k_st.astype(BF16))
        st_ref[...] = st * jnp.exp(total) + jnp.where(bd_mask, upd, 0.0)
        v16 = v.astype(BF16)
        pieces = []
        for i in range(q_chunk // HG_SUB):
            lo, hi = i * HG_SUB, (i + 1) * HG_SUB
            if bwd:
                ref = b_inc[hi - 1:hi, :]
                qt = q[lo:hi] * jnp.exp(ref - b_exc[lo:hi])
                kt = k * jnp.exp(jnp.where(row_q >= lo, b_exc - ref, 0.0))
                kt = jnp.where(row_q >= lo, kt, 0.0)
                amask = att_s >= att_t + lo
            else:
                ref = b_exc[lo:lo + 1, :]
                qt = q[lo:hi] * jnp.exp(b_inc[lo:hi] - ref)
                kt = k * jnp.exp(jnp.where(row_q < hi, ref - b_inc, 0.0))
                kt = jnp.where(row_q < hi, kt, 0.0)
                amask = att_s <= att_t + lo
            q4 = jnp.concatenate([jnp.where(lane_head == h, qt, 0.0) for h in range(4)], axis=0)
            att = _dot_nt(q4.astype(BF16), kt.astype(BF16))
            att = jnp.where(amask, att, 0.0)
            r = _dot(att.astype(BF16), v16)
            o_i = jnp.where(lane_head == 0, r[0:HG_SUB], 0.0)
            for h in range(1, 4):
                o_i = o_i + jnp.where(lane_head == h, r[h * HG_SUB:(h + 1) * HG_SUB], 0.0)
            pieces.append(o_i + inter[lo:hi])
        o = jnp.concatenate(pieces, axis=0)
        if bwd:
            tot = of_ref[0, rows, :] + o
            ms = _head_mean_sq(tot, HG_W)
            g = x_ref[0, rows, 4 * HG_W:5 * HG_W]
            y = tot * lax.rsqrt(ms + EPS) * ng_ref[...] * _silu(g)
            o_ref[0, rows, :] = y.astype(o_ref.dtype)
        else:
            o_ref[0, rows, :] = o
        return carry

    lax.fori_loop(0, nch, chunk, 0)


def _hgrn_call(hg_in, lb_rows, norm_g, o_fwd):
    b, l, _ = hg_in.shape
    bwd = o_fwd is not None
    t = min(l, 512)
    nb = l // t
    idx = (lambda bi, j: (bi, nb - 1 - j, 0)) if bwd else (lambda bi, j: (bi, j, 0))
    in_specs = [pl.BlockSpec((1, t, HG_COLS), idx), pl.BlockSpec((3, HG_W), lambda bi, j: (0, 0))]
    args = [hg_in, lb_rows]
    if bwd:
        in_specs += [pl.BlockSpec((1, HG_W), lambda bi, j: (0, 0)), pl.BlockSpec((1, t, HG_W), idx)]
        args += [norm_g, o_fwd]
    return pl.pallas_call(
        functools.partial(_hgrn_body, bwd, t // HG_CHUNK), name="hgrn_bwd" if bwd else "hgrn_fwd",
        grid=(b, nb), in_specs=in_specs,
        out_specs=pl.BlockSpec((1, t, HG_W), idx),
        out_shape=jax.ShapeDtypeStruct((b, l, HG_W), BF16 if bwd else F32),
        scratch_shapes=[pltpu.VMEM((HG_W, HG_W), F32)],
        compiler_params=_cp(("arbitrary", "arbitrary")),
    )(*args)


def _rope_swap(x):
    w = x.shape[-1]
    first = (_iota((1, w), 1) % 32) < 16
    return jnp.where(first, pltpu.roll(x, w - 16, 1), pltpu.roll(x, 16, 1))


def _attn_prep_body(x_ref, cos_ref, sin_ref, gq_ref, gk_ref, ek_ref, ev_ref, q_ref, k_ref, v_ref):
    x = x_ref[0]
    cos2 = cos_ref[...]
    sin2 = sin_ref[...]
    xq = x[:, :AT_W]
    qn = xq * lax.rsqrt(_head_mean_sq(xq, AT_W) + EPS) * gq_ref[...]
    cos_q = jnp.concatenate([cos2] * 4, axis=1)
    sin_q = jnp.concatenate([sin2] * 4, axis=1)
    q_ref[0] = ((qn * cos_q + _rope_swap(qn) * sin_q) * (HD ** -0.5)).astype(BF16)
    xk = x[:, AT_W:AT_W + KV_W]
    kn = xk * lax.rsqrt(_head_mean_sq(xk, KV_W) + EPS) * gk_ref[...]
    kr = (kn * cos2 + _rope_swap(kn) * sin2).astype(BF16)
    k_ref[0] = _dot(kr, ek_ref[...]).astype(BF16)
    xv = x[:, AT_W + KV_W:].astype(BF16)
    ones_lane = ((_iota((1, 2 * LANES), 1) % LANES) >= HD).astype(F32)
    v_ref[0] = (_dot(xv, ev_ref[...]) + ones_lane).astype(BF16)


def _attn_prep_call(at_in, cos2, sin2, gq, gk, ek, ev):
    b, l, _ = at_in.shape
    tm = min(l, 512)
    const = lambda shape: pl.BlockSpec(shape, lambda bi, i: (0, 0))
    return pl.pallas_call(
        _attn_prep_body, name="attn_prep",
        grid=(b, l // tm),
        in_specs=[pl.BlockSpec((1, tm, AT_COLS), lambda bi, i: (bi, i, 0)),
                  pl.BlockSpec((tm, KV_W), lambda bi, i: (i, 0)),
                  pl.BlockSpec((tm, KV_W), lambda bi, i: (i, 0)),
                  const((1, AT_W)), const((1, KV_W)), const((KV_W, AT_W)), const((KV_W, 2 * LANES))],
        out_specs=[pl.BlockSpec((1, tm, AT_W), lambda bi, i: (bi, i, 0)),
                   pl.BlockSpec((1, tm, AT_W), lambda bi, i: (bi, i, 0)),
                   pl.BlockSpec((1, tm, 2 * LANES), lambda bi, i: (bi, i, 0))],
        out_shape=[jax.ShapeDtypeStruct((b, l, AT_W), BF16),
                   jax.ShapeDtypeStruct((b, l, AT_W), BF16),
                   jax.ShapeDtypeStruct((b, l, 2 * LANES), BF16)],
        compiler_params=_cp(("arbitrary", "arbitrary")),
    )(at_in, cos2, sin2, gq, gk, ek, ev)


def _attn_body(tq, q_ref, k_ref, v_ref, o_ref, qm_ref, m_ref, acc_ref):
    kk = pl.program_id(3)
    lane_head = _iota((1, 4 * HD), 1) // HD

    @pl.when(kk == 0)
    def _():
        q = q_ref[0]
        for g in range(AT_GROUP):
            qm_ref[g * tq:(g + 1) * tq, :] = jnp.where(lane_head == g, q, jnp.zeros_like(q))
        m_ref[...] = jnp.full_like(m_ref, -jnp.inf)
        acc_ref[...] = jnp.zeros_like(acc_ref)

    s = _dot_nt(qm_ref[...], k_ref[0])
    m_old = m_ref[...]
    m_new = jnp.maximum(m_old, jnp.max(s, axis=-1, keepdims=True))
    alpha = jnp.exp(m_old - m_new)
    p = jnp.exp(s - m_new)
    acc_ref[...] = acc_ref[...] * alpha + _dot(p.astype(BF16), v_ref[0])
    m_ref[...] = m_new

    @pl.when(kk == pl.num_programs(3) - 1)
    def _():
        acc = acc_ref[...]
        o = acc / pltpu.roll(acc, HD, 1)
        low = _iota((1, LANES), 1) < HD
        halves = []
        for pair in range(2):
            a = o[(2 * pair) * tq:(2 * pair + 1) * tq]
            c = o[(2 * pair + 1) * tq:(2 * pair + 2) * tq]
            halves.append(jnp.where(low, a, pltpu.roll(c, HD, 1)))
        o_ref[0] = jnp.concatenate(halves, axis=1).astype(o_ref.dtype)


def _attn_call(q, kt, vp):
    b, l, _ = q.shape
    tq = min(l, 256)
    tk = min(l, 512)
    return pl.pallas_call(
        functools.partial(_attn_body, tq), name="flash_attn",
        grid=(b, 2, l // tq, l // tk),
        in_specs=[pl.BlockSpec((1, tq, 4 * HD), lambda bi, j, i, kk: (bi, i, j)),
                  pl.BlockSpec((1, tk, 4 * HD), lambda bi, j, i, kk: (bi, kk, j)),
                  pl.BlockSpec((1, tk, LANES), lambda bi, j, i, kk: (bi, kk, j))],
        out_specs=pl.BlockSpec((1, tq, 4 * HD), lambda bi, j, i, kk: (bi, i, j)),
        out_shape=jax.ShapeDtypeStruct((b, l, AT_W), BF16),
        scratch_shapes=[pltpu.VMEM((AT_GROUP * tq, 4 * HD), BF16),
                        pltpu.VMEM((AT_GROUP * tq, 1), F32),
                        pltpu.VMEM((AT_GROUP * tq, LANES), F32)],
        compiler_params=_cp(("arbitrary",) * 4),
    )(q, kt, vp)


def _ssd_conv_body(t, x_ref, prev_ref, next_ref, w_ref, b_ref, o_ref, ext_ref):
    j = pl.program_id(1)
    nb = pl.num_programs(1)
    lo, hi = SSD_W, SSD_W + SSD_XBC
    ext_ref[0:8, :] = jnp.where(j > 0, prev_ref[0, :, lo:hi], 0.0)
    ext_ref[8:8 + t, :] = x_ref[0, :, lo:hi]
    ext_ref[8 + t:16 + t, :] = jnp.where(j < nb - 1, next_ref[0, :, lo:hi], 0.0)
    pad = SSD_CONV // 2
    acc = b_ref[...] + w_ref[0:1, :] * ext_ref[pl.ds(8 - pad, t), :]
    for kk in range(1, SSD_CONV):
        acc = acc + w_ref[kk:kk + 1, :] * ext_ref[pl.ds(8 - pad + kk, t), :]
    o_ref[0] = _silu(acc)


def _ssd_conv_call(ssd_in, conv_w, conv_b):
    b, l, _ = ssd_in.shape
    t = min(l, 512)
    t8 = t // 8
    last8 = l // 8 - 1
    return pl.pallas_call(
        functools.partial(_ssd_conv_body, t), name="ssd_conv",
        grid=(b, l // t),
        in_specs=[pl.BlockSpec((1, t, SSD_COLS), lambda bi, j: (bi, j, 0)),
                  pl.BlockSpec((1, 8, SSD_COLS), lambda bi, j: (bi, jnp.maximum(j * t8 - 1, 0), 0)),
                  pl.BlockSpec((1, 8, SSD_COLS), lambda bi, j: (bi, jnp.minimum((j + 1) * t8, last8), 0)),
                  pl.BlockSpec((8, SSD_XBC), lambda bi, j: (0, 0)),
                  pl.BlockSpec((1, SSD_XBC), lambda bi, j: (0, 0))],
        out_specs=pl.BlockSpec((1, t, SSD_XBC), lambda bi, j: (bi, j, 0)),
        out_shape=jax.ShapeDtypeStruct((b, l, SSD_XBC), F32),
        scratch_shapes=[pltpu.VMEM((t + 16, SSD_XBC), F32)],
        compiler_params=_cp(("arbitrary", "arbitrary")),
    )(ssd_in, ssd_in, ssd_in, conv_w, conv_b)


def _ssd_body(bwd, nch, *refs):
    if bwd:
        xbc_ref, dtc_ref, dtr_ref, par_ref, parc_ref, z_ref, yf_ref, ng_ref, o_ref, st_ref = refs
    else:
        xbc_ref, dtc_ref, dtr_ref, par_ref, parc_ref, o_ref, st_ref = refs

    @pl.when(pl.program_id(1) == 0)
    def _():
        st_ref[...] = jnp.zeros_like(st_ref)

    qc = SSD_CHUNK
    dsel = 4 if bwd else 0
    lane_head = _iota((1, SSD_W), 1) // HD
    lane_grp = _iota((1, LANES), 1) // HD
    tril = (_iota((qc, qc), 1) <= _iota((qc, qc), 0)).astype(F32)
    triu = (_iota((qc, qc), 0) <= _iota((qc, qc), 1)).astype(F32)
    tt = _iota((qc, qc), 0)
    ss = _iota((qc, qc), 1)
    st_mask = (_iota((SSD_W, LANES), 0) // (2 * HD)) == (_iota((SSD_W, LANES), 1) // HD)
    bias_row = par_ref[0:1, 0:LANES]
    acoef_row = par_ref[1:2, 0:LANES]
    dskip_row = par_ref[2:3, :]
    bias_col = parc_ref[:, 0:1]
    acoef_col = parc_ref[:, 1:2]

    def expand(col_vals):
        out = jnp.broadcast_to(col_vals[:, dsel:dsel + 1], (qc, SSD_W))
        for h in range(1, 4):
            out = jnp.where(lane_head == h, jnp.broadcast_to(col_vals[:, dsel + h:dsel + h + 1], (qc, SSD_W)), out)
        return out

    def chunk(ci, carry):
        c = (nch - 1 - ci) if bwd else ci
        r0 = pl.multiple_of(c * qc, qc)
        rows = pl.ds(r0, qc)
        xs = xbc_ref[0, rows, 0:SSD_W]
        bm = xbc_ref[0, rows, SSD_W:SSD_W + LANES]
        cm = xbc_ref[0, rows, SSD_W + LANES:SSD_W + 2 * LANES]
        dt_c = _softplus(dtc_ref[0, rows, :] + bias_row)
        a_c = dt_c * acoef_row
        dt_r = _softplus(dtr_ref[0, :, rows] + bias_col)
        a_r = dt_r * acoef_col
        inc_c = _dot(tril, a_c, HI)
        inc_r = _dot(a_r, triu, HI)
        if bwd:
            cum_c, cum_r = inc_c - a_c, inc_r - a_r
        else:
            cum_c, cum_r = inc_c, inc_r
        total_c = inc_c[qc - 1:qc, :]
        xdt = xs * expand(dt_c)
        xdt16 = xdt.astype(BF16)
        bm16 = bm.astype(BF16)
        cm16 = cm.astype(BF16)
        gmat = [_dot_nt(jnp.where(lane_grp == g, cm, 0.0).astype(BF16), bm16) for g in range(2)]
        y = jnp.zeros((qc, SSD_W), F32)
        for h in range(4):
            col = cum_c[:, dsel + h:dsel + h + 1]
            rw = cum_r[dsel + h:dsel + h + 1, :]
            if bwd:
                dec = jnp.where(ss >= tt, jnp.exp(jnp.where(ss >= tt, rw - col, 0.0)), 0.0)
            else:
                dec = jnp.where(ss <= tt, jnp.exp(jnp.where(ss <= tt, col - rw, 0.0)), 0.0)
            yh = _dot((gmat[h // 2] * dec).astype(BF16), xdt16)
            y = jnp.where(lane_head == h, yh, y)
        st = st_ref[...]
        if bwd:
            out_dec = jnp.exp(total_c - cum_c)
            st_dec = jnp.exp(cum_c)
        else:
            out_dec = jnp.exp(cum_c)
            st_dec = jnp.exp(total_c - cum_c)
        y = y + _dot_nt(cm16, st.astype(BF16)) * expand(out_dec)
        upd = _dot_tn((xdt * expand(st_dec)).astype(BF16), bm16)
        tot_rows = jnp.broadcast_to(jnp.exp(total_c[:, dsel:dsel + 1]), (HD, LANES))
        decay_rows = jnp.concatenate(
            [tot_rows] + [jnp.broadcast_to(jnp.exp(total_c[:, dsel + h:dsel + h + 1]), (HD, LANES))
                          for h in range(1, 4)], axis=0)
        st_ref[...] = st * decay_rows + jnp.where(st_mask, upd, 0.0)
        if bwd:
            yy = yf_ref[0, rows, :] + y + xs * dskip_row
            yy = yy * _silu(z_ref[0, rows, :])
            ms = _head_mean_sq(yy, SSD_W)
            o_ref[0, rows, :] = (yy * lax.rsqrt(ms + EPS) * ng_ref[...]).astype(o_ref.dtype)
        else:
            o_ref[0, rows, :] = y
        return carry

    lax.fori_loop(0, nch, chunk, 0)


def _ssd_call(xbc, ssd_in, dtt, par_rows, par_cols, norm_g, y_fwd):
    b, l, _ = xbc.shape
    bwd = y_fwd is not None
    t = min(l, 512)
    nb = l // t
    blk = (lambda bi, j: (bi, nb - 1 - j)) if bwd else (lambda bi, j: (bi, j))
    rows3 = lambda lane_blk: (lambda bi, j: blk(bi, j) + (lane_blk,))
    in_specs = [pl.BlockSpec((1, t, SSD_XBC), rows3(0)),
                pl.BlockSpec((1, t, LANES), rows3((SSD_W + SSD_XBC) // LANES)),
                pl.BlockSpec((1, 8, t), lambda bi, j: (bi, 0, blk(bi, j)[1])),
                pl.BlockSpec((8, SSD_W), lambda bi, j: (0, 0)),
                pl.BlockSpec((8, LANES), lambda bi, j: (0, 0))]
    args = [xbc, ssd_in, dtt, par_rows, par_cols]
    if bwd:
        in_specs += [pl.BlockSpec((1, t, SSD_W), rows3(0)),
                     pl.BlockSpec((1, t, SSD_W), rows3(0)),
                     pl.BlockSpec((1, SSD_W), lambda bi, j: (0, 0))]
        args += [ssd_in, y_fwd, norm_g]
    return pl.pallas_call(
        functools.partial(_ssd_body, bwd, t // SSD_CHUNK), name="ssd_bwd" if bwd else "ssd_fwd",
        grid=(b, nb), in_specs=in_specs,
        out_specs=pl.BlockSpec((1, t, SSD_W), rows3(0)),
        out_shape=jax.ShapeDtypeStruct((b, l, SSD_W), BF16 if bwd else F32),
        scratch_shapes=[pltpu.VMEM((SSD_W, LANES), F32)],
        compiler_params=_cp(("arbitrary", "arbitrary")),
    )(*args)


def _outproj_body(x_ref, hg_ref, at_ref, ssd_ref, w1_ref, w2_ref, w3_ref, g1_ref, ng_ref, sc_ref, sh_ref,
                  wr_ref, wrt_ref, x1_ref, h2_ref, aff_ref, afft_ref):
    mix = _dot(hg_ref[0], w1_ref[...]) + _dot(at_ref[0], w2_ref[...]) + _dot(ssd_ref[0], w3_ref[...])
    x1 = x_ref[0] + g1_ref[0] * mix
    x1_ref[0] = x1
    ms = jnp.mean(x1 * x1, axis=-1, keepdims=True)
    h2 = x1 * lax.rsqrt(ms + EPS) * ng_ref[...] * (1.0 + sc_ref[0]) + sh_ref[0]
    h2_ref[0] = h2
    logits = _dot(h2, wr_ref[...], HI)
    valid = _iota((1, LANES), 1) < N_EXPERTS
    logits = jnp.where(valid, logits, -jnp.inf)
    e = jnp.exp(logits - jnp.max(logits, axis=-1, keepdims=True))
    aff_ref[0] = e / jnp.sum(e, axis=-1, keepdims=True)
    lt = _dot_nt(wrt_ref[...], h2, HI)
    et = jnp.exp(lt - jnp.max(lt, axis=0, keepdims=True))
    afft_ref[...] = et / jnp.sum(et, axis=0, keepdims=True)


def _outproj_call(x, o_hg, o_at, o_ssd, w1, w2, w3, gate1, norm_g, scale, shift, w_r, w_rt):
    b, l, _ = x.shape
    tm = min(l, 512)
    nbl = l // tm
    row = lambda w: pl.BlockSpec((1, tm, w), lambda bi, i: (bi, i, 0))
    per_b = lambda: pl.BlockSpec((1, 1, D), lambda bi, i: (bi, 0, 0))
    const = lambda shape: pl.BlockSpec(shape, lambda bi, i: (0, 0))
    return pl.pallas_call(
        _outproj_body, name="outproj_router",
        grid=(b, nbl),
        in_specs=[row(D), row(HG_W), row(AT_W), row(SSD_W), const((HG_W, D)), const((AT_W, D)), const((SSD_W, D)),
                  per_b(), const((1, D)), per_b(), per_b(), const((D, LANES)), const((N_EXPERTS, D))],
        out_specs=[row(D), row(D), row(LANES),
                   pl.BlockSpec((N_EXPERTS, tm), lambda bi, i: (0, bi * nbl + i))],
        out_shape=[jax.ShapeDtypeStruct((b, l, D), F32), jax.ShapeDtypeStruct((b, l, D), F32),
                   jax.ShapeDtypeStruct((b, l, LANES), F32), jax.ShapeDtypeStruct((N_EXPERTS, b * l), F32)],
        compiler_params=_cp(("arbitrary", "arbitrary")),
    )(x, o_hg, o_at, o_ssd, w1, w2, w3, gate1, norm_g.reshape(1, D), scale, shift, w_r, w_rt)


def _route_thr_body(cap, n_tok, afft_ref, o_ref):
    lane_chunk = min(n_tok, 2048)
    n_chunks = n_tok // lane_chunk

    def count_ge(cand):
        def body(i, acc):
            bits = lax.bitcast_convert_type(afft_ref[:, pl.ds(pl.multiple_of(i * lane_chunk, lane_chunk), lane_chunk)], I32)
            return acc + (bits >= cand).astype(F32)
        acc = lax.fori_loop(0, n_chunks, body, jnp.zeros((N_EXPERTS, lane_chunk), F32))
        return jnp.sum(acc, axis=1, keepdims=True)

    def bit_step(i, thr):
        cand = thr | jnp.left_shift(jnp.int32(1), 30 - i)
        return jnp.where(count_ge(cand) >= cap, cand, thr)

    thr = lax.fori_loop(0, 31, bit_step, jnp.zeros((N_EXPERTS, 1), I32))
    n_gt = count_ge(thr + 1)
    need = cap - n_gt

    nblk = n_tok // ROUTE_BLK
    lane = _iota((1, LANES), 1)

    def blk_counts(j, carry):
        gt_tab, eq_tab = carry
        bits = lax.bitcast_convert_type(afft_ref[:, pl.ds(pl.multiple_of(j * ROUTE_BLK, ROUTE_BLK), ROUTE_BLK)], I32)
        cg = jnp.sum((bits > thr).astype(F32), axis=1, keepdims=True)
        ce = jnp.sum((bits == thr).astype(F32), axis=1, keepdims=True)
        return (gt_tab + jnp.where(lane == j, cg, 0.0), eq_tab + jnp.where(lane == j, ce, 0.0))

    zeros = jnp.zeros((N_EXPERTS, LANES), F32)
    gt_tab, eq_tab = lax.fori_loop(0, nblk, blk_counts, (zeros, zeros))
    strict = (_iota((LANES, LANES), 0) < _iota((LANES, LANES), 1)).astype(F32)
    eq_before = _dot(eq_tab, strict, HI)
    eq_take = jnp.clip(need - eq_before, 0.0, eq_tab)
    cnt = gt_tab + eq_take
    cnt8 = jnp.floor((cnt + 7.0) * 0.125) * 8.0
    off = _dot(cnt8, strict, HI)
    o_ref[0] = jnp.broadcast_to(thr, (N_EXPERTS, LANES))
    o_ref[1] = jnp.broadcast_to(need.astype(I32), (N_EXPERTS, LANES))
    o_ref[2] = eq_before.astype(I32)
    o_ref[3] = cnt.astype(I32)
    o_ref[4] = off.astype(I32)


def _route_thr_call(afft, cap):
    n_tok = afft.shape[1]
    return pl.pallas_call(
        functools.partial(_route_thr_body, cap, n_tok), name="route_threshold",
        grid=(1,),
        in_specs=[pl.BlockSpec((N_EXPERTS, n_tok), lambda i: (0, 0))],
        out_specs=pl.BlockSpec((5, N_EXPERTS, LANES), lambda i: (0, 0, 0)),
        out_shape=jax.ShapeDtypeStruct((5, N_EXPERTS, LANES), I32),
        compiler_params=_cp(("arbitrary",)),
    )(afft)


def _route_list_body(nblk, thr_s, need_s, eqb_s, cnt_s, off_s, afft_ref, aff_ref, tri_ref, o_ref):
    e = pl.program_id(0)
    j = pl.program_id(1)

    @pl.when(j == 0)
    def _():
        o_ref[...] = jnp.zeros_like(o_ref)

    t = ROUTE_BLK
    w = ROUTE_WIN
    k = e * nblk + j
    thr = thr_s[e]
    bits = lax.bitcast_convert_type(afft_ref[pl.ds(e, 1), :], I32)
    gt = bits > thr
    eq = bits == thr
    tri = tri_ref[...]
    eq_rank = _dot(jnp.broadcast_to(eq.astype(BF16), (8, t)), tri)[0:1] + jnp.full((1, t), eqb_s[k], I32).astype(F32)
    sel = gt | (eq & (eq_rank < jnp.full((1, t), need_s[e], I32).astype(F32)))
    rank = _dot(jnp.broadcast_to(sel.astype(BF16), (8, t)), tri)[0:1]
    lane = _iota((1, LANES), 1)
    tok_id = (_iota((t, LANES), 0) + (j * t + 1)).astype(F32)
    aff = aff_ref[...]
    n_win = (cnt_s[k] + (w - 1)) // w

    def window(wi, carry):
        slot = (_iota((w, 1), 0) + wi * w).astype(F32)
        onehot = (sel & (rank == slot)).astype(F32)
        ids = _dot(onehot, tok_id, HI)
        gates = jnp.sum(jnp.where(lane == e, _dot(onehot, aff, HI), 0.0), axis=1, keepdims=True)
        packed = jnp.where(lane == 0, ids, jnp.where(lane == 1, gates, 0.0))
        start = pl.multiple_of(off_s[k] + wi * w, 8)
        o_ref[0, pl.ds(start, w), :] = packed
        return carry

    lax.fori_loop(0, n_win, window, 0)


def _route_list_call(afft, aff2d, tri, tables, slots_alloc):
    n_tok = afft.shape[1]
    nblk = n_tok // ROUTE_BLK
    thr = tables[0, :, 0]
    need = tables[1, :, 0]
    eqb = tables[2, :, :nblk].reshape(-1)
    cnt = tables[3, :, :nblk].reshape(-1)
    off = tables[4, :, :nblk].reshape(-1)
    grid_spec = pltpu.PrefetchScalarGridSpec(
        num_scalar_prefetch=5,
        grid=(N_EXPERTS, nblk),
        in_specs=[pl.BlockSpec((N_EXPERTS, ROUTE_BLK), lambda e, j, *_: (0, j)),
                  pl.BlockSpec((ROUTE_BLK, LANES), lambda e, j, *_: (j, 0)),
                  pl.BlockSpec((ROUTE_BLK, ROUTE_BLK), lambda e, j, *_: (0, 0))],
        out_specs=pl.BlockSpec((1, slots_alloc, LANES), lambda e, j, *_: (e, 0, 0)),
    )
    return pl.pallas_call(
        functools.partial(_route_list_body, nblk), name="route_lists",
        grid_spec=grid_spec,
        out_shape=jax.ShapeDtypeStruct((N_EXPERTS, slots_alloc, LANES), F32),
        compiler_params=_cp(("arbitrary", "arbitrary")),
    )(thr, need, eqb, cnt, off, afft, aff2d, tri)


def _moe_body(idx_ref, lst_ref, wg_ref, wu_ref, wd_ref, h2_hbm, acc_in_hbm, acc_hbm, xbuf, obuf, sems):
    del acc_in_hbm
    ts = SLOT_TILE

    @pl.when((pl.program_id(0) == 0) & (pl.program_id(1) == 0))
    def _():
        xbuf[...] = jnp.zeros_like(xbuf)
        obuf[...] = jnp.zeros_like(obuf)

    def gather(s):
        n = idx_ref[0, 0, s]
        return n, (pltpu.make_async_copy(h2_hbm.at[pl.ds(n, 1)], xbuf.at[pl.ds(s, 1)], sems.at[0]),
                   pltpu.make_async_copy(acc_hbm.at[pl.ds(n, 1)], obuf.at[pl.ds(s, 1)], sems.at[1]))

    def scatter(s):
        n = idx_ref[0, 0, s]
        return n, pltpu.make_async_copy(obuf.at[pl.ds(s, 1)], acc_hbm.at[pl.ds(n, 1)], sems.at[2])

    def start_gather(s, c):
        n, copies = gather(s)

        @pl.when(n >= 0)
        def _():
            for cp in copies:
                cp.start()
        return c

    def wait_gather(s, c):
        n, copies = gather(s)

        @pl.when(n >= 0)
        def _():
            for cp in copies:
                cp.wait()
        return c

    lax.fori_loop(0, ts, start_gather, 0)
    lax.fori_loop(0, ts, wait_gather, 0)

    lst = lst_ref[0]
    valid = lst[:, 0:1] > 0.5
    gate = lst[:, 1:2]
    xg = xbuf[...].astype(BF16)
    hid = _silu(_dot(xg, wg_ref[0])) * _dot(xg, wu_ref[0])
    y = _dot(hid.astype(BF16), wd_ref[0])
    obuf[...] = obuf[...] + jnp.where(valid, y * gate, 0.0)

    def start_scatter(s, c):
        n, cp = scatter(s)

        @pl.when(n >= 0)
        def _():
            cp.start()
        return c

    def wait_scatter(s, c):
        n, cp = scatter(s)

        @pl.when(n >= 0)
        def _():
            cp.wait()
        return c

    lax.fori_loop(0, ts, start_scatter, 0)
    lax.fori_loop(0, ts, wait_scatter, 0)


def _moe_call(idx3, lists, wg, wu, wd, h2_flat, n_tiles):
    n_tok = h2_flat.shape[0]
    acc0 = jnp.zeros((n_tok, D), F32)
    wspec = lambda: pl.BlockSpec((1, D, D), lambda e, i: (e, 0, 0))
    return pl.pallas_call(
        _moe_body, name="expert_ffn",
        grid=(N_EXPERTS, n_tiles),
        in_specs=[pl.BlockSpec((1, 1, SLOT_TILE), lambda e, i: (e * n_tiles + i, 0, 0), memory_space=pltpu.SMEM),
                  pl.BlockSpec((1, SLOT_TILE, LANES), lambda e, i: (e, i, 0)),
                  wspec(), wspec(), wspec(),
                  pl.BlockSpec(memory_space=pl.ANY), pl.BlockSpec(memory_space=pl.ANY)],
        out_specs=pl.BlockSpec(memory_space=pl.ANY),
        out_shape=jax.ShapeDtypeStruct((n_tok, D), F32),
        input_output_aliases={6: 0},
        scratch_shapes=[pltpu.VMEM((SLOT_TILE, D), F32), pltpu.VMEM((SLOT_TILE, D), F32),
                        pltpu.SemaphoreType.DMA((3,))],
        compiler_params=_cp(("arbitrary", "arbitrary")),
    )(idx3, lists, wg, wu, wd, h2_flat, acc0)


def _final_body(x_ref, mo_ref, g2_ref, g_ref, o_ref):
    x = x_ref[0] + g2_ref[0] * mo_ref[0]
    ms = jnp.mean(x * x, axis=-1, keepdims=True)
    o_ref[0] = x * lax.rsqrt(ms + EPS) * g_ref[...]


def _final_call(x1, moe_out, gate2, final_g):
    b, l, _ = x1.shape
    tm = min(l, 512)
    row = lambda: pl.BlockSpec((1, tm, D), lambda bi, i: (bi, i, 0))
    return pl.pallas_call(
        _final_body, name="final_norm",
        grid=(b, l // tm),
        in_specs=[row(), row(), pl.BlockSpec((1, 1, D), lambda bi, i: (bi, 0, 0)),
                  pl.BlockSpec((1, D), lambda bi, i: (0, 0))],
        out_specs=row(),
        out_shape=jax.ShapeDtypeStruct((b, l, D), F32),
        compiler_params=_cp(("arbitrary", "arbitrary")),
    )(x1, moe_out, gate2, final_g.reshape(1, D))


def _rope_tables(l):
    quarter = HD // 4
    inv = ROPE_THETA ** (-jnp.arange(quarter, dtype=F32) / quarter)
    t = jnp.arange(l)
    pos = jnp.stack([(t // GRID_W).astype(F32), (t % GRID_W).astype(F32)], axis=1)
    lane = np.arange(KV_W)
    which = (lane % HD) // (HD // 2)
    ang = pos[:, which] * inv[lane % quarter][None, :]
    sign = np.where((lane % (HD // 2)) < quarter, -1.0, 1.0).astype(np.float32)
    return jnp.cos(ang), jnp.sin(ang) * sign[None, :]


def _prep_layer(l, p):
    w_in = p['w_in'][l]
    n_main = HG_COLS + AT_COLS + SSD_W + SSD_XBC
    w_main = jnp.concatenate([w_in[:, :n_main], jnp.pad(w_in[:, n_main:], ((0, 0), (0, 120)))], axis=1).astype(BF16)
    w_dtt = w_in[:, n_main:].T.astype(BF16)
    lb_all = jnp.cumsum(jax.nn.softmax(p['hg_lb'].astype(F32), axis=0), axis=0)
    lb = (lb_all - lb_all[:1])[l]
    lb_rows = jnp.stack([jnp.log(lb), jnp.log1p(-lb), 1.0 - lb], axis=1)
    a_coef = -jnp.exp(p['a_log'][l].astype(F32)).reshape(-1)
    dt_bias = p['dt_bias'][l].astype(F32).reshape(-1)
    par_rows = jnp.zeros((8, SSD_W), F32).at[0, :8].set(dt_bias).at[1, :8].set(a_coef)
    par_rows = par_rows.at[2, :].set(jnp.repeat(p['d_skip'][l].astype(F32), HD))
    par_cols = jnp.zeros((8, LANES), F32).at[:, 0].set(dt_bias).at[:, 1].set(a_coef)
    w_out = p['w_out'][l].astype(BF16)
    lane_k = np.arange(KV_W)
    ek = np.zeros((KV_W, AT_W), np.float32)
    for rep in range(AT_GROUP):
        ek[lane_k, (lane_k // HD) * (AT_GROUP * HD) + rep * HD + lane_k % HD] = 1.0
    ev = np.zeros((KV_W, 2 * LANES), np.float32)
    ev[lane_k, (lane_k // HD) * LANES + lane_k % HD] = 1.0
    return dict(
        w_main=w_main, w_dtt=w_dtt, lb_rows=lb_rows,
        hg_norm=jnp.tile(p['hg_norm_g'][l], 4).reshape(1, HG_W),
        gq=jnp.tile(p['q_norm_g'][l], 8).reshape(1, AT_W), gk=jnp.tile(p['k_norm_g'][l], 2).reshape(1, KV_W),
        ek=jnp.asarray(ek, BF16), ev=jnp.asarray(ev, BF16),
        conv_w=jnp.pad(p['conv_w'][l], ((0, 8 - SSD_CONV), (0, 0))), conv_b=p['conv_b'][l].reshape(1, SSD_XBC),
        par_rows=par_rows, par_cols=par_cols,
        ssd_norm=jnp.tile(p['ssd_norm_g'][l], 4).reshape(1, SSD_W),
        w_o1=w_out[:HG_W], w_o2=w_out[HG_W:HG_W + AT_W], w_o3=w_out[HG_W + AT_W:],
        w_r=jnp.pad(p['w_router'][l], ((0, 0), (0, LANES - N_EXPERTS))), w_rt=p['w_router'][l].T,
        wg=p['w_gate'][l].astype(BF16), wu=p['w_up'][l].astype(BF16), wd=p['w_down'][l].astype(BF16),
    )


def _trunk(x, mod, layers, p):
    b, l, _ = x.shape
    n_tok = b * l
    cap = EC_CAPACITY * n_tok // N_EXPERTS
    nblk = n_tok // ROUTE_BLK
    slots = cap + 8 * nblk
    n_tiles = -(-slots // SLOT_TILE)
    slots_alloc = -(-(n_tiles * SLOT_TILE + ROUTE_WIN) // SLOT_TILE) * SLOT_TILE
    cos2, sin2 = _rope_tables(l)
    tri = (np.arange(ROUTE_BLK)[:, None] < np.arange(ROUTE_BLK)[None, :]).astype(np.float32)
    tri = jnp.asarray(tri, BF16)
    res = None
    for li in range(DEPTH):
        w = layers[li]
        sh1, sc1, g1, sh2, sc2, g2 = [mod[li, :, i * D:(i + 1) * D].reshape(b, 1, D) for i in range(6)]
        outs = _inproj_call(x, res, p['norm1_g'][li], sc1, sh1, w['w_main'], w['w_dtt'])
        hg_in, at_in, ssd_in, dtt = outs[:4]
        if res is not None:
            x = outs[4]
        o_f = _hgrn_call(hg_in, w['lb_rows'][0], w['hg_norm'], None)
        o_hg = _hgrn_call(hg_in, w['lb_rows'][1], w['hg_norm'], o_f)
        q, kt, vp = _attn_prep_call(at_in, cos2, sin2, w['gq'], w['gk'], w['ek'], w['ev'])
        o_at = _attn_call(q, kt, vp)
        xbc = _ssd_conv_call(ssd_in, w['conv_w'], w['conv_b'])
        y_f = _ssd_call(xbc, ssd_in, dtt, w['par_rows'], w['par_cols'], w['ssd_norm'], None)
        o_ssd = _ssd_call(xbc, ssd_in, dtt, w['par_rows'], w['par_cols'], w['ssd_norm'], y_f)
        x1, h2, aff, afft = _outproj_call(x, o_hg, o_at, o_ssd, w['w_o1'], w['w_o2'], w['w_o3'], g1,
                                          p['norm2_g'][li], sc2, sh2, w['w_r'], w['w_rt'])
        tables = _route_thr_call(afft, cap)
        lists = _route_list_call(afft, aff.reshape(n_tok, LANES), tri, tables, slots_alloc)
        idx3 = (lists[:, :n_tiles * SLOT_TILE, 0].astype(I32) - 1).reshape(N_EXPERTS * n_tiles, 1, SLOT_TILE)
        moe = _moe_call(idx3, lists, w['wg'], w['wu'], w['wd'], h2.reshape(n_tok, D), n_tiles)
        x = x1
        res = (moe.reshape(b, l, D), g2)
    return _final_call(x, res[0], res[1], p['final_g'])


def kernel(x_prompt, x_sample, c_prompt, c_sample, norm1_g, norm2_g, w_mod, b_mod, w_in, hg_lb, hg_norm_g,
           q_norm_g, k_norm_g, conv_w, conv_b, a_log, dt_bias, d_skip, ssd_norm_g, w_out, w_router,
           w_gate, w_up, w_down, final_g):
    p = dict(norm1_g=norm1_g, norm2_g=norm2_g, w_in=w_in, hg_lb=hg_lb, hg_norm_g=hg_norm_g, q_norm_g=q_norm_g,
             k_norm_g=k_norm_g, conv_w=conv_w, conv_b=conv_b, a_log=a_log, dt_bias=dt_bias, d_skip=d_skip,
             ssd_norm_g=ssd_norm_g, w_out=w_out, w_router=w_router, w_gate=w_gate, w_up=w_up, w_down=w_down,
             final_g=final_g)
    bp, bs = c_prompt.shape[0], c_sample.shape[0]
    rows = -(-(bp + bs) // 8) * 8
    c_all = jnp.pad(jnp.concatenate([c_prompt, c_sample], axis=0), ((0, rows - bp - bs), (0, 0)))
    mod = _mod_call(c_all, w_mod, b_mod)
    layers = [_prep_layer(li, p) for li in range(DEPTH)]
    y_prompt = _trunk(x_prompt, mod[:, :bp], layers, p)
    y_sample = _trunk(x_sample, mod[:, bp:bp + bs], layers, p)
    return (y_prompt, y_sample)
```

```python
import functools

import numpy as np
import jax
import jax.numpy as jnp
from jax import lax
from jax.experimental import pallas as pl
from jax.experimental.pallas import tpu as pltpu

F32 = jnp.float32
BF16 = jnp.bfloat16
I32 = jnp.int32
HI = lax.Precision.HIGHEST

D = 1024
DEPTH = 2
GRID_W = 64
EPS = 1e-6
HD = 64
HG_W = 256
HG_CHUNK = 64
HG_SUB = 16
AT_W = 512
KV_W = 128
AT_GROUP = 4
ROPE_THETA = 10000.0
QK_SCALE_LOG2 = (HD ** -0.5) * 1.4426950408889634
SSD_W = 256
SSD_XBC = 512
SSD_CHUNK = 128
SSD_CONV = 5
N_EXPERTS = 16
EC_CAPACITY = 2
HG_COLS = 5 * HG_W
AT_COLS = AT_W + 2 * KV_W
SSD_COLS = SSD_W + SSD_XBC + 128
IN_PAD = HG_COLS + AT_COLS + SSD_COLS
LANES = 128
SLOT_TILE = 256
ROUTE_BLK = 512
ROUTE_WIN = 128
VMEM_LIMIT = 56 * 1024 * 1024


def _cp(sem, vmem=VMEM_LIMIT):
    return pltpu.CompilerParams(dimension_semantics=sem, vmem_limit_bytes=vmem)


def _dot(a, b, prec=None):
    return jnp.dot(a, b, preferred_element_type=F32, precision=prec)


def _dot_nt(a, b, prec=None):
    return lax.dot_general(a, b, (((1,), (1,)), ((), ())), preferred_element_type=F32, precision=prec)


def _dot_tn(a, b, prec=None):
    return lax.dot_general(a, b, (((0,), (0,)), ((), ())), preferred_element_type=F32, precision=prec)


def _sigmoid(x):
    return 1.0 / (1.0 + jnp.exp(-x))


def _silu(x):
    return x * _sigmoid(x)


def _softplus(x):
    return jnp.maximum(x, 0.0) + jnp.log1p(jnp.exp(-jnp.abs(x)))


def _iota(shape, dim):
    return lax.broadcasted_iota(I32, shape, dim)


def _head_mean_sq(x, width):
    bd = (_iota((width, width), 0) // HD == _iota((width, width), 1) // HD).astype(F32) * (1.0 / HD)
    return _dot(x * x, bd, HI)


def _mod_body(c_ref, w_ref, b_ref, o_ref):
    o_ref[0] = _dot(_silu(c_ref[...]), w_ref[0], HI) + b_ref[0]


def _mod_call(c_all, w_mod, b_mod):
    bp = c_all.shape[0]
    return pl.pallas_call(
        _mod_body, name="adaln_mod",
        grid=(DEPTH, 6),
        in_specs=[pl.BlockSpec((bp, D), lambda l, j: (0, 0)),
                  pl.BlockSpec((1, D, D), lambda l, j: (l, 0, j)),
                  pl.BlockSpec((1, 1, D), lambda l, j: (l, 0, j))],
        out_specs=pl.BlockSpec((1, bp, D), lambda l, j: (l, 0, j)),
        out_shape=jax.ShapeDtypeStruct((DEPTH, bp, 6 * D), F32),
        compiler_params=_cp(("arbitrary", "arbitrary")),
    )(c_all, w_mod, b_mod.reshape(DEPTH, 1, 6 * D))


def _inproj_body(x_ref, g_ref, sc_ref, sh_ref, w_ref, wdt_ref, hg_ref, at_ref, ssd_ref, dtt_ref):
    x = x_ref[0]
    ms = jnp.mean(x * x, axis=-1, keepdims=True)
    y = x * lax.rsqrt(ms + EPS) * g_ref[...]
    h = (y * (1.0 + sc_ref[0]) + sh_ref[0]).astype(BF16)
    p = _dot(h, w_ref[...])
    hg_ref[0] = p[:, :HG_COLS]
    at_ref[0] = p[:, HG_COLS:HG_COLS + AT_COLS]
    ssd_ref[0] = p[:, HG_COLS + AT_COLS:]
    dtt_ref[0] = _dot_nt(wdt_ref[...], h)


def _inproj_call(x, norm_g, scale, shift, w_main, w_dtt):
    b, l, _ = x.shape
    tm = min(l, 512)
    row = lambda: pl.BlockSpec((1, tm, D), lambda bi, i: (bi, i, 0))
    per_b = lambda: pl.BlockSpec((1, 1, D), lambda bi, i: (bi, 0, 0))
    in_specs = [row(), pl.BlockSpec((1, D), lambda bi, i: (0, 0)), per_b(), per_b(),
                pl.BlockSpec((D, IN_PAD), lambda bi, i: (0, 0)),
                pl.BlockSpec((8, D), lambda bi, i: (0, 0))]
    args = [x, norm_g.reshape(1, D), scale, shift, w_main, w_dtt]
    out_specs = [pl.BlockSpec((1, tm, HG_COLS), lambda bi, i: (bi, i, 0)),
                 pl.BlockSpec((1, tm, AT_COLS), lambda bi, i: (bi, i, 0)),
                 pl.BlockSpec((1, tm, SSD_COLS), lambda bi, i: (bi, i, 0)),
                 pl.BlockSpec((1, 8, tm), lambda bi, i: (bi, 0, i))]
    out_shape = [jax.ShapeDtypeStruct((b, l, HG_COLS), F32),
                 jax.ShapeDtypeStruct((b, l, AT_COLS), F32),
                 jax.ShapeDtypeStruct((b, l, SSD_COLS), F32),
                 jax.ShapeDtypeStruct((b, 8, l), F32)]
    return pl.pallas_call(
        _inproj_body, name="norm_inproj",
        grid=(b, l // tm), in_specs=in_specs, out_specs=out_specs, out_shape=out_shape,
        compiler_params=_cp(("arbitrary", "arbitrary")),
    )(*args)


def _hgrn_body(bwd, nch, *refs):
    if bwd:
        x_ref, lb_ref, ng_ref, of_ref, o_ref, st_ref = refs
    else:
        x_ref, lb_ref, o_ref, st_ref = refs

    @pl.when(pl.program_id(1) == 0)
    def _():
        st_ref[...] = jnp.zeros_like(st_ref)

    q_chunk = HG_CHUNK
    lane_head = _iota((1, HG_W), 1) // HD
    tril = (_iota((q_chunk, q_chunk), 1) <= _iota((q_chunk, q_chunk), 0)).astype(F32)
    bd_mask = _iota((HG_W, HG_W), 0) // HD == _iota((HG_W, HG_W), 1) // HD
    row_q = _iota((q_chunk, 1), 0)
    att_t = _iota((q_chunk, q_chunk), 0) % HG_SUB
    att_s = _iota((q_chunk, q_chunk), 1)
    log_lb = lb_ref[0:1, :]
    log_1m_lb = lb_ref[1:2, :]
    one_m_lb = lb_ref[2:3, :]
    fcol = 3 * HG_W if bwd else 2 * HG_W

    def chunk(ci, carry):
        c = (nch - 1 - ci) if bwd else ci
        r0 = pl.multiple_of(c * q_chunk, q_chunk)
        rows = pl.ds(r0, q_chunk)
        q = _silu(x_ref[0, rows, 0:HG_W])
        v = x_ref[0, rows, HG_W:2 * HG_W]
        fr = x_ref[0, rows, fcol:fcol + HG_W]
        a2 = log_1m_lb - _softplus(-fr)
        mx = jnp.maximum(log_lb, a2)
        logf = mx + jnp.log1p(jnp.exp(-jnp.abs(log_lb - a2)))
        k = one_m_lb * _sigmoid(-fr)
        b_inc = _dot(tril, logf, HI)
        b_exc = b_inc - logf
        total = b_inc[q_chunk - 1:q_chunk, :]
        st = st_ref[...]
        if bwd:
            q_in = q * jnp.exp(total - b_exc)
            k_st = k * jnp.exp(b_exc)
        else:
            q_in = q * jnp.exp(b_inc)
            k_st = k * jnp.exp(total - b_inc)
        inter = _dot_nt(q_in.astype(BF16), st.astype(BF16))
        upd = _dot_tn(v.astype(BF16), k_st.astype(BF16))
        st_ref[...] = st * jnp.exp(total) + jnp.where(bd_mask, upd, 0.0)
        v16 = v.astype(BF16)
        pieces = []
        for i in range(q_chunk // HG_SUB):
            lo, hi = i * HG_SUB, (i + 1) * HG_SUB
            if bwd:
                ref = b_inc[hi - 1:hi, :]
                qt = q[lo:hi] * jnp.exp(ref - b_exc[lo:hi])
                kt = k * jnp.exp(jnp.where(row_q >= lo, b_exc - ref, 0.0))
                kt = jnp.where(row_q >= lo, kt, 0.0)
                amask = att_s >= att_t + lo
            else:
                ref = b_exc[lo:lo + 1, :]
                qt = q[lo:hi] * jnp.exp(b_inc[lo:hi] - ref)
                kt = k * jnp.exp(jnp.where(row_q < hi, ref - b_inc, 0.0))
                kt = jnp.where(row_q < hi, kt, 0.0)
                amask = att_s <= att_t + lo
            q4 = jnp.concatenate([jnp.where(lane_head == h, qt, 0.0) for h in range(4)], axis=0)
            att = _dot_nt(q4.astype(BF16), kt.astype(BF16))
            att = jnp.where(amask, att, 0.0)
            r = _dot(att.astype(BF16), v16)
            o_i = jnp.where(lane_head == 0, r[0:HG_SUB], 0.0)
            for h in range(1, 4):
                o_i = o_i + jnp.where(lane_head == h, r[h * HG_SUB:(h + 1) * HG_SUB], 0.0)
            pieces.append(o_i + inter[lo:hi])
        o = jnp.concatenate(pieces, axis=0)
        if bwd:
            tot = of_ref[0, rows, :] + o
            ms = _head_mean_sq(tot, HG_W)
            g = x_ref[0, rows, 4 * HG_W:5 * HG_W]
            y = tot * lax.rsqrt(ms + EPS) * ng_ref[...] * _silu(g)
            o_ref[0, rows, :] = y.astype(o_ref.dtype)
        else:
            o_ref[0, rows, :] = o
        return carry

    lax.fori_loop(0, nch, chunk, 0)


def _hgrn_call(hg_in, lb_rows, norm_g, o_fwd):
    b, l, _ = hg_in.shape
    bwd = o_fwd is not None
    t = min(l, 512)
    nb = l // t
    idx = (lambda bi, j: (bi, nb - 1 - j, 0)) if bwd else (lambda bi, j: (bi, j, 0))
    in_specs = [pl.BlockSpec((1, t, HG_COLS), idx), pl.BlockSpec((3, HG_W), lambda bi, j: (0, 0))]
    args = [hg_in, lb_rows]
    if bwd:
        in_specs += [pl.BlockSpec((1, HG_W), lambda bi, j: (0, 0)), pl.BlockSpec((1, t, HG_W), idx)]
        args += [norm_g, o_fwd]
    return pl.pallas_call(
        functools.partial(_hgrn_body, bwd, t // HG_CHUNK), name="hgrn_bwd" if bwd else "hgrn_fwd",
        grid=(b, nb), in_specs=in_specs,
        out_specs=pl.BlockSpec((1, t, HG_W), idx),
        out_shape=jax.ShapeDtypeStruct((b, l, HG_W), BF16 if bwd else F32),
        scratch_shapes=[pltpu.VMEM((HG_W, HG_W), F32)],
        compiler_params=_cp(("arbitrary", "arbitrary")),
    )(*args)


def _rope_swap(x):
    w = x.shape[-1]
    first = (_iota((1, w), 1) % 32) < 16
    return jnp.where(first, pltpu.roll(x, w - 16, 1), pltpu.roll(x, 16, 1))


def _attn_prep_body(x_ref, cos_ref, sin_ref, gq_ref, gk_ref, ek_ref, evt_ref, eye_ref, qt_ref, k_ref, vt_ref):
    x = x_ref[0]
    cos2 = cos_ref[...]
    sin2 = sin_ref[...]
    xq = x[:, :AT_W]
    qn = xq * lax.rsqrt(_head_mean_sq(xq, AT_W) + EPS) * gq_ref[...]
    cos_q = jnp.concatenate([cos2] * 4, axis=1)
    sin_q = jnp.concatenate([sin2] * 4, axis=1)
    q = ((qn * cos_q + _rope_swap(qn) * sin_q) * QK_SCALE_LOG2).astype(BF16)
    qt_ref[0] = _dot_nt(eye_ref[...], q).astype(BF16)
    xk = x[:, AT_W:AT_W + KV_W]
    kn = xk * lax.rsqrt(_head_mean_sq(xk, KV_W) + EPS) * gk_ref[...]
    kr = (kn * cos2 + _rope_swap(kn) * sin2).astype(BF16)
    k_ref[0] = _dot(kr, ek_ref[...]).astype(BF16)
    xv = x[:, AT_W + KV_W:].astype(BF16)
    ones_row = ((_iota((2 * LANES, 1), 0) % LANES) >= HD).astype(F32)
    vt_ref[0] = (_dot_nt(evt_ref[...], xv) + ones_row).astype(BF16)


def _attn_prep_call(at_in, cos2, sin2, gq, gk, ek, evt, eye):
    b, l, _ = at_in.shape
    tm = min(l, 512)
    const = lambda shape: pl.BlockSpec(shape, lambda bi, i: (0, 0))
    return pl.pallas_call(
        _attn_prep_body, name="attn_prep",
        grid=(b, l // tm),
        in_specs=[pl.BlockSpec((1, tm, AT_COLS), lambda bi, i: (bi, i, 0)),
                  pl.BlockSpec((tm, KV_W), lambda bi, i: (i, 0)),
                  pl.BlockSpec((tm, KV_W), lambda bi, i: (i, 0)),
                  const((1, AT_W)), const((1, KV_W)), const((KV_W, AT_W)), const((2 * LANES, KV_W)),
                  const((AT_W, AT_W))],
        out_specs=[pl.BlockSpec((1, AT_W, tm), lambda bi, i: (bi, 0, i)),
                   pl.BlockSpec((1, tm, AT_W), lambda bi, i: (bi, i, 0)),
                   pl.BlockSpec((1, 2 * LANES, tm), lambda bi, i: (bi, 0, i))],
        out_shape=[jax.ShapeDtypeStruct((b, AT_W, l), BF16),
                   jax.ShapeDtypeStruct((b, l, AT_W), BF16),
                   jax.ShapeDtypeStruct((b, 2 * LANES, l), BF16)],
        compiler_params=_cp(("arbitrary", "arbitrary")),
    )(at_in, cos2, sin2, gq, gk, ek, evt, eye)


def _attn_body(tq, qt_ref, k_ref, vt_ref, place_ref, o_ref, qm_ref, m_ref, acc_ref):
    kk = pl.program_id(3)

    @pl.when(kk == 0)
    def _():
        qm_ref[...] = jnp.zeros_like(qm_ref)
        for g in range(AT_GROUP):
            qm_ref[g * HD:(g + 1) * HD, g * tq:(g + 1) * tq] = qt_ref[0, g * HD:(g + 1) * HD, :]
        m_ref[...] = jnp.full_like(m_ref, -jnp.inf)
        acc_ref[...] = jnp.zeros_like(acc_ref)

    s = _dot(k_ref[0], qm_ref[...])
    m_old = m_ref[...]
    m_new = jnp.maximum(m_old, jnp.max(s, axis=0, keepdims=True))
    alpha = jnp.exp2(m_old - m_new)
    p = jnp.exp2(s - m_new).astype(BF16)
    acc_ref[...] = acc_ref[...] * alpha + _dot(vt_ref[0], p)
    m_ref[...] = m_new

    @pl.when(kk == pl.num_programs(3) - 1)
    def _():
        acc = acc_ref[...]
        o_t = (acc[0:HD] / acc[HD:HD + 1]).astype(BF16)
        out = _dot_tn(o_t[:, 0:tq], place_ref[0])
        for g in range(1, AT_GROUP):
            out = out + _dot_tn(o_t[:, g * tq:(g + 1) * tq], place_ref[g])
        o_ref[0] = out.astype(o_ref.dtype)


def _attn_call(qt, kt, vt, place):
    b, _, l = qt.shape
    tq = min(l, 512)
    tk = min(l, 1024)
    return pl.pallas_call(
        functools.partial(_attn_body, tq), name="flash_attn",
        grid=(b, 2, l // tq, l // tk),
        in_specs=[pl.BlockSpec((1, 4 * HD, tq), lambda bi, j, i, kk: (bi, j, i)),
                  pl.BlockSpec((1, tk, 4 * HD), lambda bi, j, i, kk: (bi, kk, j)),
                  pl.BlockSpec((1, LANES, tk), lambda bi, j, i, kk: (bi, j, kk)),
                  pl.BlockSpec((AT_GROUP, HD, 4 * HD), lambda bi, j, i, kk: (0, 0, 0))],
        out_specs=pl.BlockSpec((1, tq, 4 * HD), lambda bi, j, i, kk: (bi, i, j)),
        out_shape=jax.ShapeDtypeStruct((b, l, AT_W), BF16),
        scratch_shapes=[pltpu.VMEM((4 * HD, AT_GROUP * tq), BF16),
                        pltpu.VMEM((1, AT_GROUP * tq), F32),
                        pltpu.VMEM((LANES, AT_GROUP * tq), F32)],
        compiler_params=_cp(("arbitrary",) * 4),
    )(qt, kt, vt, place)


def _ssd_conv_body(t, x_ref, prev_ref, next_ref, w_ref, b_ref, o_ref, ext_ref):
    j = pl.program_id(1)
    nb = pl.num_programs(1)
    lo, hi = SSD_W, SSD_W + SSD_XBC
    ext_ref[0:8, :] = jnp.where(j > 0, prev_ref[0, :, lo:hi], 0.0)
    ext_ref[8:8 + t, :] = x_ref[0, :, lo:hi]
    ext_ref[8 + t:16 + t, :] = jnp.where(j < nb - 1, next_ref[0, :, lo:hi], 0.0)
    pad = SSD_CONV // 2
    acc = b_ref[...] + w_ref[0:1, :] * ext_ref[pl.ds(8 - pad, t), :]
    for kk in range(1, SSD_CONV):
        acc = acc + w_ref[kk:kk + 1, :] * ext_ref[pl.ds(8 - pad + kk, t), :]
    o_ref[0] = _silu(acc)


def _ssd_conv_call(ssd_in, conv_w, conv_b):
    b, l, _ = ssd_in.shape
    t = min(l, 512)
    t8 = t // 8
    last8 = l // 8 - 1
    return pl.pallas_call(
        functools.partial(_ssd_conv_body, t), name="ssd_conv",
        grid=(b, l // t),
        in_specs=[pl.BlockSpec((1, t, SSD_COLS), lambda bi, j: (bi, j, 0)),
                  pl.BlockSpec((1, 8, SSD_COLS), lambda bi, j: (bi, jnp.maximum(j * t8 - 1, 0), 0)),
                  pl.BlockSpec((1, 8, SSD_COLS), lambda bi, j: (bi, jnp.minimum((j + 1) * t8, last8), 0)),
                  pl.BlockSpec((8, SSD_XBC), lambda bi, j: (0, 0)),
                  pl.BlockSpec((1, SSD_XBC), lambda bi, j: (0, 0))],
        out_specs=pl.BlockSpec((1, t, SSD_XBC), lambda bi, j: (bi, j, 0)),
        out_shape=jax.ShapeDtypeStruct((b, l, SSD_XBC), F32),
        scratch_shapes=[pltpu.VMEM((t + 16, SSD_XBC), F32)],
        compiler_params=_cp(("arbitrary", "arbitrary")),
    )(ssd_in, ssd_in, ssd_in, conv_w, conv_b)


def _ssd_body(bwd, nch, *refs):
    if bwd:
        xbc_ref, dtc_ref, dtr_ref, par_ref, parc_ref, z_ref, yf_ref, ng_ref, o_ref, st_ref = refs
    else:
        xbc_ref, dtc_ref, dtr_ref, par_ref, parc_ref, o_ref, st_ref = refs

    @pl.when(pl.program_id(1) == 0)
    def _():
        st_ref[...] = jnp.zeros_like(st_ref)

    qc = SSD_CHUNK
    dsel = 4 if bwd else 0
    lane_head = _iota((1, SSD_W), 1) // HD
    lane_grp = _iota((1, LANES), 1) // HD
    tril = (_iota((qc, qc), 1) <= _iota((qc, qc), 0)).astype(F32)
    triu = (_iota((qc, qc), 0) <= _iota((qc, qc), 1)).astype(F32)
    tt = _iota((qc, qc), 0)
    ss = _iota((qc, qc), 1)
    st_mask = (_iota((SSD_W, LANES), 0) // (2 * HD)) == (_iota((SSD_W, LANES), 1) // HD)
    bias_row = par_ref[0:1, 0:LANES]
    acoef_row = par_ref[1:2, 0:LANES]
    dskip_row = par_ref[2:3, :]
    bias_col = parc_ref[:, 0:1]
    acoef_col = parc_ref[:, 1:2]

    def expand(col_vals):
        out = jnp.broadcast_to(col_vals[:, dsel:dsel + 1], (qc, SSD_W))
        for h in range(1, 4):
            out = jnp.where(lane_head == h, jnp.broadcast_to(col_vals[:, dsel + h:dsel + h + 1], (qc, SSD_W)), out)
        return out

    def chunk(ci, carry):
        c = (nch - 1 - ci) if bwd else ci
        r0 = pl.multiple_of(c * qc, qc)
        rows = pl.ds(r0, qc)
        xs = xbc_ref[0, rows, 0:SSD_W]
        bm = xbc_ref[0, rows, SSD_W:SSD_W + LANES]
        cm = xbc_ref[0, rows, SSD_W + LANES:SSD_W + 2 * LANES]
        dt_c = _softplus(dtc_ref[0, rows, :] + bias_row)
        a_c = dt_c * acoef_row
        dt_r = _softplus(dtr_ref[0, :, rows] + bias_col)
        a_r = dt_r * acoef_col
        inc_c = _dot(tril, a_c, HI)
        inc_r = _dot(a_r, triu, HI)
        if bwd:
            cum_c, cum_r = inc_c - a_c, inc_r - a_r
        else:
            cum_c, cum_r = inc_c, inc_r
        total_c = inc_c[qc - 1:qc, :]
        xdt = xs * expand(dt_c)
        xdt16 = xdt.astype(BF16)
        bm16 = bm.astype(BF16)
        cm16 = cm.astype(BF16)
        gmat = [_dot_nt(jnp.where(lane_grp == g, cm, 0.0).astype(BF16), bm16) for g in range(2)]
        y = jnp.zeros((qc, SSD_W), F32)
        for h in range(4):
            col = cum_c[:, dsel + h:dsel + h + 1]
            rw = cum_r[dsel + h:dsel + h + 1, :]
            if bwd:
                dec = jnp.where(ss >= tt, jnp.exp(jnp.where(ss >= tt, rw - col, 0.0)), 0.0)
            else:
                dec = jnp.where(ss <= tt, jnp.exp(jnp.where(ss <= tt, col - rw, 0.0)), 0.0)
            yh = _dot((gmat[h // 2] * dec).astype(BF16), xdt16)
            y = jnp.where(lane_head == h, yh, y)
        st = st_ref[...]
        if bwd:
            out_dec = jnp.exp(total_c - cum_c)
            st_dec = jnp.exp(cum_c)
        else:
            out_dec = jnp.exp(cum_c)
            st_dec = jnp.exp(total_c - cum_c)
        y = y + _dot_nt(cm16, st.astype(BF16)) * expand(out_dec)
        upd = _dot_tn((xdt * expand(st_dec)).astype(BF16), bm16)
        tot_rows = jnp.broadcast_to(jnp.exp(total_c[:, dsel:dsel + 1]), (HD, LANES))
        decay_rows = jnp.concatenate(
            [tot_rows] + [jnp.broadcast_to(jnp.exp(total_c[:, dsel + h:dsel + h + 1]), (HD, LANES))
                          for h in range(1, 4)], axis=0)
        st_ref[...] = st * decay_rows + jnp.where(st_mask, upd, 0.0)
        if bwd:
            yy = yf_ref[0, rows, :] + y + xs * dskip_row
            yy = yy * _silu(z_ref[0, rows, :])
            ms = _head_mean_sq(yy, SSD_W)
            o_ref[0, rows, :] = (yy * lax.rsqrt(ms + EPS) * ng_ref[...]).astype(o_ref.dtype)
        else:
            o_ref[0, rows, :] = y
        return carry

    lax.fori_loop(0, nch, chunk, 0)


def _ssd_call(xbc, ssd_in, dtt, par_rows, par_cols, norm_g, y_fwd):
    b, l, _ = xbc.shape
    bwd = y_fwd is not None
    t = min(l, 512)
    nb = l // t
    blk = (lambda bi, j: (bi, nb - 1 - j)) if bwd else (lambda bi, j: (bi, j))
    rows3 = lambda lane_blk: (lambda bi, j: blk(bi, j) + (lane_blk,))
    in_specs = [pl.BlockSpec((1, t, SSD_XBC), rows3(0)),
                pl.BlockSpec((1, t, LANES), rows3((SSD_W + SSD_XBC) // LANES)),
                pl.BlockSpec((1, 8, t), lambda bi, j: (bi, 0, blk(bi, j)[1])),
                pl.BlockSpec((8, SSD_W), lambda bi, j: (0, 0)),
                pl.BlockSpec((8, LANES), lambda bi, j: (0, 0))]
    args = [xbc, ssd_in, dtt, par_rows, par_cols]
    if bwd:
        in_specs += [pl.BlockSpec((1, t, SSD_W), rows3(0)),
                     pl.BlockSpec((1, t, SSD_W), rows3(0)),
                     pl.BlockSpec((1, SSD_W), lambda bi, j: (0, 0))]
        args += [ssd_in, y_fwd, norm_g]
    return pl.pallas_call(
        functools.partial(_ssd_body, bwd, t // SSD_CHUNK), name="ssd_bwd" if bwd else "ssd_fwd",
        grid=(b, nb), in_specs=in_specs,
        out_specs=pl.BlockSpec((1, t, SSD_W), rows3(0)),
        out_shape=jax.ShapeDtypeStruct((b, l, SSD_W), BF16 if bwd else F32),
        scratch_shapes=[pltpu.VMEM((SSD_W, LANES), F32)],
        compiler_params=_cp(("arbitrary", "arbitrary")),
    )(*args)


def _outproj_body(x_ref, hg_ref, at_ref, ssd_ref, w1_ref, w2_ref, w3_ref, g1_ref, ng_ref, sc_ref, sh_ref,
                  wr_ref, x1_ref, h2_ref, aff_ref, afft_ref):
    mix = _dot(hg_ref[0], w1_ref[...]) + _dot(at_ref[0], w2_ref[...]) + _dot(ssd_ref[0], w3_ref[...])
    x1 = x_ref[0] + g1_ref[0] * mix
    x1_ref[0] = x1
    ms = jnp.mean(x1 * x1, axis=-1, keepdims=True)
    h2 = x1 * lax.rsqrt(ms + EPS) * ng_ref[...] * (1.0 + sc_ref[0]) + sh_ref[0]
    h2_ref[0] = h2.astype(h2_ref.dtype)
    logits = _dot(h2, wr_ref[...], HI)
    valid = _iota((1, LANES), 1) < N_EXPERTS
    logits = jnp.where(valid, logits, -jnp.inf)
    e = jnp.exp(logits - jnp.max(logits, axis=-1, keepdims=True))
    aff = e / jnp.sum(e, axis=-1, keepdims=True)
    aff_ref[0] = aff
    afft_ref[...] = jnp.transpose(aff)[0:N_EXPERTS, :]


def _outproj_call(x, o_hg, o_at, o_ssd, w1, w2, w3, gate1, norm_g, scale, shift, w_r):
    b, l, _ = x.shape
    tm = min(l, 512)
    nbl = l // tm
    row = lambda w: pl.BlockSpec((1, tm, w), lambda bi, i: (bi, i, 0))
    per_b = lambda: pl.BlockSpec((1, 1, D), lambda bi, i: (bi, 0, 0))
    const = lambda shape: pl.BlockSpec(shape, lambda bi, i: (0, 0))
    return pl.pallas_call(
        _outproj_body, name="outproj_router",
        grid=(b, nbl),
        in_specs=[row(D), row(HG_W), row(AT_W), row(SSD_W), const((HG_W, D)), const((AT_W, D)), const((SSD_W, D)),
                  per_b(), const((1, D)), per_b(), per_b(), const((D, LANES))],
        out_specs=[row(D), row(D), row(LANES),
                   pl.BlockSpec((N_EXPERTS, tm), lambda bi, i: (0, bi * nbl + i))],
        out_shape=[jax.ShapeDtypeStruct((b, l, D), F32), jax.ShapeDtypeStruct((b, l, D), BF16),
                   jax.ShapeDtypeStruct((b, l, LANES), F32), jax.ShapeDtypeStruct((N_EXPERTS, b * l), F32)],
        compiler_params=_cp(("arbitrary", "arbitrary")),
    )(x, o_hg, o_at, o_ssd, w1, w2, w3, gate1, norm_g.reshape(1, D), scale, shift, w_r)


def _route_thr_body(cap, n_tok, afft_ref, o_ref, t_ref):
    lane_chunk = min(n_tok, 2048)
    n_chunks = n_tok // lane_chunk

    def count_ge(cand):
        def body(i, acc):
            bits = lax.bitcast_convert_type(afft_ref[:, pl.ds(pl.multiple_of(i * lane_chunk, lane_chunk), lane_chunk)], I32)
            return acc + (bits >= cand).astype(F32)
        acc = lax.fori_loop(0, n_chunks, body, jnp.zeros((N_EXPERTS, lane_chunk), F32))
        return jnp.sum(acc, axis=1, keepdims=True)

    def bit_step(i, thr):
        cand = thr | jnp.left_shift(jnp.int32(1), 30 - i)
        return jnp.where(count_ge(cand) >= cap, cand, thr)

    thr = lax.fori_loop(0, 31, bit_step, jnp.zeros((N_EXPERTS, 1), I32))
    n_gt = count_ge(thr + 1)
    need = cap - n_gt

    nblk = n_tok // ROUTE_BLK
    lane = _iota((1, LANES), 1)

    def blk_counts(j, carry):
        gt_tab, eq_tab = carry
        bits = lax.bitcast_convert_type(afft_ref[:, pl.ds(pl.multiple_of(j * ROUTE_BLK, ROUTE_BLK), ROUTE_BLK)], I32)
        cg = jnp.sum((bits > thr).astype(F32), axis=1, keepdims=True)
        ce = jnp.sum((bits == thr).astype(F32), axis=1, keepdims=True)
        return (gt_tab + jnp.where(lane == j, cg, 0.0), eq_tab + jnp.where(lane == j, ce, 0.0))

    zeros = jnp.zeros((N_EXPERTS, LANES), F32)
    gt_tab, eq_tab = lax.fori_loop(0, nblk, blk_counts, (zeros, zeros))
    strict = (_iota((LANES, LANES), 0) < _iota((LANES, LANES), 1)).astype(F32)
    eq_before = _dot(eq_tab, strict, HI)
    eq_take = jnp.clip(need - eq_before, 0.0, eq_tab)
    cnt = gt_tab + eq_take
    cnt8 = jnp.floor((cnt + 7.0) * 0.125) * 8.0
    off = _dot(cnt8, strict, HI)
    o_ref[0] = cnt.astype(I32)
    o_ref[1] = off.astype(I32)
    ident = (_iota((N_EXPERTS, LANES), 0) == _iota((N_EXPERTS, LANES), 1)).astype(F32)
    flip = lambda tab: _dot_tn(tab, ident, HI).astype(I32)
    thr_b = jnp.broadcast_to(thr, (N_EXPERTS, LANES))
    thr_hi = flip(jnp.right_shift(thr_b, 15).astype(F32))
    thr_lo = flip(jnp.bitwise_and(thr_b, 0x7FFF).astype(F32))
    t_ref[0] = jnp.left_shift(thr_hi, 15) | thr_lo
    t_ref[1] = flip(jnp.broadcast_to(need, (N_EXPERTS, LANES)))
    t_ref[2] = flip(eq_before)


def _route_thr_call(afft, cap):
    n_tok = afft.shape[1]
    return pl.pallas_call(
        functools.partial(_route_thr_body, cap, n_tok), name="route_threshold",
        grid=(1,),
        in_specs=[pl.BlockSpec((N_EXPERTS, n_tok), lambda i: (0, 0))],
        out_specs=[pl.BlockSpec((2, N_EXPERTS, LANES), lambda i: (0, 0, 0)),
                   pl.BlockSpec((3, LANES, LANES), lambda i: (0, 0, 0))],
        out_shape=[jax.ShapeDtypeStruct((2, N_EXPERTS, LANES), I32),
                   jax.ShapeDtypeStruct((3, LANES, LANES), I32)],
        compiler_params=_cp(("arbitrary",)),
    )(afft)


def _route_select(aff_ref, tab_ref, tri_ref):
    j = pl.program_id(0)
    bits = lax.bitcast_convert_type(aff_ref[...], I32)
    thr = tab_ref[0, 0:1, :]
    need = tab_ref[1, 0:1, :].astype(F32)
    eq_before = tab_ref[2, pl.ds(j, 1), :].astype(F32)
    gt = bits > thr
    eq = bits == thr
    tri = tri_ref[...]
    eq_rank = _dot(tri, eq.astype(BF16)) + eq_before
    sel = gt | (eq & (eq_rank < need))
    rank = _dot(tri, sel.astype(BF16))
    return sel, rank


def _onehot_t(sel, rank, e, wi):
    slot = (_iota((1, ROUTE_WIN), 1) + wi * ROUTE_WIN).astype(F32)
    return (sel[:, e:e + 1] & (rank[:, e:e + 1] == slot)).astype(BF16)


def _gather_body(nblk, cnt_s, off_s, aff_ref, tab_ref, tri_ref, h2_ref, xg_in, xg_hbm, stage, sem):
    del xg_in
    j = pl.program_id(0)
    sel, rank = _route_select(aff_ref, tab_ref, tri_ref)
    h2 = h2_ref[...]
    w = ROUTE_WIN

    def group_copy(e, off, wi, r):
        return pltpu.make_async_copy(stage.at[e, pl.ds(r * 8, 8)],
                                     xg_hbm.at[e, pl.ds(pl.multiple_of(off + wi * w + r * 8, 8), 8)], sem.at[e])

    def groups(cnt, wi):
        return (jnp.minimum(cnt - wi * w, w) + 7) // 8

    for e in range(N_EXPERTS):
        cnt = cnt_s[e * nblk + j]
        off = off_s[e * nblk + j]
        n_win = (cnt + (w - 1)) // w

        def window(wi, carry, e=e, cnt=cnt, off=off, n_win=n_win):
            stage[e] = _dot_tn(_onehot_t(sel, rank, e, wi), h2)
            n8 = groups(cnt, wi)
            lax.fori_loop(0, n8, lambda r, c: (group_copy(e, off, wi, r).start(), c)[1], 0)

            @pl.when(wi + 1 < n_win)
            def _():
                lax.fori_loop(0, n8, lambda r, c: (group_copy(e, off, wi, r).wait(), c)[1], 0)
            return carry

        lax.fori_loop(0, n_win, window, 0)

    for e in range(N_EXPERTS):
        cnt = cnt_s[e * nblk + j]
        off = off_s[e * nblk + j]
        n_win = (cnt + (w - 1)) // w

        @pl.when(n_win > 0)
        def _(e=e, cnt=cnt, off=off, n_win=n_win):
            wi = n_win - 1
            lax.fori_loop(0, groups(cnt, wi), lambda r, c: (group_copy(e, off, wi, r).wait(), c)[1], 0)


def _gather_call(cnt, off, aff2d, tab_t, tri, h2_flat, slots_alloc):
    n_tok = h2_flat.shape[0]
    nblk = n_tok // ROUTE_BLK
    xg0 = jnp.zeros((N_EXPERTS, slots_alloc, D), F32)
    grid_spec = pltpu.PrefetchScalarGridSpec(
        num_scalar_prefetch=2,
        grid=(nblk,),
        in_specs=[pl.BlockSpec((ROUTE_BLK, LANES), lambda j, *_: (j, 0)),
                  pl.BlockSpec((3, LANES, LANES), lambda j, *_: (0, 0, 0)),
                  pl.BlockSpec((ROUTE_BLK, ROUTE_BLK), lambda j, *_: (0, 0)),
                  pl.BlockSpec((ROUTE_BLK, D), lambda j, *_: (j, 0)),
                  pl.BlockSpec(memory_space=pl.ANY)],
        out_specs=pl.BlockSpec(memory_space=pl.ANY),
        scratch_shapes=[pltpu.VMEM((N_EXPERTS, ROUTE_WIN, D), F32), pltpu.SemaphoreType.DMA((N_EXPERTS,))],
    )
    return pl.pallas_call(
        functools.partial(_gather_body, nblk), name="expert_gather",
        grid_spec=grid_spec,
        out_shape=jax.ShapeDtypeStruct((N_EXPERTS, slots_alloc, D), F32),
        input_output_aliases={6: 0},
        compiler_params=_cp(("arbitrary",)),
    )(cnt, off, aff2d, tab_t, tri, h2_flat, xg0)


def _ffn_body(x_ref, wg_ref, wu_ref, wd_ref, y_ref):
    xg = x_ref[0].astype(BF16)
    hid = _silu(_dot(xg, wg_ref[0])) * _dot(xg, wu_ref[0])
    y_ref[0] = _dot(hid.astype(BF16), wd_ref[0])


def _ffn_call(xg, wg, wu, wd):
    _, slots_alloc, _ = xg.shape
    wspec = lambda: pl.BlockSpec((1, D, D), lambda e, i: (e, 0, 0))
    tile = lambda: pl.BlockSpec((1, SLOT_TILE, D), lambda e, i: (e, i, 0))
    return pl.pallas_call(
        _ffn_body, name="expert_ffn",
        grid=(N_EXPERTS, slots_alloc // SLOT_TILE),
        in_specs=[tile(), wspec(), wspec(), wspec()],
        out_specs=tile(),
        out_shape=jax.ShapeDtypeStruct(xg.shape, F32),
        compiler_params=_cp(("arbitrary", "arbitrary")),
    )(xg, wg, wu, wd)


def _combine_body(final, nblk, cnt_s, off_s, aff_ref, tab_ref, tri_ref, x1_ref, g2_ref, fg_ref, y_hbm,
                  o_ref, ybuf, acc_ref, sems):
    j = pl.program_id(0)
    sel, rank = _route_select(aff_ref, tab_ref, tri_ref)
    aff = aff_ref[...]
    w = ROUTE_WIN

    def win_copy(e, off, wi):
        return pltpu.make_async_copy(y_hbm.at[e, pl.ds(pl.multiple_of(off + wi * w, 8), w)], ybuf.at[e], sems.at[e])

    for e in range(N_EXPERTS):
        win_copy(e, off_s[e * nblk + j], 0).start()
    acc_ref[...] = jnp.zeros_like(acc_ref)
    for e in range(N_EXPERTS):
        cnt = cnt_s[e * nblk + j]
        off = off_s[e * nblk + j]
        win_copy(e, off, 0).wait()

        def window(wi, carry, e=e, off=off):
            @pl.when(wi > 0)
            def _():
                cp = win_copy(e, off, wi)
                cp.start()
                cp.wait()
            yw = ybuf[e]
            y_hi = yw.astype(BF16)
            y_lo = (yw - y_hi.astype(F32)).astype(BF16)
            oh = _onehot_t(sel, rank, e, wi)
            acc_ref[...] += aff[:, e:e + 1] * (_dot(oh, y_hi) + _dot(oh, y_lo))
            return carry

        lax.fori_loop(0, (cnt + (w - 1)) // w, window, 0)
    x2 = x1_ref[...] + g2_ref[0] * acc_ref[...]
    if final:
        ms = jnp.mean(x2 * x2, axis=-1, keepdims=True)
        x2 = x2 * lax.rsqrt(ms + EPS) * fg_ref[...]
    o_ref[...] = x2


def _combine_call(cnt, off, aff2d, tab_t, tri, x1_flat, gate2, final_g, y, blocks_per_batch, final):
    n_tok = x1_flat.shape[0]
    nblk = n_tok // ROUTE_BLK
    grid_spec = pltpu.PrefetchScalarGridSpec(
        num_scalar_prefetch=2,
        grid=(nblk,),
        in_specs=[pl.BlockSpec((ROUTE_BLK, LANES), lambda j, *_: (j, 0)),
                  pl.BlockSpec((3, LANES, LANES), lambda j, *_: (0, 0, 0)),
                  pl.BlockSpec((ROUTE_BLK, ROUTE_BLK), lambda j, *_: (0, 0)),
                  pl.BlockSpec((ROUTE_BLK, D), lambda j, *_: (j, 0)),
                  pl.BlockSpec((1, 1, D), lambda j, *_: (j // blocks_per_batch, 0, 0)),
                  pl.BlockSpec((1, D), lambda j, *_: (0, 0)),
                  pl.BlockSpec(memory_space=pl.ANY)],
        out_specs=pl.BlockSpec((ROUTE_BLK, D), lambda j, *_: (j, 0)),
        scratch_shapes=[pltpu.VMEM((N_EXPERTS, ROUTE_WIN, D), F32), pltpu.VMEM((ROUTE_BLK, D), F32),
                        pltpu.SemaphoreType.DMA((N_EXPERTS,))],
    )
    return pl.pallas_call(
        functools.partial(_combine_body, final, nblk), name="expert_combine",
        grid_spec=grid_spec,
        out_shape=jax.ShapeDtypeStruct((n_tok, D), F32),
        compiler_params=_cp(("arbitrary",)),
    )(cnt, off, aff2d, tab_t, tri, x1_flat, gate2, final_g.reshape(1, D), y)


def _rope_tables(l):
    quarter = HD // 4
    inv = ROPE_THETA ** (-jnp.arange(quarter, dtype=F32) / quarter)
    t = jnp.arange(l)
    pos = jnp.stack([(t // GRID_W).astype(F32), (t % GRID_W).astype(F32)], axis=1)
    lane = np.arange(KV_W)
    which = (lane % HD) // (HD // 2)
    ang = pos[:, which] * inv[lane % quarter][None, :]
    sign = np.where((lane % (HD // 2)) < quarter, -1.0, 1.0).astype(np.float32)
    return jnp.cos(ang), jnp.sin(ang) * sign[None, :]


def _prep_layer(l, p):
    w_in = p['w_in'][l]
    n_main = HG_COLS + AT_COLS + SSD_W + SSD_XBC
    w_main = jnp.concatenate([w_in[:, :n_main], jnp.pad(w_in[:, n_main:], ((0, 0), (0, 120)))], axis=1).astype(BF16)
    w_dtt = w_in[:, n_main:].T.astype(BF16)
    lb_all = jnp.cumsum(jax.nn.softmax(p['hg_lb'].astype(F32), axis=0), axis=0)
    lb = (lb_all - lb_all[:1])[l]
    lb_rows = jnp.stack([jnp.log(lb), jnp.log1p(-lb), 1.0 - lb], axis=1)
    a_coef = -jnp.exp(p['a_log'][l].astype(F32)).reshape(-1)
    dt_bias = p['dt_bias'][l].astype(F32).reshape(-1)
    par_rows = jnp.zeros((8, SSD_W), F32).at[0, :8].set(dt_bias).at[1, :8].set(a_coef)
    par_rows = par_rows.at[2, :].set(jnp.repeat(p['d_skip'][l].astype(F32), HD))
    par_cols = jnp.zeros((8, LANES), F32).at[:, 0].set(dt_bias).at[:, 1].set(a_coef)
    w_out = p['w_out'][l].astype(BF16)
    lane_k = np.arange(KV_W)
    ek = np.zeros((KV_W, AT_W), np.float32)
    for rep in range(AT_GROUP):
        ek[lane_k, (lane_k // HD) * (AT_GROUP * HD) + rep * HD + lane_k % HD] = 1.0
    evt = np.zeros((2 * LANES, KV_W), np.float32)
    evt[(lane_k // HD) * LANES + lane_k % HD, lane_k] = 1.0
    place = np.zeros((AT_GROUP, HD, AT_GROUP * HD), np.float32)
    for g in range(AT_GROUP):
        place[g, np.arange(HD), g * HD + np.arange(HD)] = 1.0
    return dict(
        w_main=w_main, w_dtt=w_dtt, lb_rows=lb_rows,
        hg_norm=jnp.tile(p['hg_norm_g'][l], 4).reshape(1, HG_W),
        gq=jnp.tile(p['q_norm_g'][l], 8).reshape(1, AT_W), gk=jnp.tile(p['k_norm_g'][l], 2).reshape(1, KV_W),
        ek=jnp.asarray(ek, BF16), evt=jnp.asarray(evt, BF16), place=jnp.asarray(place, BF16),
        eye=jnp.asarray(np.eye(AT_W, dtype=np.float32), BF16),
        conv_w=jnp.pad(p['conv_w'][l], ((0, 8 - SSD_CONV), (0, 0))), conv_b=p['conv_b'][l].reshape(1, SSD_XBC),
        par_rows=par_rows, par_cols=par_cols,
        ssd_norm=jnp.tile(p['ssd_norm_g'][l], 4).reshape(1, SSD_W),
        w_o1=w_out[:HG_W], w_o2=w_out[HG_W:HG_W + AT_W], w_o3=w_out[HG_W + AT_W:],
        w_r=jnp.pad(p['w_router'][l], ((0, 0), (0, LANES - N_EXPERTS))),
        wg=p['w_gate'][l].astype(BF16), wu=p['w_up'][l].astype(BF16), wd=p['w_down'][l].astype(BF16),
    )


def _trunk(x, mod, layers, p):
    b, l, _ = x.shape
    n_tok = b * l
    cap = EC_CAPACITY * n_tok // N_EXPERTS
    nblk = n_tok // ROUTE_BLK
    slots = cap + 8 * nblk
    n_tiles = -(-slots // SLOT_TILE)
    slots_alloc = -(-(n_tiles * SLOT_TILE + ROUTE_WIN) // SLOT_TILE) * SLOT_TILE
    cos2, sin2 = _rope_tables(l)
    tri = (np.arange(ROUTE_BLK)[None, :] < np.arange(ROUTE_BLK)[:, None]).astype(np.float32)
    tri = jnp.asarray(tri, BF16)
    for li in range(DEPTH):
        w = layers[li]
        sh1, sc1, g1, sh2, sc2, g2 = [mod[li, :, i * D:(i + 1) * D].reshape(b, 1, D) for i in range(6)]
        hg_in, at_in, ssd_in, dtt = _inproj_call(x, p['norm1_g'][li], sc1, sh1, w['w_main'], w['w_dtt'])
        o_f = _hgrn_call(hg_in, w['lb_rows'][0], w['hg_norm'], None)
        o_hg = _hgrn_call(hg_in, w['lb_rows'][1], w['hg_norm'], o_f)
        qt, kt, vt = _attn_prep_call(at_in, cos2, sin2, w['gq'], w['gk'], w['ek'], w['evt'], w['eye'])
        o_at = _attn_call(qt, kt, vt, w['place'])
        xbc = _ssd_conv_call(ssd_in, w['conv_w'], w['conv_b'])
        y_f = _ssd_call(xbc, ssd_in, dtt, w['par_rows'], w['par_cols'], w['ssd_norm'], None)
        o_ssd = _ssd_call(xbc, ssd_in, dtt, w['par_rows'], w['par_cols'], w['ssd_norm'], y_f)
        x1, h2, aff, afft = _outproj_call(x, o_hg, o_at, o_ssd, w['w_o1'], w['w_o2'], w['w_o3'], g1,
                                          p['norm2_g'][li], sc2, sh2, w['w_r'])
        tab_e, tab_t = _route_thr_call(afft, cap)
        cnt = tab_e[0, :, :nblk].reshape(-1)
        off = tab_e[1, :, :nblk].reshape(-1)
        aff2d = aff.reshape(n_tok, LANES)
        xg = _gather_call(cnt, off, aff2d, tab_t, tri, h2.reshape(n_tok, D), slots_alloc)
        y = _ffn_call(xg, w['wg'], w['wu'], w['wd'])
        x = _combine_call(cnt, off, aff2d, tab_t, tri, x1.reshape(n_tok, D), g2, p['final_g'], y,
                          l // ROUTE_BLK, li == DEPTH - 1).reshape(b, l, D)
    return x


def kernel(x_prompt, x_sample, c_prompt, c_sample, norm1_g, norm2_g, w_mod, b_mod, w_in, hg_lb, hg_norm_g,
           q_norm_g, k_norm_g, conv_w, conv_b, a_log, dt_bias, d_skip, ssd_norm_g, w_out, w_router,
           w_gate, w_up, w_down, final_g):
    p = dict(norm1_g=norm1_g, norm2_g=norm2_g, w_in=w_in, hg_lb=hg_lb, hg_norm_g=hg_norm_g, q_norm_g=q_norm_g,
             k_norm_g=k_norm_g, conv_w=conv_w, conv_b=conv_b, a_log=a_log, dt_bias=dt_bias, d_skip=d_skip,
             ssd_norm_g=ssd_norm_g, w_out=w_out, w_router=w_router, w_gate=w_gate, w_up=w_up, w_down=w_down,
             final_g=final_g)
    bp, bs = c_prompt.shape[0], c_sample.shape[0]
    rows = -(-(bp + bs) // 8) * 8
    c_all = jnp.pad(jnp.concatenate([c_prompt, c_sample], axis=0), ((0, rows - bp - bs), (0, 0)))
    mod = _mod_call(c_all, w_mod, b_mod)
    layers = [_prep_layer(li, p) for li in range(DEPTH)]
    y_prompt = _trunk(x_prompt, mod[:, :bp], layers, p)
    y_sample = _trunk(x_sample, mod[:, bp:bp + bs], layers, p)
    return (y_prompt, y_sample)
```

```python
import functools

import numpy as np
import jax
import jax.numpy as jnp
from jax import lax
from jax.experimental import pallas as pl
from jax.experimental.pallas import tpu as pltpu

F32 = jnp.float32
BF16 = jnp.bfloat16
I32 = jnp.int32
HI = lax.Precision.HIGHEST

D = 1024
DEPTH = 2
GRID_W = 64
EPS = 1e-6
HD = 64
HG_W = 256
HG_CHUNK = 64
HG_SUB = 16
AT_W = 512
KV_W = 128
AT_GROUP = 4
ROPE_THETA = 10000.0
QK_SCALE_LOG2 = (HD ** -0.5) * 1.4426950408889634
SSD_W = 256
SSD_XBC = 512
SSD_CHUNK = 128
SSD_CONV = 5
N_EXPERTS = 16
EC_CAPACITY = 2
HG_COLS = 5 * HG_W
AT_COLS = AT_W + 2 * KV_W
SSD_COLS = SSD_W + SSD_XBC + 128
IN_PAD = HG_COLS + AT_COLS + SSD_COLS
LANES = 128
ATTN_KEY_BLK = 1024
SLOT_TILE = 256
ROUTE_BLK = 512
ROUTE_WIN = 128
VMEM_LIMIT = 56 * 1024 * 1024


def _cp(sem, vmem=VMEM_LIMIT):
    return pltpu.CompilerParams(dimension_semantics=sem, vmem_limit_bytes=vmem)


def _dot(a, b, prec=None):
    return jnp.dot(a, b, preferred_element_type=F32, precision=prec)


def _dot_nt(a, b, prec=None):
    return lax.dot_general(a, b, (((1,), (1,)), ((), ())), preferred_element_type=F32, precision=prec)


def _dot_tn(a, b, prec=None):
    return lax.dot_general(a, b, (((0,), (0,)), ((), ())), preferred_element_type=F32, precision=prec)


def _sigmoid(x):
    return 1.0 / (1.0 + jnp.exp(-x))


def _silu(x):
    return x * _sigmoid(x)


def _softplus(x):
    return jnp.maximum(x, 0.0) + jnp.log1p(jnp.exp(-jnp.abs(x)))


def _iota(shape, dim):
    return lax.broadcasted_iota(I32, shape, dim)


def _head_mean_sq(x, width):
    bd = (_iota((width, width), 0) // HD == _iota((width, width), 1) // HD).astype(F32) * (1.0 / HD)
    return _dot(x * x, bd, HI)


def _mod_body(c_ref, w_ref, b_ref, o_ref):
    o_ref[0] = _dot(_silu(c_ref[...]), w_ref[0], HI) + b_ref[0]


def _mod_call(c_all, w_mod, b_mod):
    bp = c_all.shape[0]
    return pl.pallas_call(
        _mod_body, name="adaln_mod",
        grid=(DEPTH, 6),
        in_specs=[pl.BlockSpec((bp, D), lambda l, j: (0, 0)),
                  pl.BlockSpec((1, D, D), lambda l, j: (l, 0, j)),
                  pl.BlockSpec((1, 1, D), lambda l, j: (l, 0, j))],
        out_specs=pl.BlockSpec((1, bp, D), lambda l, j: (l, 0, j)),
        out_shape=jax.ShapeDtypeStruct((DEPTH, bp, 6 * D), F32),
        compiler_params=_cp(("arbitrary", "arbitrary")),
    )(c_all, w_mod, b_mod.reshape(DEPTH, 1, 6 * D))


def _inproj_body(x_ref, g_ref, sc_ref, sh_ref, w_ref, wdt_ref, hg_ref, at_ref, ssd_ref, dtt_ref):
    x = x_ref[0]
    ms = jnp.mean(x * x, axis=-1, keepdims=True)
    y = x * lax.rsqrt(ms + EPS) * g_ref[...]
    h = (y * (1.0 + sc_ref[0]) + sh_ref[0]).astype(BF16)
    p = _dot(h, w_ref[...])
    hg_ref[0] = p[:, :HG_COLS]
    at_ref[0] = p[:, HG_COLS:HG_COLS + AT_COLS]
    ssd_ref[0] = p[:, HG_COLS + AT_COLS:]
    dtt_ref[0] = _dot_nt(wdt_ref[...], h)


def _inproj_call(x, norm_g, scale, shift, w_main, w_dtt):
    b, l, _ = x.shape
    tm = min(l, 512)
    row = lambda: pl.BlockSpec((1, tm, D), lambda bi, i: (bi, i, 0))
    per_b = lambda: pl.BlockSpec((1, 1, D), lambda bi, i: (bi, 0, 0))
    in_specs = [row(), pl.BlockSpec((1, D), lambda bi, i: (0, 0)), per_b(), per_b(),
                pl.BlockSpec((D, IN_PAD), lambda bi, i: (0, 0)),
                pl.BlockSpec((8, D), lambda bi, i: (0, 0))]
    args = [x, norm_g.reshape(1, D), scale, shift, w_main, w_dtt]
    out_specs = [pl.BlockSpec((1, tm, HG_COLS), lambda bi, i: (bi, i, 0)),
                 pl.BlockSpec((1, tm, AT_COLS), lambda bi, i: (bi, i, 0)),
                 pl.BlockSpec((1, tm, SSD_COLS), lambda bi, i: (bi, i, 0)),
                 pl.BlockSpec((1, 8, tm), lambda bi, i: (bi, 0, i))]
    out_shape = [jax.ShapeDtypeStruct((b, l, HG_COLS), F32),
                 jax.ShapeDtypeStruct((b, l, AT_COLS), F32),
                 jax.ShapeDtypeStruct((b, l, SSD_COLS), F32),
                 jax.ShapeDtypeStruct((b, 8, l), F32)]
    return pl.pallas_call(
        _inproj_body, name="norm_inproj",
        grid=(b, l // tm), in_specs=in_specs, out_specs=out_specs, out_shape=out_shape,
        compiler_params=_cp(("arbitrary", "arbitrary")),
    )(*args)


def _hgrn_body(bwd, nch, *refs):
    if bwd:
        x_ref, lb_ref, ng_ref, of_ref, o_ref, st_ref = refs
    else:
        x_ref, lb_ref, o_ref, st_ref = refs

    @pl.when(pl.program_id(1) == 0)
    def _():
        st_ref[...] = jnp.zeros_like(st_ref)

    q_chunk = HG_CHUNK
    lane_head = _iota((1, HG_W), 1) // HD
    tril = (_iota((q_chunk, q_chunk), 1) <= _iota((q_chunk, q_chunk), 0)).astype(F32)
    bd_mask = _iota((HG_W, HG_W), 0) // HD == _iota((HG_W, HG_W), 1) // HD
    row_q = _iota((q_chunk, 1), 0)
    att_t = _iota((q_chunk, q_chunk), 0) % HG_SUB
    att_s = _iota((q_chunk, q_chunk), 1)
    log_lb = lb_ref[0:1, :]
    log_1m_lb = lb_ref[1:2, :]
    one_m_lb = lb_ref[2:3, :]
    fcol = 3 * HG_W if bwd else 2 * HG_W

    def chunk(ci, carry):
        c = (nch - 1 - ci) if bwd else ci
        r0 = pl.multiple_of(c * q_chunk, q_chunk)
        rows = pl.ds(r0, q_chunk)
        q = _silu(x_ref[0, rows, 0:HG_W])
        v = x_ref[0, rows, HG_W:2 * HG_W]
        fr = x_ref[0, rows, fcol:fcol + HG_W]
        a2 = log_1m_lb - _softplus(-fr)
        mx = jnp.maximum(log_lb, a2)
        logf = mx + jnp.log1p(jnp.exp(-jnp.abs(log_lb - a2)))
        k = one_m_lb * _sigmoid(-fr)
        b_inc = _dot(tril, logf, HI)
        b_exc = b_inc - logf
        total = b_inc[q_chunk - 1:q_chunk, :]
        st = st_ref[...]
        if bwd:
            q_in = q * jnp.exp(total - b_exc)
            k_st = k * jnp.exp(b_exc)
        else:
            q_in = q * jnp.exp(b_inc)
            k_st = k * jnp.exp(total - b_inc)
        inter = _dot_nt(q_in.astype(BF16), st.astype(BF16))
        upd = _dot_tn(v.astype(BF16), k_st.astype(BF16))
        st_ref[...] = st * jnp.exp(total) + jnp.where(bd_mask, upd, 0.0)
        v16 = v.astype(BF16)
        pieces = []
        for i in range(q_chunk // HG_SUB):
            lo, hi = i * HG_SUB, (i + 1) * HG_SUB
            if bwd:
                ref = b_inc[hi - 1:hi, :]
                qt = q[lo:hi] * jnp.exp(ref - b_exc[lo:hi])
                kt = k * jnp.exp(jnp.where(row_q >= lo, b_exc - ref, 0.0))
                kt = jnp.where(row_q >= lo, kt, 0.0)
                amask = att_s >= att_t + lo
            else:
                ref = b_exc[lo:lo + 1, :]
                qt = q[lo:hi] * jnp.exp(b_inc[lo:hi] - ref)
                kt = k * jnp.exp(jnp.where(row_q < hi, ref - b_inc, 0.0))
                kt = jnp.where(row_q < hi, kt, 0.0)
                amask = att_s <= att_t + lo
            q4 = jnp.concatenate([jnp.where(lane_head == h, qt, 0.0) for h in range(4)], axis=0)
            att = _dot_nt(q4.astype(BF16), kt.astype(BF16))
            att = jnp.where(amask, att, 0.0)
            r = _dot(att.astype(BF16), v16)
            o_i = jnp.where(lane_head == 0, r[0:HG_SUB], 0.0)
            for h in range(1, 4):
                o_i = o_i + jnp.where(lane_head == h, r[h * HG_SUB:(h + 1) * HG_SUB], 0.0)
            pieces.append(o_i + inter[lo:hi])
        o = jnp.concatenate(pieces, axis=0)
        if bwd:
            tot = of_ref[0, rows, :] + o
            ms = _head_mean_sq(tot, HG_W)
            g = x_ref[0, rows, 4 * HG_W:5 * HG_W]
            y = tot * lax.rsqrt(ms + EPS) * ng_ref[...] * _silu(g)
            o_ref[0, rows, :] = y.astype(o_ref.dtype)
        else:
            o_ref[0, rows, :] = o
        return carry

    lax.fori_loop(0, nch, chunk, 0, unroll=min(nch, 4))


def _hgrn_call(hg_in, lb_rows, norm_g, o_fwd):
    b, l, _ = hg_in.shape
    bwd = o_fwd is not None
    t = min(l, 512)
    nb = l // t
    idx = (lambda bi, j: (bi, nb - 1 - j, 0)) if bwd else (lambda bi, j: (bi, j, 0))
    in_specs = [pl.BlockSpec((1, t, HG_COLS), idx), pl.BlockSpec((3, HG_W), lambda bi, j: (0, 0))]
    args = [hg_in, lb_rows]
    if bwd:
        in_specs += [pl.BlockSpec((1, HG_W), lambda bi, j: (0, 0)), pl.BlockSpec((1, t, HG_W), idx)]
        args += [norm_g, o_fwd]
    return pl.pallas_call(
        functools.partial(_hgrn_body, bwd, t // HG_CHUNK), name="hgrn_bwd" if bwd else "hgrn_fwd",
        grid=(b, nb), in_specs=in_specs,
        out_specs=pl.BlockSpec((1, t, HG_W), idx),
        out_shape=jax.ShapeDtypeStruct((b, l, HG_W), BF16 if bwd else F32),
        scratch_shapes=[pltpu.VMEM((HG_W, HG_W), F32)],
        compiler_params=_cp(("arbitrary", "arbitrary")),
    )(*args)


def _rope_swap(x):
    w = x.shape[-1]
    first = (_iota((1, w), 1) % 32) < 16
    return jnp.where(first, pltpu.roll(x, w - 16, 1), pltpu.roll(x, 16, 1))


def _attn_prep_body(x_ref, cos_ref, sin_ref, gq_ref, gk_ref, ek_ref, evt_ref, eye_ref, qt_ref, k_ref, vt_ref):
    x = x_ref[0]
    cos2 = cos_ref[...]
    sin2 = sin_ref[...]
    xq = x[:, :AT_W]
    qn = xq * lax.rsqrt(_head_mean_sq(xq, AT_W) + EPS) * gq_ref[...]
    cos_q = jnp.concatenate([cos2] * 4, axis=1)
    sin_q = jnp.concatenate([sin2] * 4, axis=1)
    q = ((qn * cos_q + _rope_swap(qn) * sin_q) * QK_SCALE_LOG2).astype(BF16)
    qt_ref[0] = _dot_nt(eye_ref[...], q).astype(BF16)
    xk = x[:, AT_W:AT_W + KV_W]
    kn = xk * lax.rsqrt(_head_mean_sq(xk, KV_W) + EPS) * gk_ref[...]
    kr = (kn * cos2 + _rope_swap(kn) * sin2).astype(BF16)
    k_ref[0, 0] = _dot(kr, ek_ref[...]).astype(BF16)
    xv = x[:, AT_W + KV_W:].astype(BF16)
    ones_row = ((_iota((2 * LANES, 1), 0) % LANES) >= HD).astype(F32)
    vt_ref[0, 0] = (_dot_nt(evt_ref[...], xv) + ones_row).astype(BF16)


def _attn_prep_call(at_in, cos2, sin2, gq, gk, ek, evt, eye):
    b, l, _ = at_in.shape
    tm = min(l, ATTN_KEY_BLK)
    const = lambda shape: pl.BlockSpec(shape, lambda bi, i: (0, 0))
    return pl.pallas_call(
        _attn_prep_body, name="attn_prep",
        grid=(b, l // tm),
        in_specs=[pl.BlockSpec((1, tm, AT_COLS), lambda bi, i: (bi, i, 0)),
                  pl.BlockSpec((tm, KV_W), lambda bi, i: (i, 0)),
                  pl.BlockSpec((tm, KV_W), lambda bi, i: (i, 0)),
                  const((1, AT_W)), const((1, KV_W)), const((KV_W, AT_W)), const((2 * LANES, KV_W)),
                  const((AT_W, AT_W))],
        out_specs=[pl.BlockSpec((1, AT_W, tm), lambda bi, i: (bi, 0, i)),
                   pl.BlockSpec((1, 1, tm, AT_W), lambda bi, i: (bi, i, 0, 0)),
                   pl.BlockSpec((1, 1, 2 * LANES, tm), lambda bi, i: (bi, i, 0, 0))],
        out_shape=[jax.ShapeDtypeStruct((b, AT_W, l), BF16),
                   jax.ShapeDtypeStruct((b, l // tm, tm, AT_W), BF16),
                   jax.ShapeDtypeStruct((b, l // tm, 2 * LANES, tm), BF16)],
        compiler_params=_cp(("arbitrary", "arbitrary")),
    )(at_in, cos2, sin2, gq, gk, ek, evt, eye)


def _attn_body(tq, nk, qt_ref, k_ref, vt_ref, place_ref, o_ref, qm_ref, s_ref, m_ref, acc_ref):
    qm_ref[...] = jnp.zeros_like(qm_ref)
    for g in range(AT_GROUP):
        qm_ref[g * HD:(g + 1) * HD, g * tq:(g + 1) * tq] = qt_ref[0, g * HD:(g + 1) * HD, :]
    m_ref[...] = jnp.full_like(m_ref, -jnp.inf)
    acc_ref[...] = jnp.zeros_like(acc_ref)

    def scores(slot, kk):
        s_ref[slot] = _dot(k_ref[0, kk], qm_ref[...]).astype(BF16)

    def consume(slot, kk):
        s = s_ref[slot]
        m_old = m_ref[...]
        m_new = jnp.maximum(m_old, jnp.max(s, axis=0, keepdims=True).astype(F32))
        alpha = jnp.exp2(m_old - m_new)
        p = jnp.exp2(s - m_new.astype(BF16))
        acc_ref[...] = acc_ref[...] * alpha + _dot(vt_ref[0, kk], p)
        m_ref[...] = m_new

    scores(0, 0)
    if nk > 1:
        def pair(i, carry):
            kk = 2 * i
            scores(1, kk + 1)
            consume(0, kk)
            scores(0, kk + 2)
            consume(1, kk + 1)
            return carry

        lax.fori_loop(0, nk // 2 - 1, pair, 0)
        scores(1, nk - 1)
        consume(0, nk - 2)
        consume(1, nk - 1)
    else:
        consume(0, 0)
    acc = acc_ref[...]
    o_t = (acc[0:HD] / acc[HD:HD + 1]).astype(BF16)
    out = _dot_tn(o_t[:, 0:tq], place_ref[0])
    for g in range(1, AT_GROUP):
        out = out + _dot_tn(o_t[:, g * tq:(g + 1) * tq], place_ref[g])
    o_ref[0] = out.astype(o_ref.dtype)


def _attn_call(qt, kt, vt, place):
    b, _, l = qt.shape
    _, nk, tk, _ = kt.shape
    assert nk == 1 or nk % 2 == 0
    tq = min(l, 512)
    return pl.pallas_call(
        functools.partial(_attn_body, tq, nk), name="flash_attn",
        grid=(b, 2, l // tq),
        in_specs=[pl.BlockSpec((1, 4 * HD, tq), lambda bi, j, i: (bi, j, i)),
                  pl.BlockSpec((1, nk, tk, 4 * HD), lambda bi, j, i: (bi, 0, 0, j)),
                  pl.BlockSpec((1, nk, LANES, tk), lambda bi, j, i: (bi, 0, j, 0)),
                  pl.BlockSpec((AT_GROUP, HD, 4 * HD), lambda bi, j, i: (0, 0, 0))],
        out_specs=pl.BlockSpec((1, tq, 4 * HD), lambda bi, j, i: (bi, i, j)),
        out_shape=jax.ShapeDtypeStruct((b, l, AT_W), BF16),
        scratch_shapes=[pltpu.VMEM((4 * HD, AT_GROUP * tq), BF16),
                        pltpu.VMEM((2, tk, AT_GROUP * tq), BF16),
                        pltpu.VMEM((1, AT_GROUP * tq), F32),
                        pltpu.VMEM((LANES, AT_GROUP * tq), F32)],
        compiler_params=_cp(("arbitrary",) * 3),
    )(qt, kt, vt, place)


def _ssd_conv_body(t, x_ref, prev_ref, next_ref, w_ref, b_ref, o_ref, ext_ref):
    j = pl.program_id(1)
    nb = pl.num_programs(1)
    lo, hi = SSD_W, SSD_W + SSD_XBC
    ext_ref[0:8, :] = jnp.where(j > 0, prev_ref[0, :, lo:hi], 0.0)
    ext_ref[8:8 + t, :] = x_ref[0, :, lo:hi]
    ext_ref[8 + t:16 + t, :] = jnp.where(j < nb - 1, next_ref[0, :, lo:hi], 0.0)
    pad = SSD_CONV // 2
    acc = b_ref[...] + w_ref[0:1, :] * ext_ref[pl.ds(8 - pad, t), :]
    for kk in range(1, SSD_CONV):
        acc = acc + w_ref[kk:kk + 1, :] * ext_ref[pl.ds(8 - pad + kk, t), :]
    o_ref[0] = _silu(acc)


def _ssd_conv_call(ssd_in, conv_w, conv_b):
    b, l, _ = ssd_in.shape
    t = min(l, 512)
    t8 = t // 8
    last8 = l // 8 - 1
    return pl.pallas_call(
        functools.partial(_ssd_conv_body, t), name="ssd_conv",
        grid=(b, l // t),
        in_specs=[pl.BlockSpec((1, t, SSD_COLS), lambda bi, j: (bi, j, 0)),
                  pl.BlockSpec((1, 8, SSD_COLS), lambda bi, j: (bi, jnp.maximum(j * t8 - 1, 0), 0)),
                  pl.BlockSpec((1, 8, SSD_COLS), lambda bi, j: (bi, jnp.minimum((j + 1) * t8, last8), 0)),
                  pl.BlockSpec((8, SSD_XBC), lambda bi, j: (0, 0)),
                  pl.BlockSpec((1, SSD_XBC), lambda bi, j: (0, 0))],
        out_specs=pl.BlockSpec((1, t, SSD_XBC), lambda bi, j: (bi, j, 0)),
        out_shape=jax.ShapeDtypeStruct((b, l, SSD_XBC), F32),
        scratch_shapes=[pltpu.VMEM((t + 16, SSD_XBC), F32)],
        compiler_params=_cp(("arbitrary", "arbitrary")),
    )(ssd_in, ssd_in, ssd_in, conv_w, conv_b)


def _ssd_body(bwd, nch, *refs):
    if bwd:
        xbc_ref, dtc_ref, dtr_ref, par_ref, parc_ref, z_ref, yf_ref, ng_ref, o_ref, st_ref = refs
    else:
        xbc_ref, dtc_ref, dtr_ref, par_ref, parc_ref, o_ref, st_ref = refs

    @pl.when(pl.program_id(1) == 0)
    def _():
        st_ref[...] = jnp.zeros_like(st_ref)

    qc = SSD_CHUNK
    dsel = 4 if bwd else 0
    lane_head = _iota((1, SSD_W), 1) // HD
    lane_grp = _iota((1, LANES), 1) // HD
    tril = (_iota((qc, qc), 1) <= _iota((qc, qc), 0)).astype(F32)
    triu = (_iota((qc, qc), 0) <= _iota((qc, qc), 1)).astype(F32)
    tt = _iota((qc, qc), 0)
    ss = _iota((qc, qc), 1)
    st_mask = (_iota((SSD_W, LANES), 0) // (2 * HD)) == (_iota((SSD_W, LANES), 1) // HD)
    bias_row = par_ref[0:1, 0:LANES]
    acoef_row = par_ref[1:2, 0:LANES]
    dskip_row = par_ref[2:3, :]
    bias_col = parc_ref[:, 0:1]
    acoef_col = parc_ref[:, 1:2]

    def expand(col_vals):
        out = jnp.broadcast_to(col_vals[:, dsel:dsel + 1], (qc, SSD_W))
        for h in range(1, 4):
            out = jnp.where(lane_head == h, jnp.broadcast_to(col_vals[:, dsel + h:dsel + h + 1], (qc, SSD_W)), out)
        return out

    def chunk(ci, carry):
        c = (nch - 1 - ci) if bwd else ci
        r0 = pl.multiple_of(c * qc, qc)
        rows = pl.ds(r0, qc)
        xs = xbc_ref[0, rows, 0:SSD_W]
        bm = xbc_ref[0, rows, SSD_W:SSD_W + LANES]
        cm = xbc_ref[0, rows, SSD_W + LANES:SSD_W + 2 * LANES]
        dt_c = _softplus(dtc_ref[0, rows, :] + bias_row)
        a_c = dt_c * acoef_row
        dt_r = _softplus(dtr_ref[0, :, rows] + bias_col)
        a_r = dt_r * acoef_col
        inc_c = _dot(tril, a_c, HI)
        inc_r = _dot(a_r, triu, HI)
        if bwd:
            cum_c, cum_r = inc_c - a_c, inc_r - a_r
        else:
            cum_c, cum_r = inc_c, inc_r
        total_c = inc_c[qc - 1:qc, :]
        xdt = xs * expand(dt_c)
        xdt16 = xdt.astype(BF16)
        bm16 = bm.astype(BF16)
        cm16 = cm.astype(BF16)
        gmat = [_dot_nt(jnp.where(lane_grp == g, cm, 0.0).astype(BF16), bm16) for g in range(2)]
        y = jnp.zeros((qc, SSD_W), F32)
        for h in range(4):
            col = cum_c[:, dsel + h:dsel + h + 1]
            rw = cum_r[dsel + h:dsel + h + 1, :]
            if bwd:
                dec = jnp.where(ss >= tt, jnp.exp(jnp.where(ss >= tt, rw - col, 0.0)), 0.0)
            else:
                dec = jnp.where(ss <= tt, jnp.exp(jnp.where(ss <= tt, col - rw, 0.0)), 0.0)
            yh = _dot((gmat[h // 2] * dec).astype(BF16), xdt16)
            y = jnp.where(lane_head == h, yh, y)
        st = st_ref[...]
        if bwd:
            out_dec = jnp.exp(total_c - cum_c)
            st_dec = jnp.exp(cum_c)
        else:
            out_dec = jnp.exp(cum_c)
            st_dec = jnp.exp(total_c - cum_c)
        y = y + _dot_nt(cm16, st.astype(BF16)) * expand(out_dec)
        upd = _dot_tn((xdt * expand(st_dec)).astype(BF16), bm16)
        tot_rows = jnp.broadcast_to(jnp.exp(total_c[:, dsel:dsel + 1]), (HD, LANES))
        decay_rows = jnp.concatenate(
            [tot_rows] + [jnp.broadcast_to(jnp.exp(total_c[:, dsel + h:dsel + h + 1]), (HD, LANES))
                          for h in range(1, 4)], axis=0)
        st_ref[...] = st * decay_rows + jnp.where(st_mask, upd, 0.0)
        if bwd:
            yy = yf_ref[0, rows, :] + y + xs * dskip_row
            yy = yy * _silu(z_ref[0, rows, :])
            ms = _head_mean_sq(yy, SSD_W)
            o_ref[0, rows, :] = (yy * lax.rsqrt(ms + EPS) * ng_ref[...]).astype(o_ref.dtype)
        else:
            o_ref[0, rows, :] = y
        return carry

    lax.fori_loop(0, nch, chunk, 0, unroll=min(nch, 4))


def _ssd_call(xbc, ssd_in, dtt, par_rows, par_cols, norm_g, y_fwd):
    b, l, _ = xbc.shape
    bwd = y_fwd is not None
    t = min(l, 512)
    nb = l // t
    blk = (lambda bi, j: (bi, nb - 1 - j)) if bwd else (lambda bi, j: (bi, j))
    rows3 = lambda lane_blk: (lambda bi, j: blk(bi, j) + (lane_blk,))
    in_specs = [pl.BlockSpec((1, t, SSD_XBC), rows3(0)),
                pl.BlockSpec((1, t, LANES), rows3((SSD_W + SSD_XBC) // LANES)),
                pl.BlockSpec((1, 8, t), lambda bi, j: (bi, 0, blk(bi, j)[1])),
                pl.BlockSpec((8, SSD_W), lambda bi, j: (0, 0)),
                pl.BlockSpec((8, LANES), lambda bi, j: (0, 0))]
    args = [xbc, ssd_in, dtt, par_rows, par_cols]
    if bwd:
        in_specs += [pl.BlockSpec((1, t, SSD_W), rows3(0)),
                     pl.BlockSpec((1, t, SSD_W), rows3(0)),
                     pl.BlockSpec((1, SSD_W), lambda bi, j: (0, 0))]
        args += [ssd_in, y_fwd, norm_g]
    return pl.pallas_call(
        functools.partial(_ssd_body, bwd, t // SSD_CHUNK), name="ssd_bwd" if bwd else "ssd_fwd",
        grid=(b, nb), in_specs=in_specs,
        out_specs=pl.BlockSpec((1, t, SSD_W), rows3(0)),
        out_shape=jax.ShapeDtypeStruct((b, l, SSD_W), BF16 if bwd else F32),
        scratch_shapes=[pltpu.VMEM((SSD_W, LANES), F32)],
        compiler_params=_cp(("arbitrary", "arbitrary")),
    )(*args)


def _outproj_body(x_ref, hg_ref, at_ref, ssd_ref, w1_ref, w2_ref, w3_ref, g1_ref, ng_ref, sc_ref, sh_ref,
                  wr_ref, x1_ref, h2_ref, aff_ref, afft_ref):
    mix = _dot(hg_ref[0], w1_ref[...]) + _dot(at_ref[0], w2_ref[...]) + _dot(ssd_ref[0], w3_ref[...])
    x1 = x_ref[0] + g1_ref[0] * mix
    x1_ref[0] = x1
    ms = jnp.mean(x1 * x1, axis=-1, keepdims=True)
    h2 = x1 * lax.rsqrt(ms + EPS) * ng_ref[...] * (1.0 + sc_ref[0]) + sh_ref[0]
    h2_ref[0] = h2.astype(h2_ref.dtype)
    logits = _dot(h2, wr_ref[...], HI)
    valid = _iota((1, LANES), 1) < N_EXPERTS
    logits = jnp.where(valid, logits, -jnp.inf)
    e = jnp.exp(logits - jnp.max(logits, axis=-1, keepdims=True))
    aff = e / jnp.sum(e, axis=-1, keepdims=True)
    aff_ref[0] = aff
    afft_ref[...] = jnp.transpose(aff)[0:N_EXPERTS, :]


def _outproj_call(x, o_hg, o_at, o_ssd, w1, w2, w3, gate1, norm_g, scale, shift, w_r):
    b, l, _ = x.shape
    tm = min(l, 512)
    nbl = l // tm
    row = lambda w: pl.BlockSpec((1, tm, w), lambda bi, i: (bi, i, 0))
    per_b = lambda: pl.BlockSpec((1, 1, D), lambda bi, i: (bi, 0, 0))
    const = lambda shape: pl.BlockSpec(shape, lambda bi, i: (0, 0))
    return pl.pallas_call(
        _outproj_body, name="outproj_router",
        grid=(b, nbl),
        in_specs=[row(D), row(HG_W), row(AT_W), row(SSD_W), const((HG_W, D)), const((AT_W, D)), const((SSD_W, D)),
                  per_b(), const((1, D)), per_b(), per_b(), const((D, LANES))],
        out_specs=[row(D), row(D), row(LANES),
                   pl.BlockSpec((N_EXPERTS, tm), lambda bi, i: (0, bi * nbl + i))],
        out_shape=[jax.ShapeDtypeStruct((b, l, D), F32), jax.ShapeDtypeStruct((b, l, D), BF16),
                   jax.ShapeDtypeStruct((b, l, LANES), F32), jax.ShapeDtypeStruct((N_EXPERTS, b * l), F32)],
        compiler_params=_cp(("arbitrary", "arbitrary")),
    )(x, o_hg, o_at, o_ssd, w1, w2, w3, gate1, norm_g.reshape(1, D), scale, shift, w_r)


def _route_thr_body(cap, n_tok, afft_ref, o_ref, t_ref):
    lane_chunk = min(n_tok, 2048)
    n_chunks = n_tok // lane_chunk

    def count_ge(cand):
        def body(i, acc):
            bits = lax.bitcast_convert_type(afft_ref[:, pl.ds(pl.multiple_of(i * lane_chunk, lane_chunk), lane_chunk)], I32)
            return acc + (bits >= cand).astype(F32)
        acc = lax.fori_loop(0, n_chunks, body, jnp.zeros((N_EXPERTS, lane_chunk), F32))
        return jnp.sum(acc, axis=1, keepdims=True)

    def bit_step(i, thr):
        cand = thr | jnp.left_shift(jnp.int32(1), 30 - i)
        return jnp.where(count_ge(cand) >= cap, cand, thr)

    thr = lax.fori_loop(0, 31, bit_step, jnp.zeros((N_EXPERTS, 1), I32))
    n_gt = count_ge(thr + 1)
    need = cap - n_gt

    nblk = n_tok // ROUTE_BLK
    lane = _iota((1, LANES), 1)

    def blk_counts(j, carry):
        gt_tab, eq_tab = carry
        bits = lax.bitcast_convert_type(afft_ref[:, pl.ds(pl.multiple_of(j * ROUTE_BLK, ROUTE_BLK), ROUTE_BLK)], I32)
        cg = jnp.sum((bits > thr).astype(F32), axis=1, keepdims=True)
        ce = jnp.sum((bits == thr).astype(F32), axis=1, keepdims=True)
        return (gt_tab + jnp.where(lane == j, cg, 0.0), eq_tab + jnp.where(lane == j, ce, 0.0))

    zeros = jnp.zeros((N_EXPERTS, LANES), F32)
    gt_tab, eq_tab = lax.fori_loop(0, nblk, blk_counts, (zeros, zeros))
    strict = (_iota((LANES, LANES), 0) < _iota((LANES, LANES), 1)).astype(F32)
    eq_before = _dot(eq_tab, strict, HI)
    eq_take = jnp.clip(need - eq_before, 0.0, eq_tab)
    cnt = gt_tab + eq_take
    cnt8 = jnp.floor((cnt + 7.0) * 0.125) * 8.0
    off = _dot(cnt8, strict, HI)
    o_ref[0] = cnt.astype(I32)
    o_ref[1] = off.astype(I32)
    o_ref[2] = jnp.broadcast_to(thr, (N_EXPERTS, LANES))
    o_ref[3] = jnp.broadcast_to(need.astype(I32), (N_EXPERTS, LANES))
    o_ref[4] = eq_before.astype(I32)
    ident = (_iota((N_EXPERTS, LANES), 0) == _iota((N_EXPERTS, LANES), 1)).astype(F32)
    flip = lambda tab: _dot_tn(tab, ident, HI).astype(I32)
    thr_b = jnp.broadcast_to(thr, (N_EXPERTS, LANES))
    thr_hi = flip(jnp.right_shift(thr_b, 15).astype(F32))
    thr_lo = flip(jnp.bitwise_and(thr_b, 0x7FFF).astype(F32))
    t_ref[0] = jnp.left_shift(thr_hi, 15) | thr_lo
    t_ref[1] = flip(jnp.broadcast_to(need, (N_EXPERTS, LANES)))
    t_ref[2] = flip(eq_before)


def _route_thr_call(afft, cap):
    n_tok = afft.shape[1]
    return pl.pallas_call(
        functools.partial(_route_thr_body, cap, n_tok), name="route_threshold",
        grid=(1,),
        in_specs=[pl.BlockSpec((N_EXPERTS, n_tok), lambda i: (0, 0))],
        out_specs=[pl.BlockSpec((5, N_EXPERTS, LANES), lambda i: (0, 0, 0)),
                   pl.BlockSpec((3, LANES, LANES), lambda i: (0, 0, 0))],
        out_shape=[jax.ShapeDtypeStruct((5, N_EXPERTS, LANES), I32),
                   jax.ShapeDtypeStruct((3, LANES, LANES), I32)],
        compiler_params=_cp(("arbitrary",)),
    )(afft)


def _route_select(aff_ref, tab_ref, tri_ref):
    j = pl.program_id(0)
    bits = lax.bitcast_convert_type(aff_ref[...], I32)
    thr = tab_ref[0, 0:1, :]
    need = tab_ref[1, 0:1, :].astype(F32)
    eq_before = tab_ref[2, pl.ds(j, 1), :].astype(F32)
    gt = bits > thr
    eq = bits == thr
    tri = tri_ref[...]
    eq_rank = _dot(tri, eq.astype(BF16)) + eq_before
    sel = gt | (eq & (eq_rank < need))
    rank = _dot(tri, sel.astype(BF16))
    return jnp.where(sel, rank, -1.0)


def _onehot_t(sel_rank, e, wi):
    slot = (_iota((1, ROUTE_WIN), 1) + wi * ROUTE_WIN).astype(F32)
    return jnp.where(sel_rank[:, e:e + 1] == slot, 1.0, 0.0).astype(BF16)


def _gather_body(nblk, cnt_s, off_s, afft_ref, tab_ref, tri_ref, h2_ref, xg_in, xg_hbm, oh_ref, stage, sem):
    del xg_in
    j = pl.program_id(0)
    w = ROUTE_WIN
    aff = afft_ref[...]
    bits = lax.bitcast_convert_type(aff, I32)
    lane = _iota((1, LANES), 1)
    thr = tab_ref[2, :, 0:1]
    need = tab_ref[3, :, 0:1].astype(F32)
    eq_before = jnp.sum(jnp.where(lane == j, tab_ref[4], 0), axis=1, keepdims=True).astype(F32)
    gt = bits > thr
    eq = bits == thr
    tri = tri_ref[...]
    eq_rank = _dot_nt(eq.astype(BF16), tri) + eq_before
    sel = gt | (eq & (eq_rank < need))
    rank = _dot_nt(sel.astype(BF16), tri)
    sel_rank = jnp.where(sel, rank, -1.0)
    h2 = h2_ref[...]

    def onehot(e, wi):
        slot = (_iota((w, 1), 0) + wi * w).astype(F32)
        return sel_rank[e:e + 1] == slot

    def gate_lanes(e, oh):
        gate = jnp.sum(jnp.where(oh, aff[e:e + 1], 0.0), axis=1, keepdims=True)
        return jnp.broadcast_to(gate, (w, LANES))

    def group_copy(e, off, wi, r):
        return pltpu.make_async_copy(stage.at[e, pl.ds(r * 8, 8)],
                                     xg_hbm.at[e, pl.ds(pl.multiple_of(off + wi * w + r * 8, 8), 8)], sem.at[e])

    def groups(cnt, wi):
        return (jnp.minimum(cnt - wi * w, w) + 7) // 8

    def start_groups(e, cnt, off, wi):
        lax.fori_loop(0, groups(cnt, wi), lambda r, c: (group_copy(e, off, wi, r).start(), c)[1], 0)

    def wait_groups(e, cnt, off, wi):
        lax.fori_loop(0, groups(cnt, wi), lambda r, c: (group_copy(e, off, wi, r).wait(), c)[1], 0)

    for e in range(N_EXPERTS):
        oh = onehot(e, 0)
        oh_ref[e * w:(e + 1) * w, :] = jnp.where(oh, 1.0, 0.0).astype(BF16)
        stage[e, :, D:] = gate_lanes(e, oh)
    stage[:, :, 0:D] = _dot(oh_ref[...], h2).reshape(N_EXPERTS, w, D)
    for e in range(N_EXPERTS):
        start_groups(e, cnt_s[e * nblk + j], off_s[e * nblk + j], 0)

    for e in range(N_EXPERTS):
        cnt = cnt_s[e * nblk + j]
        off = off_s[e * nblk + j]
        n_win = (cnt + (w - 1)) // w

        def window(wi, carry, e=e, cnt=cnt, off=off):
            wait_groups(e, cnt, off, wi - 1)
            oh = onehot(e, wi)
            stage[e, :, 0:D] = _dot(jnp.where(oh, 1.0, 0.0).astype(BF16), h2)
            stage[e, :, D:] = gate_lanes(e, oh)
            start_groups(e, cnt, off, wi)
            return carry

        lax.fori_loop(1, n_win, window, 0)

    for e in range(N_EXPERTS):
        cnt = cnt_s[e * nblk + j]
        off = off_s[e * nblk + j]
        wait_groups(e, cnt, off, jnp.maximum((cnt + (w - 1)) // w - 1, 0))


def _gather_call(cnt, off, afft, tab_e, tri, h2_flat, slots_alloc):
    n_tok = h2_flat.shape[0]
    nblk = n_tok // ROUTE_BLK
    xg0 = jnp.zeros((N_EXPERTS, slots_alloc, D + LANES), F32)
    grid_spec = pltpu.PrefetchScalarGridSpec(
        num_scalar_prefetch=2,
        grid=(nblk,),
        in_specs=[pl.BlockSpec((N_EXPERTS, ROUTE_BLK), lambda j, *_: (0, j)),
                  pl.BlockSpec((5, N_EXPERTS, LANES), lambda j, *_: (0, 0, 0)),
                  pl.BlockSpec((ROUTE_BLK, ROUTE_BLK), lambda j, *_: (0, 0)),
                  pl.BlockSpec((ROUTE_BLK, D), lambda j, *_: (j, 0)),
                  pl.BlockSpec(memory_space=pl.ANY)],
        out_specs=pl.BlockSpec(memory_space=pl.ANY),
        scratch_shapes=[pltpu.VMEM((N_EXPERTS * ROUTE_WIN, ROUTE_BLK), BF16),
                        pltpu.VMEM((N_EXPERTS, ROUTE_WIN, D + LANES), F32),
                        pltpu.SemaphoreType.DMA((N_EXPERTS,))],
    )
    return pl.pallas_call(
        functools.partial(_gather_body, nblk), name="expert_gather",
        grid_spec=grid_spec,
        out_shape=jax.ShapeDtypeStruct((N_EXPERTS, slots_alloc, D + LANES), F32),
        input_output_aliases={6: 0},
        compiler_params=_cp(("arbitrary",)),
    )(cnt, off, afft, tab_e, tri, h2_flat, xg0)


def _ffn_body(x_ref, wg_ref, wu_ref, wd_ref, y_ref):
    xg = x_ref[0, :, 0:D].astype(BF16)
    gate = x_ref[0, :, D:D + 1]
    hid = _silu(_dot(xg, wg_ref[0])) * _dot(xg, wu_ref[0])
    y_ref[0] = _dot(hid.astype(BF16), wd_ref[0]) * gate


def _ffn_call(xg, wg, wu, wd):
    _, slots_alloc, _ = xg.shape
    wspec = lambda: pl.BlockSpec((1, D, D), lambda e, i: (e, 0, 0))
    return pl.pallas_call(
        _ffn_body, name="expert_ffn",
        grid=(N_EXPERTS, slots_alloc // SLOT_TILE),
        in_specs=[pl.BlockSpec((1, SLOT_TILE, D + LANES), lambda e, i: (e, i, 0)), wspec(), wspec(), wspec()],
        out_specs=pl.BlockSpec((1, SLOT_TILE, D), lambda e, i: (e, i, 0)),
        out_shape=jax.ShapeDtypeStruct((N_EXPERTS, slots_alloc, D), F32),
        compiler_params=_cp(("arbitrary", "arbitrary")),
    )(xg, wg, wu, wd)


def _combine_body(final, nblk, cnt_s, off_s, aff_ref, tab_ref, tri_ref, x1_ref, g2_ref, fg_ref, y_hbm,
                  o_ref, ybuf, ysplit, oh_ref, acc_ref, sems):
    j = pl.program_id(0)
    sel_rank = _route_select(aff_ref, tab_ref, tri_ref)
    w = ROUTE_WIN

    def win_copy(e, off, wi):
        return pltpu.make_async_copy(y_hbm.at[e, pl.ds(pl.multiple_of(off + wi * w, 8), w)], ybuf.at[e], sems.at[e])

    def split(yw):
        y_hi = yw.astype(BF16)
        return y_hi, (yw - y_hi.astype(F32)).astype(BF16)

    for e in range(N_EXPERTS):
        win_copy(e, off_s[e * nblk + j], 0).start()
    for e in range(N_EXPERTS):
        oh = _onehot_t(sel_rank, e, 0)
        oh_ref[:, (2 * e) * w:(2 * e + 1) * w] = oh
        oh_ref[:, (2 * e + 1) * w:(2 * e + 2) * w] = oh
    for e in range(N_EXPERTS):
        win_copy(e, off_s[e * nblk + j], 0).wait()
        y_hi, y_lo = split(ybuf[e])
        ysplit[(2 * e) * w:(2 * e + 1) * w, :] = y_hi
        ysplit[(2 * e + 1) * w:(2 * e + 2) * w, :] = y_lo
    acc_ref[...] = _dot(oh_ref[...], ysplit[...])
    for e in range(N_EXPERTS):
        cnt = cnt_s[e * nblk + j]
        off = off_s[e * nblk + j]

        def window(wi, carry, e=e, off=off):
            cp = win_copy(e, off, wi)
            cp.start()
            cp.wait()
            y_hi, y_lo = split(ybuf[e])
            oh = _onehot_t(sel_rank, e, wi)
            acc_ref[...] += _dot(oh, y_hi) + _dot(oh, y_lo)
            return carry

        lax.fori_loop(1, (cnt + (w - 1)) // w, window, 0)
    x2 = x1_ref[...] + g2_ref[0] * acc_ref[...]
    if final:
        ms = jnp.mean(x2 * x2, axis=-1, keepdims=True)
        x2 = x2 * lax.rsqrt(ms + EPS) * fg_ref[...]
    o_ref[...] = x2


def _combine_call(cnt, off, aff2d, tab_t, tri, x1_flat, gate2, final_g, y, blocks_per_batch, final):
    n_tok = x1_flat.shape[0]
    nblk = n_tok // ROUTE_BLK
    grid_spec = pltpu.PrefetchScalarGridSpec(
        num_scalar_prefetch=2,
        grid=(nblk,),
        in_specs=[pl.BlockSpec((ROUTE_BLK, LANES), lambda j, *_: (j, 0)),
                  pl.BlockSpec((3, LANES, LANES), lambda j, *_: (0, 0, 0)),
                  pl.BlockSpec((ROUTE_BLK, ROUTE_BLK), lambda j, *_: (0, 0)),
                  pl.BlockSpec((ROUTE_BLK, D), lambda j, *_: (j, 0)),
                  pl.BlockSpec((1, 1, D), lambda j, *_: (j // blocks_per_batch, 0, 0)),
                  pl.BlockSpec((1, D), lambda j, *_: (0, 0)),
                  pl.BlockSpec(memory_space=pl.ANY)],
        out_specs=pl.BlockSpec((ROUTE_BLK, D), lambda j, *_: (j, 0)),
        scratch_shapes=[pltpu.VMEM((N_EXPERTS, ROUTE_WIN, D), F32),
                        pltpu.VMEM((2 * N_EXPERTS * ROUTE_WIN, D), BF16),
                        pltpu.VMEM((ROUTE_BLK, 2 * N_EXPERTS * ROUTE_WIN), BF16),
                        pltpu.VMEM((ROUTE_BLK, D), F32),
                        pltpu.SemaphoreType.DMA((N_EXPERTS,))],
    )
    return pl.pallas_call(
        functools.partial(_combine_body, final, nblk), name="expert_combine",
        grid_spec=grid_spec,
        out_shape=jax.ShapeDtypeStruct((n_tok, D), F32),
        compiler_params=_cp(("arbitrary",)),
    )(cnt, off, aff2d, tab_t, tri, x1_flat, gate2, final_g.reshape(1, D), y)


def _rope_tables(l):
    quarter = HD // 4
    inv = ROPE_THETA ** (-jnp.arange(quarter, dtype=F32) / quarter)
    t = jnp.arange(l)
    pos = jnp.stack([(t // GRID_W).astype(F32), (t % GRID_W).astype(F32)], axis=1)
    lane = np.arange(KV_W)
    which = (lane % HD) // (HD // 2)
    ang = pos[:, which] * inv[lane % quarter][None, :]
    sign = np.where((lane % (HD // 2)) < quarter, -1.0, 1.0).astype(np.float32)
    return jnp.cos(ang), jnp.sin(ang) * sign[None, :]


def _prep_layer(l, p):
    w_in = p['w_in'][l]
    n_main = HG_COLS + AT_COLS + SSD_W + SSD_XBC
    w_main = jnp.concatenate([w_in[:, :n_main], jnp.pad(w_in[:, n_main:], ((0, 0), (0, 120)))], axis=1).astype(BF16)
    w_dtt = w_in[:, n_main:].T.astype(BF16)
    lb_all = jnp.cumsum(jax.nn.softmax(p['hg_lb'].astype(F32), axis=0), axis=0)
    lb = (lb_all - lb_all[:1])[l]
    lb_rows = jnp.stack([jnp.log(lb), jnp.log1p(-lb), 1.0 - lb], axis=1)
    a_coef = -jnp.exp(p['a_log'][l].astype(F32)).reshape(-1)
    dt_bias = p['dt_bias'][l].astype(F32).reshape(-1)
    par_rows = jnp.zeros((8, SSD_W), F32).at[0, :8].set(dt_bias).at[1, :8].set(a_coef)
    par_rows = par_rows.at[2, :].set(jnp.repeat(p['d_skip'][l].astype(F32), HD))
    par_cols = jnp.zeros((8, LANES), F32).at[:, 0].set(dt_bias).at[:, 1].set(a_coef)
    w_out = p['w_out'][l].astype(BF16)
    lane_k = np.arange(KV_W)
    ek = np.zeros((KV_W, AT_W), np.float32)
    for rep in range(AT_GROUP):
        ek[lane_k, (lane_k // HD) * (AT_GROUP * HD) + rep * HD + lane_k % HD] = 1.0
    evt = np.zeros((2 * LANES, KV_W), np.float32)
    evt[(lane_k // HD) * LANES + lane_k % HD, lane_k] = 1.0
    place = np.zeros((AT_GROUP, HD, AT_GROUP * HD), np.float32)
    for g in range(AT_GROUP):
        place[g, np.arange(HD), g * HD + np.arange(HD)] = 1.0
    return dict(
        w_main=w_main, w_dtt=w_dtt, lb_rows=lb_rows,
        hg_norm=jnp.tile(p['hg_norm_g'][l], 4).reshape(1, HG_W),
        gq=jnp.tile(p['q_norm_g'][l], 8).reshape(1, AT_W), gk=jnp.tile(p['k_norm_g'][l], 2).reshape(1, KV_W),
        ek=jnp.asarray(ek, BF16), evt=jnp.asarray(evt, BF16), place=jnp.asarray(place, BF16),
        eye=jnp.asarray(np.eye(AT_W, dtype=np.float32), BF16),
        conv_w=jnp.pad(p['conv_w'][l], ((0, 8 - SSD_CONV), (0, 0))), conv_b=p['conv_b'][l].reshape(1, SSD_XBC),
        par_rows=par_rows, par_cols=par_cols,
        ssd_norm=jnp.tile(p['ssd_norm_g'][l], 4).reshape(1, SSD_W),
        w_o1=w_out[:HG_W], w_o2=w_out[HG_W:HG_W + AT_W], w_o3=w_out[HG_W + AT_W:],
        w_r=jnp.pad(p['w_router'][l], ((0, 0), (0, LANES - N_EXPERTS))),
        wg=p['w_gate'][l].astype(BF16), wu=p['w_up'][l].astype(BF16), wd=p['w_down'][l].astype(BF16),
    )


def _trunk(x, mod, layers, p):
    b, l, _ = x.shape
    n_tok = b * l
    cap = EC_CAPACITY * n_tok // N_EXPERTS
    nblk = n_tok // ROUTE_BLK
    slots = cap + 8 * nblk
    n_tiles = -(-slots // SLOT_TILE)
    slots_alloc = -(-(n_tiles * SLOT_TILE + ROUTE_WIN) // SLOT_TILE) * SLOT_TILE
    cos2, sin2 = _rope_tables(l)
    tri = (np.arange(ROUTE_BLK)[None, :] < np.arange(ROUTE_BLK)[:, None]).astype(np.float32)
    tri = jnp.asarray(tri, BF16)
    for li in range(DEPTH):
        w = layers[li]
        sh1, sc1, g1, sh2, sc2, g2 = [mod[li, :, i * D:(i + 1) * D].reshape(b, 1, D) for i in range(6)]
        hg_in, at_in, ssd_in, dtt = _inproj_call(x, p['norm1_g'][li], sc1, sh1, w['w_main'], w['w_dtt'])
        o_f = _hgrn_call(hg_in, w['lb_rows'][0], w['hg_norm'], None)
        o_hg = _hgrn_call(hg_in, w['lb_rows'][1], w['hg_norm'], o_f)
        qt, kt, vt = _attn_prep_call(at_in, cos2, sin2, w['gq'], w['gk'], w['ek'], w['evt'], w['eye'])
        o_at = _attn_call(qt, kt, vt, w['place'])
        xbc = _ssd_conv_call(ssd_in, w['conv_w'], w['conv_b'])
        y_f = _ssd_call(xbc, ssd_in, dtt, w['par_rows'], w['par_cols'], w['ssd_norm'], None)
        o_ssd = _ssd_call(xbc, ssd_in, dtt, w['par_rows'], w['par_cols'], w['ssd_norm'], y_f)
        x1, h2, aff, afft = _outproj_call(x, o_hg, o_at, o_ssd, w['w_o1'], w['w_o2'], w['w_o3'], g1,
                                          p['norm2_g'][li], sc2, sh2, w['w_r'])
        tab_e, tab_t = _route_thr_call(afft, cap)
        cnt = tab_e[0, :, :nblk].reshape(-1)
        off = tab_e[1, :, :nblk].reshape(-1)
        aff2d = aff.reshape(n_tok, LANES)
        xg = _gather_call(cnt, off, afft, tab_e, tri, h2.reshape(n_tok, D), slots_alloc)
        y = _ffn_call(xg, w['wg'], w['wu'], w['wd'])
        x = _combine_call(cnt, off, aff2d, tab_t, tri, x1.reshape(n_tok, D), g2, p['final_g'], y,
                          l // ROUTE_BLK, li == DEPTH - 1).reshape(b, l, D)
    return x


def kernel(x_prompt, x_sample, c_prompt, c_sample, norm1_g, norm2_g, w_mod, b_mod, w_in, hg_lb, hg_norm_g,
           q_norm_g, k_norm_g, conv_w, conv_b, a_log, dt_bias, d_skip, ssd_norm_g, w_out, w_router,
           w_gate, w_up, w_down, final_g):
    p = dict(norm1_g=norm1_g, norm2_g=norm2_g, w_in=w_in, hg_lb=hg_lb, hg_norm_g=hg_norm_g, q_norm_g=q_norm_g,
             k_norm_g=k_norm_g, conv_w=conv_w, conv_b=conv_b, a_log=a_log, dt_bias=dt_bias, d_skip=d_skip,
             ssd_norm_g=ssd_norm_g, w_out=w_out, w_router=w_router, w_gate=w_gate, w_up=w_up, w_down=w_down,
             final_g=final_g)
    bp, bs = c_prompt.shape[0], c_sample.shape[0]
    rows = -(-(bp + bs) // 8) * 8
    c_all = jnp.pad(jnp.concatenate([c_prompt, c_sample], axis=0), ((0, rows - bp - bs), (0, 0)))
    mod = _mod_call(c_all, w_mod, b_mod)
    layers = [_prep_layer(li, p) for li in range(DEPTH)]
    y_prompt = _trunk(x_prompt, mod[:, :bp], layers, p)
    y_sample = _trunk(x_sample, mod[:, bp:bp + bs], layers, p)
    return (y_prompt, y_sample)
```

```python
import functools

import numpy as np
import jax
import jax.numpy as jnp
from jax import lax
from jax.experimental import pallas as pl
from jax.experimental.pallas import tpu as pltpu

F32 = jnp.float32
BF16 = jnp.bfloat16
I32 = jnp.int32
HI = lax.Precision.HIGHEST

D = 1024
DEPTH = 2
GRID_W = 64
EPS = 1e-6
HD = 64
HG_W = 256
HG_CHUNK = 64
HG_SUB = 16
AT_W = 512
KV_W = 128
AT_GROUP = 4
ROPE_THETA = 10000.0
QK_SCALE_LOG2 = (HD ** -0.5) * 1.4426950408889634
SSD_W = 256
SSD_XBC = 512
SSD_CHUNK = 128
SSD_CONV = 5
N_EXPERTS = 16
EC_CAPACITY = 2
HG_COLS = 5 * HG_W
AT_COLS = AT_W + 2 * KV_W
SSD_COLS = SSD_W + SSD_XBC + 128
IN_PAD = HG_COLS + AT_COLS + SSD_COLS
LANES = 128
ATTN_KEY_BLK = 512
SLOT_TILE = 256
ROUTE_BLK = 256
ROUTE_WIN = 64
VMEM_LIMIT = 56 * 1024 * 1024


def _cp(sem, vmem=VMEM_LIMIT):
    return pltpu.CompilerParams(dimension_semantics=sem, vmem_limit_bytes=vmem)


def _dot(a, b, prec=None):
    return jnp.dot(a, b, preferred_element_type=F32, precision=prec)


def _dot_nt(a, b, prec=None):
    return lax.dot_general(a, b, (((1,), (1,)), ((), ())), preferred_element_type=F32, precision=prec)


def _dot_tn(a, b, prec=None):
    return lax.dot_general(a, b, (((0,), (0,)), ((), ())), preferred_element_type=F32, precision=prec)


def _sigmoid(x):
    return 1.0 / (1.0 + jnp.exp(-x))


def _silu(x):
    return x * _sigmoid(x)


def _softplus(x):
    return jnp.maximum(x, 0.0) + jnp.log1p(jnp.exp(-jnp.abs(x)))


def _iota(shape, dim):
    return lax.broadcasted_iota(I32, shape, dim)


def _split_bf16(x):
    hi = x.astype(BF16)
    return hi, (x - hi.astype(F32)).astype(BF16)


def _head_mean_sq(x, width):
    bd = jnp.where(_iota((width, width), 0) // HD == _iota((width, width), 1) // HD, 1.0 / HD, 0.0).astype(BF16)
    hi, lo = _split_bf16(x * x)
    return _dot(hi, bd) + _dot(lo, bd)


def _mod_body(c_ref, w_ref, b_ref, o_ref):
    o_ref[0] = _dot(_silu(c_ref[...]), w_ref[0], HI) + b_ref[0]


def _mod_call(c_all, w_mod, b_mod):
    bp = c_all.shape[0]
    return pl.pallas_call(
        _mod_body, name="adaln_mod",
        grid=(DEPTH, 6),
        in_specs=[pl.BlockSpec((bp, D), lambda l, j: (0, 0)),
                  pl.BlockSpec((1, D, D), lambda l, j: (l, 0, j)),
                  pl.BlockSpec((1, 1, D), lambda l, j: (l, 0, j))],
        out_specs=pl.BlockSpec((1, bp, D), lambda l, j: (l, 0, j)),
        out_shape=jax.ShapeDtypeStruct((DEPTH, bp, 6 * D), F32),
        compiler_params=_cp(("arbitrary", "arbitrary")),
    )(c_all, w_mod, b_mod.reshape(DEPTH, 1, 6 * D))


def _inproj_body(x_ref, g_ref, sc_ref, sh_ref, w_ref, wdt_ref, hg_ref, at_ref, ssd_ref, dtt_ref):
    x = x_ref[0]
    ms = jnp.mean(x * x, axis=-1, keepdims=True)
    y = x * lax.rsqrt(ms + EPS) * g_ref[...]
    h = (y * (1.0 + sc_ref[0]) + sh_ref[0]).astype(BF16)
    p = _dot(h, w_ref[...])
    hg_ref[0] = p[:, :HG_COLS]
    at_ref[0] = p[:, HG_COLS:HG_COLS + AT_COLS]
    ssd_ref[0] = p[:, HG_COLS + AT_COLS:]
    dtt_ref[0] = _dot_nt(wdt_ref[...], h)


def _inproj_call(x, norm_g, scale, shift, w_main, w_dtt):
    b, l, _ = x.shape
    tm = min(l, 512)
    row = lambda: pl.BlockSpec((1, tm, D), lambda bi, i: (bi, i, 0))
    per_b = lambda: pl.BlockSpec((1, 1, D), lambda bi, i: (bi, 0, 0))
    in_specs = [row(), pl.BlockSpec((1, D), lambda bi, i: (0, 0)), per_b(), per_b(),
                pl.BlockSpec((D, IN_PAD), lambda bi, i: (0, 0)),
                pl.BlockSpec((8, D), lambda bi, i: (0, 0))]
    args = [x, norm_g.reshape(1, D), scale, shift, w_main, w_dtt]
    out_specs = [pl.BlockSpec((1, tm, HG_COLS), lambda bi, i: (bi, i, 0)),
                 pl.BlockSpec((1, tm, AT_COLS), lambda bi, i: (bi, i, 0)),
                 pl.BlockSpec((1, tm, SSD_COLS), lambda bi, i: (bi, i, 0)),
                 pl.BlockSpec((1, 8, tm), lambda bi, i: (bi, 0, i))]
    out_shape = [jax.ShapeDtypeStruct((b, l, HG_COLS), F32),
                 jax.ShapeDtypeStruct((b, l, AT_COLS), F32),
                 jax.ShapeDtypeStruct((b, l, SSD_COLS), F32),
                 jax.ShapeDtypeStruct((b, 8, l), F32)]
    return pl.pallas_call(
        _inproj_body, name="norm_inproj",
        grid=(b, l // tm), in_specs=in_specs, out_specs=out_specs, out_shape=out_shape,
        compiler_params=_cp(("arbitrary", "arbitrary")),
    )(*args)


def _hgrn_body(bwd, nch, *refs):
    if bwd:
        x_ref, lb_ref, ng_ref, of_ref, o_ref, st_ref = refs
    else:
        x_ref, lb_ref, o_ref, st_ref = refs

    @pl.when(pl.program_id(1) == 0)
    def _():
        st_ref[...] = jnp.zeros_like(st_ref)

    q_chunk = HG_CHUNK
    lane_head = _iota((1, HG_W), 1) // HD
    tril = (_iota((q_chunk, q_chunk), 1) <= _iota((q_chunk, q_chunk), 0)).astype(F32)
    bd_mask = _iota((HG_W, HG_W), 0) // HD == _iota((HG_W, HG_W), 1) // HD
    row_q = _iota((q_chunk, 1), 0)
    att_t = _iota((q_chunk, q_chunk), 0) % HG_SUB
    att_s = _iota((q_chunk, q_chunk), 1)
    log_lb = lb_ref[0:1, :]
    log_1m_lb = lb_ref[1:2, :]
    one_m_lb = lb_ref[2:3, :]
    fcol = 3 * HG_W if bwd else 2 * HG_W

    def chunk(ci, carry):
        c = (nch - 1 - ci) if bwd else ci
        r0 = pl.multiple_of(c * q_chunk, q_chunk)
        rows = pl.ds(r0, q_chunk)
        q = _silu(x_ref[0, rows, 0:HG_W])
        v = x_ref[0, rows, HG_W:2 * HG_W]
        fr = x_ref[0, rows, fcol:fcol + HG_W]
        a2 = log_1m_lb - _softplus(-fr)
        mx = jnp.maximum(log_lb, a2)
        logf = mx + jnp.log1p(jnp.exp(-jnp.abs(log_lb - a2)))
        k = one_m_lb * _sigmoid(-fr)
        b_inc = _dot(tril, logf, HI)
        b_exc = b_inc - logf
        total = b_inc[q_chunk - 1:q_chunk, :]
        st = st_ref[...]
        if bwd:
            q_in = q * jnp.exp(total - b_exc)
            k_st = k * jnp.exp(b_exc)
        else:
            q_in = q * jnp.exp(b_inc)
            k_st = k * jnp.exp(total - b_inc)
        inter = _dot_nt(q_in.astype(BF16), st.astype(BF16))
        upd = _dot_tn(v.astype(BF16), k_st.astype(BF16))
        st_ref[...] = st * jnp.exp(total) + jnp.where(bd_mask, upd, 0.0)
        v16 = v.astype(BF16)
        pieces = []
        for i in range(q_chunk // HG_SUB):
            lo, hi = i * HG_SUB, (i + 1) * HG_SUB
            if bwd:
                ref = b_inc[hi - 1:hi, :]
                qt = q[lo:hi] * jnp.exp(ref - b_exc[lo:hi])
                kt = k * jnp.exp(jnp.where(row_q >= lo, b_exc - ref, 0.0))
                kt = jnp.where(row_q >= lo, kt, 0.0)
                amask = att_s >= att_t + lo
            else:
                ref = b_exc[lo:lo + 1, :]
                qt = q[lo:hi] * jnp.exp(b_inc[lo:hi] - ref)
                kt = k * jnp.exp(jnp.where(row_q < hi, ref - b_inc, 0.0))
                kt = jnp.where(row_q < hi, kt, 0.0)
                amask = att_s <= att_t + lo
            q4 = jnp.concatenate([jnp.where(lane_head == h, qt, 0.0) for h in range(4)], axis=0)
            att = _dot_nt(q4.astype(BF16), kt.astype(BF16))
            att = jnp.where(amask, att, 0.0)
            r = _dot(att.astype(BF16), v16)
            o_i = jnp.where(lane_head == 0, r[0:HG_SUB], 0.0)
            for h in range(1, 4):
                o_i = o_i + jnp.where(lane_head == h, r[h * HG_SUB:(h + 1) * HG_SUB], 0.0)
            pieces.append(o_i + inter[lo:hi])
        o = jnp.concatenate(pieces, axis=0)
        if bwd:
            tot = of_ref[0, rows, :] + o
            ms = _head_mean_sq(tot, HG_W)
            g = x_ref[0, rows, 4 * HG_W:5 * HG_W]
            y = tot * lax.rsqrt(ms + EPS) * ng_ref[...] * _silu(g)
            o_ref[0, rows, :] = y.astype(o_ref.dtype)
        else:
            o_ref[0, rows, :] = o
        return carry

    lax.fori_loop(0, nch, chunk, 0, unroll=min(nch, 4))


def _hgrn_call(hg_in, lb_rows, norm_g, o_fwd):
    b, l, _ = hg_in.shape
    bwd = o_fwd is not None
    t = min(l, 512)
    nb = l // t
    idx = (lambda bi, j: (bi, nb - 1 - j, 0)) if bwd else (lambda bi, j: (bi, j, 0))
    in_specs = [pl.BlockSpec((1, t, HG_COLS), idx), pl.BlockSpec((3, HG_W), lambda bi, j: (0, 0))]
    args = [hg_in, lb_rows]
    if bwd:
        in_specs += [pl.BlockSpec((1, HG_W), lambda bi, j: (0, 0)), pl.BlockSpec((1, t, HG_W), idx)]
        args += [norm_g, o_fwd]
    return pl.pallas_call(
        functools.partial(_hgrn_body, bwd, t // HG_CHUNK), name="hgrn_bwd" if bwd else "hgrn_fwd",
        grid=(b, nb), in_specs=in_specs,
        out_specs=pl.BlockSpec((1, t, HG_W), idx),
        out_shape=jax.ShapeDtypeStruct((b, l, HG_W), BF16 if bwd else F32),
        scratch_shapes=[pltpu.VMEM((HG_W, HG_W), F32)],
        compiler_params=_cp(("arbitrary", "arbitrary")),
    )(*args)


def _rope_swap(x):
    w = x.shape[-1]
    first = (_iota((1, w), 1) % 32) < 16
    return jnp.where(first, pltpu.roll(x, w - 16, 1), pltpu.roll(x, 16, 1))


def _attn_prep_body(x_ref, cos_ref, sin_ref, gq_ref, gk_ref, ek_ref, evt_ref, eye_ref, qt_ref, k_ref, vt_ref):
    x = x_ref[0]
    cos2 = cos_ref[...]
    sin2 = sin_ref[...]
    xq = x[:, :AT_W]
    qn = xq * lax.rsqrt(_head_mean_sq(xq, AT_W) + EPS) * gq_ref[...]
    cos_q = jnp.concatenate([cos2] * 4, axis=1)
    sin_q = jnp.concatenate([sin2] * 4, axis=1)
    q = ((qn * cos_q + _rope_swap(qn) * sin_q) * QK_SCALE_LOG2).astype(BF16)
    qt_ref[0] = _dot_nt(eye_ref[...], q).astype(BF16)
    xk = x[:, AT_W:AT_W + KV_W]
    kn = xk * lax.rsqrt(_head_mean_sq(xk, KV_W) + EPS) * gk_ref[...]
    kr = (kn * cos2 + _rope_swap(kn) * sin2).astype(BF16)
    k_ref[0, 0] = _dot(kr, ek_ref[...]).astype(BF16)
    xv = x[:, AT_W + KV_W:].astype(BF16)
    ones_row = ((_iota((2 * LANES, 1), 0) % LANES) >= HD).astype(F32)
    vt_ref[0, 0] = (_dot_nt(evt_ref[...], xv) + ones_row).astype(BF16)


def _attn_prep_call(at_in, cos2, sin2, gq, gk, ek, evt, eye):
    b, l, _ = at_in.shape
    tm = min(l, ATTN_KEY_BLK)
    const = lambda shape: pl.BlockSpec(shape, lambda bi, i: (0, 0))
    return pl.pallas_call(
        _attn_prep_body, name="attn_prep",
        grid=(b, l // tm),
        in_specs=[pl.BlockSpec((1, tm, AT_COLS), lambda bi, i: (bi, i, 0)),
                  pl.BlockSpec((tm, KV_W), lambda bi, i: (i, 0)),
                  pl.BlockSpec((tm, KV_W), lambda bi, i: (i, 0)),
                  const((1, AT_W)), const((1, KV_W)), const((KV_W, AT_W)), const((2 * LANES, KV_W)),
                  const((AT_W, AT_W))],
        out_specs=[pl.BlockSpec((1, AT_W, tm), lambda bi, i: (bi, 0, i)),
                   pl.BlockSpec((1, 1, tm, AT_W), lambda bi, i: (bi, i, 0, 0)),
                   pl.BlockSpec((1, 1, 2 * LANES, tm), lambda bi, i: (bi, i, 0, 0))],
        out_shape=[jax.ShapeDtypeStruct((b, AT_W, l), BF16),
                   jax.ShapeDtypeStruct((b, l // tm, tm, AT_W), BF16),
                   jax.ShapeDtypeStruct((b, l // tm, 2 * LANES, tm), BF16)],
        compiler_params=_cp(("arbitrary", "arbitrary")),
    )(at_in, cos2, sin2, gq, gk, ek, evt, eye)


def _attn_body(tq, nk, qt_ref, k_ref, vt_ref, place_ref, o_ref, qm_ref, s_ref, m_ref, acc_ref):
    qm_ref[...] = jnp.zeros_like(qm_ref)
    for g in range(AT_GROUP):
        qm_ref[g * HD:(g + 1) * HD, g * tq:(g + 1) * tq] = qt_ref[0, g * HD:(g + 1) * HD, :]
    m_ref[...] = jnp.full_like(m_ref, -jnp.inf)
    acc_ref[...] = jnp.zeros_like(acc_ref)

    def scores(slot, kk):
        s_ref[slot] = _dot(k_ref[0, kk], qm_ref[...]).astype(BF16)

    def consume(slot, kk):
        s = s_ref[slot]
        m_old = m_ref[...]
        m_new = jnp.maximum(m_old, jnp.max(s, axis=0, keepdims=True).astype(F32))
        alpha = jnp.exp2(m_old - m_new)
        p = jnp.exp2(s - m_new.astype(BF16))
        acc_ref[...] = acc_ref[...] * alpha + _dot(vt_ref[0, kk], p)
        m_ref[...] = m_new

    scores(0, 0)
    if nk > 1:
        def pair(i, carry):
            kk = 2 * i
            scores(1, kk + 1)
            consume(0, kk)
            scores(0, kk + 2)
            consume(1, kk + 1)
            return carry

        lax.fori_loop(0, nk // 2 - 1, pair, 0)
        scores(1, nk - 1)
        consume(0, nk - 2)
        consume(1, nk - 1)
    else:
        consume(0, 0)
    acc = acc_ref[...]
    o_t = (acc[0:HD] / acc[HD:HD + 1]).astype(BF16)
    out = _dot_tn(o_t[:, 0:tq], place_ref[0])
    for g in range(1, AT_GROUP):
        out = out + _dot_tn(o_t[:, g * tq:(g + 1) * tq], place_ref[g])
    o_ref[0] = out.astype(o_ref.dtype)


def _attn_call(qt, kt, vt, place):
    b, _, l = qt.shape
    _, nk, tk, _ = kt.shape
    assert nk == 1 or nk % 2 == 0
    tq = min(l, 512)
    return pl.pallas_call(
        functools.partial(_attn_body, tq, nk), name="flash_attn",
        grid=(b, 2, l // tq),
        in_specs=[pl.BlockSpec((1, 4 * HD, tq), lambda bi, j, i: (bi, j, i)),
                  pl.BlockSpec((1, nk, tk, 4 * HD), lambda bi, j, i: (bi, 0, 0, j)),
                  pl.BlockSpec((1, nk, LANES, tk), lambda bi, j, i: (bi, 0, j, 0)),
                  pl.BlockSpec((AT_GROUP, HD, 4 * HD), lambda bi, j, i: (0, 0, 0))],
        out_specs=pl.BlockSpec((1, tq, 4 * HD), lambda bi, j, i: (bi, i, j)),
        out_shape=jax.ShapeDtypeStruct((b, l, AT_W), BF16),
        scratch_shapes=[pltpu.VMEM((4 * HD, AT_GROUP * tq), BF16),
                        pltpu.VMEM((2, tk, AT_GROUP * tq), BF16),
                        pltpu.VMEM((1, AT_GROUP * tq), F32),
                        pltpu.VMEM((LANES, AT_GROUP * tq), F32)],
        compiler_params=_cp(("arbitrary",) * 3),
    )(qt, kt, vt, place)


def _ssd_conv_body(t, x_ref, prev_ref, next_ref, w_ref, b_ref, o_ref, ext_ref):
    j = pl.program_id(1)
    nb = pl.num_programs(1)
    lo, hi = SSD_W, SSD_W + SSD_XBC
    ext_ref[0:8, :] = jnp.where(j > 0, prev_ref[0, :, lo:hi], 0.0)
    ext_ref[8:8 + t, :] = x_ref[0, :, lo:hi]
    ext_ref[8 + t:16 + t, :] = jnp.where(j < nb - 1, next_ref[0, :, lo:hi], 0.0)
    pad = SSD_CONV // 2
    acc = b_ref[...] + w_ref[0:1, :] * ext_ref[pl.ds(8 - pad, t), :]
    for kk in range(1, SSD_CONV):
        acc = acc + w_ref[kk:kk + 1, :] * ext_ref[pl.ds(8 - pad + kk, t), :]
    o_ref[0] = _silu(acc)


def _ssd_conv_call(ssd_in, conv_w, conv_b):
    b, l, _ = ssd_in.shape
    t = min(l, 512)
    t8 = t // 8
    last8 = l // 8 - 1
    return pl.pallas_call(
        functools.partial(_ssd_conv_body, t), name="ssd_conv",
        grid=(b, l // t),
        in_specs=[pl.BlockSpec((1, t, SSD_COLS), lambda bi, j: (bi, j, 0)),
                  pl.BlockSpec((1, 8, SSD_COLS), lambda bi, j: (bi, jnp.maximum(j * t8 - 1, 0), 0)),
                  pl.BlockSpec((1, 8, SSD_COLS), lambda bi, j: (bi, jnp.minimum((j + 1) * t8, last8), 0)),
                  pl.BlockSpec((8, SSD_XBC), lambda bi, j: (0, 0)),
                  pl.BlockSpec((1, SSD_XBC), lambda bi, j: (0, 0))],
        out_specs=pl.BlockSpec((1, t, SSD_XBC), lambda bi, j: (bi, j, 0)),
        out_shape=jax.ShapeDtypeStruct((b, l, SSD_XBC), F32),
        scratch_shapes=[pltpu.VMEM((t + 16, SSD_XBC), F32)],
        compiler_params=_cp(("arbitrary", "arbitrary")),
    )(ssd_in, ssd_in, ssd_in, conv_w, conv_b)


def _ssd_body(bwd, nch, *refs):
    if bwd:
        xbc_ref, dtc_ref, dtr_ref, par_ref, parc_ref, z_ref, yf_ref, ng_ref, o_ref, st_ref = refs
    else:
        xbc_ref, dtc_ref, dtr_ref, par_ref, parc_ref, o_ref, st_ref = refs

    @pl.when(pl.program_id(1) == 0)
    def _():
        st_ref[...] = jnp.zeros_like(st_ref)

    qc = SSD_CHUNK
    dsel = 4 if bwd else 0
    lane_head = _iota((1, SSD_W), 1) // HD
    lane_grp = _iota((1, LANES), 1) // HD
    tril = (_iota((qc, qc), 1) <= _iota((qc, qc), 0)).astype(F32)
    triu = (_iota((qc, qc), 0) <= _iota((qc, qc), 1)).astype(F32)
    tt = _iota((qc, qc), 0)
    ss = _iota((qc, qc), 1)
    st_mask = (_iota((SSD_W, LANES), 0) // (2 * HD)) == (_iota((SSD_W, LANES), 1) // HD)
    bias_row = par_ref[0:1, 0:LANES]
    acoef_row = par_ref[1:2, 0:LANES]
    dskip_row = par_ref[2:3, :]
    bias_col = parc_ref[:, 0:1]
    acoef_col = parc_ref[:, 1:2]

    def expand(col_vals):
        out = jnp.broadcast_to(col_vals[:, dsel:dsel + 1], (qc, SSD_W))
        for h in range(1, 4):
            out = jnp.where(lane_head == h, jnp.broadcast_to(col_vals[:, dsel + h:dsel + h + 1], (qc, SSD_W)), out)
        return out

    def chunk(ci, carry):
        c = (nch - 1 - ci) if bwd else ci
        r0 = pl.multiple_of(c * qc, qc)
        rows = pl.ds(r0, qc)
        xs = xbc_ref[0, rows, 0:SSD_W]
        bm = xbc_ref[0, rows, SSD_W:SSD_W + LANES]
        cm = xbc_ref[0, rows, SSD_W + LANES:SSD_W + 2 * LANES]
        dt_c = _softplus(dtc_ref[0, rows, :] + bias_row)
        a_c = dt_c * acoef_row
        dt_r = _softplus(dtr_ref[0, :, rows] + bias_col)
        a_r = dt_r * acoef_col
        inc_c = _dot(tril, a_c, HI)
        inc_r = _dot(a_r, triu, HI)
        if bwd:
            cum_c, cum_r = inc_c - a_c, inc_r - a_r
        else:
            cum_c, cum_r = inc_c, inc_r
        total_c = inc_c[qc - 1:qc, :]
        xdt = xs * expand(dt_c)
        xdt16 = xdt.astype(BF16)
        bm16 = bm.astype(BF16)
        cm16 = cm.astype(BF16)
        gmat = [_dot_nt(jnp.where(lane_grp == g, cm, 0.0).astype(BF16), bm16) for g in range(2)]
        y = jnp.zeros((qc, SSD_W), F32)
        for h in range(4):
            col = cum_c[:, dsel + h:dsel + h + 1]
            rw = cum_r[dsel + h:dsel + h + 1, :]
            if bwd:
                dec = jnp.where(ss >= tt, jnp.exp(jnp.where(ss >= tt, rw - col, 0.0)), 0.0)
            else:
                dec = jnp.where(ss <= tt, jnp.exp(jnp.where(ss <= tt, col - rw, 0.0)), 0.0)
            yh = _dot((gmat[h // 2] * dec).astype(BF16), xdt16)
            y = jnp.where(lane_head == h, yh, y)
        st = st_ref[...]
        if bwd:
            out_dec = jnp.exp(total_c - cum_c)
            st_dec = jnp.exp(cum_c)
        else:
            out_dec = jnp.exp(cum_c)
            st_dec = jnp.exp(total_c - cum_c)
        y = y + _dot_nt(cm16, st.astype(BF16)) * expand(out_dec)
        upd = _dot_tn((xdt * expand(st_dec)).astype(BF16), bm16)
        tot_rows = jnp.broadcast_to(jnp.exp(total_c[:, dsel:dsel + 1]), (HD, LANES))
        decay_rows = jnp.concatenate(
            [tot_rows] + [jnp.broadcast_to(jnp.exp(total_c[:, dsel + h:dsel + h + 1]), (HD, LANES))
                          for h in range(1, 4)], axis=0)
        st_ref[...] = st * decay_rows + jnp.where(st_mask, upd, 0.0)
        if bwd:
            yy = yf_ref[0, rows, :] + y + xs * dskip_row
            yy = yy * _silu(z_ref[0, rows, :])
            ms = _head_mean_sq(yy, SSD_W)
            o_ref[0, rows, :] = (yy * lax.rsqrt(ms + EPS) * ng_ref[...]).astype(o_ref.dtype)
        else:
            o_ref[0, rows, :] = y
        return carry

    lax.fori_loop(0, nch, chunk, 0, unroll=min(nch, 4))


def _ssd_call(xbc, ssd_in, dtt, par_rows, par_cols, norm_g, y_fwd):
    b, l, _ = xbc.shape
    bwd = y_fwd is not None
    t = min(l, 512)
    nb = l // t
    blk = (lambda bi, j: (bi, nb - 1 - j)) if bwd else (lambda bi, j: (bi, j))
    rows3 = lambda lane_blk: (lambda bi, j: blk(bi, j) + (lane_blk,))
    in_specs = [pl.BlockSpec((1, t, SSD_XBC), rows3(0)),
                pl.BlockSpec((1, t, LANES), rows3((SSD_W + SSD_XBC) // LANES)),
                pl.BlockSpec((1, 8, t), lambda bi, j: (bi, 0, blk(bi, j)[1])),
                pl.BlockSpec((8, SSD_W), lambda bi, j: (0, 0)),
                pl.BlockSpec((8, LANES), lambda bi, j: (0, 0))]
    args = [xbc, ssd_in, dtt, par_rows, par_cols]
    if bwd:
        in_specs += [pl.BlockSpec((1, t, SSD_W), rows3(0)),
                     pl.BlockSpec((1, t, SSD_W), rows3(0)),
                     pl.BlockSpec((1, SSD_W), lambda bi, j: (0, 0))]
        args += [ssd_in, y_fwd, norm_g]
    return pl.pallas_call(
        functools.partial(_ssd_body, bwd, t // SSD_CHUNK), name="ssd_bwd" if bwd else "ssd_fwd",
        grid=(b, nb), in_specs=in_specs,
        out_specs=pl.BlockSpec((1, t, SSD_W), rows3(0)),
        out_shape=jax.ShapeDtypeStruct((b, l, SSD_W), BF16 if bwd else F32),
        scratch_shapes=[pltpu.VMEM((SSD_W, LANES), F32)],
        compiler_params=_cp(("arbitrary", "arbitrary")),
    )(*args)


def _outproj_body(x_ref, hg_ref, at_ref, ssd_ref, w1_ref, w2_ref, w3_ref, g1_ref, ng_ref, sc_ref, sh_ref,
                  wr_ref, x1_ref, h2_ref, aff_ref, afft_ref):
    mix = _dot(hg_ref[0], w1_ref[...]) + _dot(at_ref[0], w2_ref[...]) + _dot(ssd_ref[0], w3_ref[...])
    x1 = x_ref[0] + g1_ref[0] * mix
    x1_ref[0] = x1
    ms = jnp.mean(x1 * x1, axis=-1, keepdims=True)
    h2 = x1 * lax.rsqrt(ms + EPS) * ng_ref[...] * (1.0 + sc_ref[0]) + sh_ref[0]
    h2_ref[0] = h2.astype(h2_ref.dtype)
    h_hi, h_lo = _split_bf16(h2)
    logits = _dot(h_hi, wr_ref[0]) + _dot(h_lo, wr_ref[0]) + _dot(h_hi, wr_ref[1])
    valid = _iota((1, LANES), 1) < N_EXPERTS
    logits = jnp.where(valid, logits, -jnp.inf)
    e = jnp.exp(logits - jnp.max(logits, axis=-1, keepdims=True))
    aff = e / jnp.sum(e, axis=-1, keepdims=True)
    aff_ref[0] = aff
    afft_ref[...] = jnp.transpose(aff)[0:N_EXPERTS, :]


def _outproj_call(x, o_hg, o_at, o_ssd, w1, w2, w3, gate1, norm_g, scale, shift, w_r):
    b, l, _ = x.shape
    tm = min(l, 512)
    nbl = l // tm
    row = lambda w: pl.BlockSpec((1, tm, w), lambda bi, i: (bi, i, 0))
    per_b = lambda: pl.BlockSpec((1, 1, D), lambda bi, i: (bi, 0, 0))
    const = lambda shape: pl.BlockSpec(shape, lambda bi, i: (0, 0))
    return pl.pallas_call(
        _outproj_body, name="outproj_router",
        grid=(b, nbl),
        in_specs=[row(D), row(HG_W), row(AT_W), row(SSD_W), const((HG_W, D)), const((AT_W, D)), const((SSD_W, D)),
                  per_b(), const((1, D)), per_b(), per_b(),
                  pl.BlockSpec((2, D, LANES), lambda bi, i: (0, 0, 0))],
        out_specs=[row(D), row(D), row(LANES),
                   pl.BlockSpec((N_EXPERTS, tm), lambda bi, i: (0, bi * nbl + i))],
        out_shape=[jax.ShapeDtypeStruct((b, l, D), F32), jax.ShapeDtypeStruct((b, l, D), BF16),
                   jax.ShapeDtypeStruct((b, l, LANES), F32), jax.ShapeDtypeStruct((N_EXPERTS, b * l), F32)],
        compiler_params=_cp(("arbitrary", "arbitrary")),
    )(x, o_hg, o_at, o_ssd, w1, w2, w3, gate1, norm_g.reshape(1, D), scale, shift, w_r)


def _route_thr_body(cap, n_tok, afft_ref, o_ref, t_ref):
    lane_chunk = 2048 if n_tok % 2048 == 0 else ROUTE_BLK
    n_chunks = n_tok // lane_chunk

    def count_ge(cand):
        def body(i, acc):
            bits = lax.bitcast_convert_type(afft_ref[:, pl.ds(pl.multiple_of(i * lane_chunk, lane_chunk), lane_chunk)], I32)
            return acc + (bits >= cand).astype(F32)
        acc = lax.fori_loop(0, n_chunks, body, jnp.zeros((N_EXPERTS, lane_chunk), F32))
        return jnp.sum(acc, axis=1, keepdims=True)

    def bit_step(i, thr):
        cand = thr | jnp.left_shift(jnp.int32(1), 30 - i)
        return jnp.where(count_ge(cand) >= cap, cand, thr)

    thr = lax.fori_loop(0, 31, bit_step, jnp.zeros((N_EXPERTS, 1), I32))
    n_gt = count_ge(thr + 1)
    need = cap - n_gt

    nblk = n_tok // ROUTE_BLK
    nbp = o_ref.shape[2]
    lane = _iota((1, nbp), 1)

    def blk_counts(j, carry):
        gt_tab, eq_tab = carry
        bits = lax.bitcast_convert_type(afft_ref[:, pl.ds(pl.multiple_of(j * ROUTE_BLK, ROUTE_BLK), ROUTE_BLK)], I32)
        cg = jnp.sum((bits > thr).astype(F32), axis=1, keepdims=True)
        ce = jnp.sum((bits == thr).astype(F32), axis=1, keepdims=True)
        return (gt_tab + jnp.where(lane == j, cg, 0.0), eq_tab + jnp.where(lane == j, ce, 0.0))

    zeros = jnp.zeros((N_EXPERTS, nbp), F32)
    gt_tab, eq_tab = lax.fori_loop(0, nblk, blk_counts, (zeros, zeros))
    strict = (_iota((nbp, nbp), 0) < _iota((nbp, nbp), 1)).astype(F32)
    eq_before = _dot(eq_tab, strict, HI)
    eq_take = jnp.clip(need - eq_before, 0.0, eq_tab)
    cnt = gt_tab + eq_take
    cnt8 = jnp.floor((cnt + 7.0) * 0.125) * 8.0
    off = _dot(cnt8, strict, HI)
    o_ref[0] = cnt.astype(I32)
    o_ref[1] = off.astype(I32)
    o_ref[2] = jnp.broadcast_to(thr, (N_EXPERTS, nbp))
    o_ref[3] = jnp.broadcast_to(need.astype(I32), (N_EXPERTS, nbp))
    o_ref[4] = eq_before.astype(I32)
    ident = (_iota((N_EXPERTS, LANES), 0) == _iota((N_EXPERTS, LANES), 1)).astype(F32)
    flip = lambda tab: _dot_tn(tab, ident, HI).astype(I32)
    thr_b = jnp.broadcast_to(thr, (N_EXPERTS, nbp))
    thr_hi = flip(jnp.right_shift(thr_b, 15).astype(F32))
    thr_lo = flip(jnp.bitwise_and(thr_b, 0x7FFF).astype(F32))
    t_ref[0] = jnp.left_shift(thr_hi, 15) | thr_lo
    t_ref[1] = flip(jnp.broadcast_to(need, (N_EXPERTS, nbp)))
    t_ref[2] = flip(eq_before)


def _route_thr_call(afft, cap):
    n_tok = afft.shape[1]
    nbp = -(-(n_tok // ROUTE_BLK) // LANES) * LANES
    return pl.pallas_call(
        functools.partial(_route_thr_body, cap, n_tok), name="route_threshold",
        grid=(1,),
        in_specs=[pl.BlockSpec((N_EXPERTS, n_tok), lambda i: (0, 0))],
        out_specs=[pl.BlockSpec((5, N_EXPERTS, nbp), lambda i: (0, 0, 0)),
                   pl.BlockSpec((3, nbp, LANES), lambda i: (0, 0, 0))],
        out_shape=[jax.ShapeDtypeStruct((5, N_EXPERTS, nbp), I32),
                   jax.ShapeDtypeStruct((3, nbp, LANES), I32)],
        compiler_params=_cp(("arbitrary",)),
    )(afft)


def _route_select(aff_ref, tab_ref, tri_ref):
    j = pl.program_id(0)
    bits = lax.bitcast_convert_type(aff_ref[...], I32)
    thr = tab_ref[0, 0:1, :]
    need = tab_ref[1, 0:1, :].astype(F32)
    eq_before = tab_ref[2, pl.ds(j, 1), :].astype(F32)
    gt = bits > thr
    eq = bits == thr
    tri = tri_ref[...]
    eq_rank = _dot(tri, eq.astype(BF16)) + eq_before
    sel = gt | (eq & (eq_rank < need))
    rank = _dot(tri, sel.astype(BF16))
    return jnp.where(sel, rank, -1.0)


def _onehot_pair(sel_rank, e, wi):
    slot = (_iota((1, 2 * ROUTE_WIN), 1) % ROUTE_WIN + wi * ROUTE_WIN).astype(F32)
    return jnp.where(sel_rank[:, e:e + 1] == slot, 1.0, 0.0).astype(BF16)


def _gather_body(nblk, cnt_s, off_s, afft_ref, tab_ref, tri_ref, h2_ref, xg_in, xg_hbm, oh_ref, stage, sem):
    del xg_in
    j = pl.program_id(0)
    w = ROUTE_WIN
    aff = afft_ref[...]
    bits = lax.bitcast_convert_type(aff, I32)
    blk_lane = _iota((1, tab_ref.shape[2]), 1)
    thr = tab_ref[2, :, 0:1]
    need = tab_ref[3, :, 0:1].astype(F32)
    eq_before = jnp.sum(jnp.where(blk_lane == j, tab_ref[4], 0), axis=1, keepdims=True).astype(F32)
    gt = bits > thr
    eq = bits == thr
    tri = tri_ref[...]
    eq_rank = _dot_nt(eq.astype(BF16), tri) + eq_before
    sel = gt | (eq & (eq_rank < need))
    rank = _dot_nt(sel.astype(BF16), tri)
    sel_rank = jnp.where(sel, rank, -1.0)
    h2 = h2_ref[...]

    def onehot(e, wi):
        slot = (_iota((w, 1), 0) + wi * w).astype(F32)
        return sel_rank[e:e + 1] == slot

    def gate_lanes(e, oh):
        gate = jnp.sum(jnp.where(oh, aff[e:e + 1], 0.0), axis=1, keepdims=True)
        return jnp.broadcast_to(gate, (w, LANES))

    def group_copy(e, off, wi, r):
        return pltpu.make_async_copy(stage.at[e, pl.ds(r * 8, 8)],
                                     xg_hbm.at[e, pl.ds(pl.multiple_of(off + wi * w + r * 8, 8), 8)], sem.at[e])

    def groups(cnt, wi):
        return (jnp.minimum(cnt - wi * w, w) + 7) // 8

    def start_groups(e, cnt, off, wi):
        lax.fori_loop(0, groups(cnt, wi), lambda r, c: (group_copy(e, off, wi, r).start(), c)[1], 0)

    def wait_groups(e, cnt, off, wi):
        lax.fori_loop(0, groups(cnt, wi), lambda r, c: (group_copy(e, off, wi, r).wait(), c)[1], 0)

    for e in range(N_EXPERTS):
        oh = onehot(e, 0)
        oh_ref[e * w:(e + 1) * w, :] = jnp.where(oh, 1.0, 0.0).astype(BF16)
        stage[e, :, D:] = gate_lanes(e, oh)
    stage[:, :, 0:D] = _dot(oh_ref[...], h2).reshape(N_EXPERTS, w, D)
    for e in range(N_EXPERTS):
        start_groups(e, cnt_s[e * nblk + j], off_s[e * nblk + j], 0)

    for e in range(N_EXPERTS):
        cnt = cnt_s[e * nblk + j]
        off = off_s[e * nblk + j]
        n_win = (cnt + (w - 1)) // w

        def window(wi, carry, e=e, cnt=cnt, off=off):
            wait_groups(e, cnt, off, wi - 1)
            oh = onehot(e, wi)
            stage[e, :, 0:D] = _dot(jnp.where(oh, 1.0, 0.0).astype(BF16), h2)
            stage[e, :, D:] = gate_lanes(e, oh)
            start_groups(e, cnt, off, wi)
            return carry

        lax.fori_loop(1, n_win, window, 0)

    for e in range(N_EXPERTS):
        cnt = cnt_s[e * nblk + j]
        off = off_s[e * nblk + j]
        wait_groups(e, cnt, off, jnp.maximum((cnt + (w - 1)) // w - 1, 0))


def _gather_call(cnt, off, afft, tab_e, tri, h2_flat, slots_alloc):
    n_tok = h2_flat.shape[0]
    nblk = n_tok // ROUTE_BLK
    xg0 = jnp.zeros((N_EXPERTS, slots_alloc, D + LANES), F32)
    grid_spec = pltpu.PrefetchScalarGridSpec(
        num_scalar_prefetch=2,
        grid=(nblk,),
        in_specs=[pl.BlockSpec((N_EXPERTS, ROUTE_BLK), lambda j, *_: (0, j)),
                  pl.BlockSpec(tab_e.shape, lambda j, *_: (0, 0, 0)),
                  pl.BlockSpec((ROUTE_BLK, ROUTE_BLK), lambda j, *_: (0, 0)),
                  pl.BlockSpec((ROUTE_BLK, D), lambda j, *_: (j, 0)),
                  pl.BlockSpec(memory_space=pl.ANY)],
        out_specs=pl.BlockSpec(memory_space=pl.ANY),
        scratch_shapes=[pltpu.VMEM((N_EXPERTS * ROUTE_WIN, ROUTE_BLK), BF16),
                        pltpu.VMEM((N_EXPERTS, ROUTE_WIN, D + LANES), F32),
                        pltpu.SemaphoreType.DMA((N_EXPERTS,))],
    )
    return pl.pallas_call(
        functools.partial(_gather_body, nblk), name="expert_gather",
        grid_spec=grid_spec,
        out_shape=jax.ShapeDtypeStruct((N_EXPERTS, slots_alloc, D + LANES), F32),
        input_output_aliases={6: 0},
        compiler_params=_cp(("arbitrary",)),
    )(cnt, off, afft, tab_e, tri, h2_flat, xg0)


def _ffn_body(used_s, x_ref, wg_ref, wu_ref, wd_ref, y_ref):
    e = pl.program_id(0)
    i = pl.program_id(1)

    @pl.when(i * SLOT_TILE < used_s[e])
    def _():
        xg = x_ref[0, :, 0:D].astype(BF16)
        gate = x_ref[0, :, D:D + 1]
        hid = _silu(_dot(xg, wg_ref[0])) * _dot(xg, wu_ref[0])
        y_ref[0] = _dot(hid.astype(BF16), wd_ref[0]) * gate

    @pl.when(i * SLOT_TILE >= used_s[e])
    def _():
        y_ref[...] = jnp.zeros_like(y_ref)


def _ffn_call(used, xg, wg, wu, wd):
    _, slots_alloc, _ = xg.shape
    wspec = lambda: pl.BlockSpec((1, D, D), lambda e, i, used_s: (e, 0, 0))
    last_tile = lambda e, i, used_s: jnp.minimum(i, jnp.maximum(used_s[e] - 1, 0) // SLOT_TILE)
    grid_spec = pltpu.PrefetchScalarGridSpec(
        num_scalar_prefetch=1,
        grid=(N_EXPERTS, slots_alloc // SLOT_TILE),
        in_specs=[pl.BlockSpec((1, SLOT_TILE, D + LANES), lambda e, i, used_s: (e, last_tile(e, i, used_s), 0)),
                  wspec(), wspec(), wspec()],
        out_specs=pl.BlockSpec((1, SLOT_TILE, D), lambda e, i, used_s: (e, i, 0)),
    )
    return pl.pallas_call(
        _ffn_body, name="expert_ffn",
        grid_spec=grid_spec,
        out_shape=jax.ShapeDtypeStruct((N_EXPERTS, slots_alloc, D), F32),
        compiler_params=_cp(("arbitrary", "arbitrary")),
    )(used, xg, wg, wu, wd)


def _combine_body(final, nblk, cnt_s, off_s, aff_ref, tab_ref, tri_ref, x1_ref, g2_ref, fg_ref, y_hbm,
                  o_ref, ybuf, ysplit, oh_ref, acc_ref, sems):
    j = pl.program_id(0)
    sel_rank = _route_select(aff_ref, tab_ref, tri_ref)
    w = ROUTE_WIN

    def win_copy(e, off, wi):
        return pltpu.make_async_copy(y_hbm.at[e, pl.ds(pl.multiple_of(off + wi * w, 8), w)], ybuf.at[e], sems.at[e])

    split = _split_bf16

    for e in range(N_EXPERTS):
        win_copy(e, off_s[e * nblk + j], 0).start()
    for e in range(N_EXPERTS):
        oh_ref[:, e * 2 * w:(e + 1) * 2 * w] = _onehot_pair(sel_rank, e, 0)
    for e in range(N_EXPERTS):
        win_copy(e, off_s[e * nblk + j], 0).wait()
        y_hi, y_lo = split(ybuf[e])
        ysplit[(2 * e) * w:(2 * e + 1) * w, :] = y_hi
        ysplit[(2 * e + 1) * w:(2 * e + 2) * w, :] = y_lo
    acc_ref[...] = _dot(oh_ref[...], ysplit[...])
    for e in range(N_EXPERTS):
        cnt = cnt_s[e * nblk + j]
        off = off_s[e * nblk + j]

        def window(wi, carry, e=e, off=off):
            cp = win_copy(e, off, wi)
            cp.start()
            cp.wait()
            y_hi, y_lo = split(ybuf[e])
            acc_ref[...] += _dot(_onehot_pair(sel_rank, e, wi), jnp.concatenate([y_hi, y_lo], axis=0))
            return carry

        lax.fori_loop(1, (cnt + (w - 1)) // w, window, 0)
    x2 = x1_ref[...] + g2_ref[0] * acc_ref[...]
    if final:
        ms = jnp.mean(x2 * x2, axis=-1, keepdims=True)
        x2 = x2 * lax.rsqrt(ms + EPS) * fg_ref[...]
    o_ref[...] = x2


def _combine_call(cnt, off, aff2d, tab_t, tri, x1_flat, gate2, final_g, y, blocks_per_batch, final):
    n_tok = x1_flat.shape[0]
    nblk = n_tok // ROUTE_BLK
    grid_spec = pltpu.PrefetchScalarGridSpec(
        num_scalar_prefetch=2,
        grid=(nblk,),
        in_specs=[pl.BlockSpec((ROUTE_BLK, LANES), lambda j, *_: (j, 0)),
                  pl.BlockSpec(tab_t.shape, lambda j, *_: (0, 0, 0)),
                  pl.BlockSpec((ROUTE_BLK, ROUTE_BLK), lambda j, *_: (0, 0)),
                  pl.BlockSpec((ROUTE_BLK, D), lambda j, *_: (j, 0)),
                  pl.BlockSpec((1, 1, D), lambda j, *_: (j // blocks_per_batch, 0, 0)),
                  pl.BlockSpec((1, D), lambda j, *_: (0, 0)),
                  pl.BlockSpec(memory_space=pl.ANY)],
        out_specs=pl.BlockSpec((ROUTE_BLK, D), lambda j, *_: (j, 0)),
        scratch_shapes=[pltpu.VMEM((N_EXPERTS, ROUTE_WIN, D), F32),
                        pltpu.VMEM((2 * N_EXPERTS * ROUTE_WIN, D), BF16),
                        pltpu.VMEM((ROUTE_BLK, 2 * N_EXPERTS * ROUTE_WIN), BF16),
                        pltpu.VMEM((ROUTE_BLK, D), F32),
                        pltpu.SemaphoreType.DMA((N_EXPERTS,))],
    )
    return pl.pallas_call(
        functools.partial(_combine_body, final, nblk), name="expert_combine",
        grid_spec=grid_spec,
        out_shape=jax.ShapeDtypeStruct((n_tok, D), F32),
        compiler_params=_cp(("arbitrary",)),
    )(cnt, off, aff2d, tab_t, tri, x1_flat, gate2, final_g.reshape(1, D), y)


def _rope_tables(l):
    quarter = HD // 4
    inv = ROPE_THETA ** (-jnp.arange(quarter, dtype=F32) / quarter)
    t = jnp.arange(l)
    pos = jnp.stack([(t // GRID_W).astype(F32), (t % GRID_W).astype(F32)], axis=1)
    lane = np.arange(KV_W)
    which = (lane % HD) // (HD // 2)
    ang = pos[:, which] * inv[lane % quarter][None, :]
    sign = np.where((lane % (HD // 2)) < quarter, -1.0, 1.0).astype(np.float32)
    return jnp.cos(ang), jnp.sin(ang) * sign[None, :]


def _split_bf16_stack(w):
    hi = w.astype(BF16)
    return jnp.stack([hi, (w - hi.astype(F32)).astype(BF16)])


def _prep_layer(l, p):
    w_in = p['w_in'][l]
    n_main = HG_COLS + AT_COLS + SSD_W + SSD_XBC
    w_main = jnp.concatenate([w_in[:, :n_main], jnp.pad(w_in[:, n_main:], ((0, 0), (0, 120)))], axis=1).astype(BF16)
    w_dtt = w_in[:, n_main:].T.astype(BF16)
    lb_all = jnp.cumsum(jax.nn.softmax(p['hg_lb'].astype(F32), axis=0), axis=0)
    lb = (lb_all - lb_all[:1])[l]
    lb_rows = jnp.stack([jnp.log(lb), jnp.log1p(-lb), 1.0 - lb], axis=1)
    a_coef = -jnp.exp(p['a_log'][l].astype(F32)).reshape(-1)
    dt_bias = p['dt_bias'][l].astype(F32).reshape(-1)
    par_rows = jnp.zeros((8, SSD_W), F32).at[0, :8].set(dt_bias).at[1, :8].set(a_coef)
    par_rows = par_rows.at[2, :].set(jnp.repeat(p['d_skip'][l].astype(F32), HD))
    par_cols = jnp.zeros((8, LANES), F32).at[:, 0].set(dt_bias).at[:, 1].set(a_coef)
    w_out = p['w_out'][l].astype(BF16)
    lane_k = np.arange(KV_W)
    ek = np.zeros((KV_W, AT_W), np.float32)
    for rep in range(AT_GROUP):
        ek[lane_k, (lane_k // HD) * (AT_GROUP * HD) + rep * HD + lane_k % HD] = 1.0
    evt = np.zeros((2 * LANES, KV_W), np.float32)
    evt[(lane_k // HD) * LANES + lane_k % HD, lane_k] = 1.0
    place = np.zeros((AT_GROUP, HD, AT_GROUP * HD), np.float32)
    for g in range(AT_GROUP):
        place[g, np.arange(HD), g * HD + np.arange(HD)] = 1.0
    return dict(
        w_main=w_main, w_dtt=w_dtt, lb_rows=lb_rows,
        hg_norm=jnp.tile(p['hg_norm_g'][l], 4).reshape(1, HG_W),
        gq=jnp.tile(p['q_norm_g'][l], 8).reshape(1, AT_W), gk=jnp.tile(p['k_norm_g'][l], 2).reshape(1, KV_W),
        ek=jnp.asarray(ek, BF16), evt=jnp.asarray(evt, BF16), place=jnp.asarray(place, BF16),
        eye=jnp.asarray(np.eye(AT_W, dtype=np.float32), BF16),
        conv_w=jnp.pad(p['conv_w'][l], ((0, 8 - SSD_CONV), (0, 0))), conv_b=p['conv_b'][l].reshape(1, SSD_XBC),
        par_rows=par_rows, par_cols=par_cols,
        ssd_norm=jnp.tile(p['ssd_norm_g'][l], 4).reshape(1, SSD_W),
        w_o1=w_out[:HG_W], w_o2=w_out[HG_W:HG_W + AT_W], w_o3=w_out[HG_W + AT_W:],
        w_r=_split_bf16_stack(jnp.pad(p['w_router'][l], ((0, 0), (0, LANES - N_EXPERTS)))),
        wg=p['w_gate'][l].astype(BF16), wu=p['w_up'][l].astype(BF16), wd=p['w_down'][l].astype(BF16),
    )


def _trunk(x, mod, layers, p):
    b, l, _ = x.shape
    n_tok = b * l
    cap = EC_CAPACITY * n_tok // N_EXPERTS
    nblk = n_tok // ROUTE_BLK
    slots = cap + 8 * nblk
    n_tiles = -(-slots // SLOT_TILE)
    slots_alloc = -(-(n_tiles * SLOT_TILE + ROUTE_WIN) // SLOT_TILE) * SLOT_TILE
    cos2, sin2 = _rope_tables(l)
    tri = (np.arange(ROUTE_BLK)[None, :] < np.arange(ROUTE_BLK)[:, None]).astype(np.float32)
    tri = jnp.asarray(tri, BF16)
    for li in range(DEPTH):
        w = layers[li]
        sh1, sc1, g1, sh2, sc2, g2 = [mod[li, :, i * D:(i + 1) * D].reshape(b, 1, D) for i in range(6)]
        hg_in, at_in, ssd_in, dtt = _inproj_call(x, p['norm1_g'][li], sc1, sh1, w['w_main'], w['w_dtt'])
        o_f = _hgrn_call(hg_in, w['lb_rows'][0], w['hg_norm'], None)
        o_hg = _hgrn_call(hg_in, w['lb_rows'][1], w['hg_norm'], o_f)
        qt, kt, vt = _attn_prep_call(at_in, cos2, sin2, w['gq'], w['gk'], w['ek'], w['evt'], w['eye'])
        o_at = _attn_call(qt, kt, vt, w['place'])
        xbc = _ssd_conv_call(ssd_in, w['conv_w'], w['conv_b'])
        y_f = _ssd_call(xbc, ssd_in, dtt, w['par_rows'], w['par_cols'], w['ssd_norm'], None)
        o_ssd = _ssd_call(xbc, ssd_in, dtt, w['par_rows'], w['par_cols'], w['ssd_norm'], y_f)
        x1, h2, aff, afft = _outproj_call(x, o_hg, o_at, o_ssd, w['w_o1'], w['w_o2'], w['w_o3'], g1,
                                          p['norm2_g'][li], sc2, sh2, w['w_r'])
        tab_e, tab_t = _route_thr_call(afft, cap)
        cnt = tab_e[0, :, :nblk].reshape(-1)
        off = tab_e[1, :, :nblk].reshape(-1)
        aff2d = aff.reshape(n_tok, LANES)
        xg = _gather_call(cnt, off, afft, tab_e, tri, h2.reshape(n_tok, D), slots_alloc)
        used = tab_e[1, :, nblk - 1] + (tab_e[0, :, nblk - 1] + 7) // 8 * 8
        y = _ffn_call(used, xg, w['wg'], w['wu'], w['wd'])
        x = _combine_call(cnt, off, aff2d, tab_t, tri, x1.reshape(n_tok, D), g2, p['final_g'], y,
                          l // ROUTE_BLK, li == DEPTH - 1).reshape(b, l, D)
    return x


def kernel(x_prompt, x_sample, c_prompt, c_sample, norm1_g, norm2_g, w_mod, b_mod, w_in, hg_lb, hg_norm_g,
           q_norm_g, k_norm_g, conv_w, conv_b, a_log, dt_bias, d_skip, ssd_norm_g, w_out, w_router,
           w_gate, w_up, w_down, final_g):
    p = dict(norm1_g=norm1_g, norm2_g=norm2_g, w_in=w_in, hg_lb=hg_lb, hg_norm_g=hg_norm_g, q_norm_g=q_norm_g,
             k_norm_g=k_norm_g, conv_w=conv_w, conv_b=conv_b, a_log=a_log, dt_bias=dt_bias, d_skip=d_skip,
             ssd_norm_g=ssd_norm_g, w_out=w_out, w_router=w_router, w_gate=w_gate, w_up=w_up, w_down=w_down,
             final_g=final_g)
    bp, bs = c_prompt.shape[0], c_sample.shape[0]
    rows = -(-(bp + bs) // 8) * 8
    c_all = jnp.pad(jnp.concatenate([c_prompt, c_sample], axis=0), ((0, rows - bp - bs), (0, 0)))
    mod = _mod_call(c_all, w_mod, b_mod)
    layers = [_prep_layer(li, p) for li in range(DEPTH)]
    y_prompt = _trunk(x_prompt, mod[:, :bp], layers, p)
    y_sample = _trunk(x_sample, mod[:, bp:bp + bs], layers, p)
    return (y_prompt, y_sample)
```

```python
import functools

import numpy as np
import jax
import jax.numpy as jnp
from jax import lax
from jax.experimental import pallas as pl
from jax.experimental.pallas import tpu as pltpu

F32 = jnp.float32
BF16 = jnp.bfloat16
I32 = jnp.int32
HI = lax.Precision.HIGHEST

D = 1024
DEPTH = 2
GRID_W = 64
EPS = 1e-6
HD = 64
HG_W = 256
HG_CHUNK = 64
HG_SUB = 16
AT_W = 512
KV_W = 128
AT_GROUP = 4
ROPE_THETA = 10000.0
QK_SCALE_LOG2 = (HD ** -0.5) * 1.4426950408889634
SSD_W = 256
SSD_XBC = 512
SSD_CHUNK = 128
SSD_CONV = 5
N_EXPERTS = 16
EC_CAPACITY = 2
HG_COLS = 5 * HG_W
AT_COLS = AT_W + 2 * KV_W
SSD_COLS = SSD_W + SSD_XBC + 128
IN_PAD = HG_COLS + AT_COLS + SSD_COLS
LANES = 128
ATTN_KEY_BLK = 1024
SLOT_TILE = 256
ROUTE_BLK = 256
ROUTE_WIN = 64
VMEM_LIMIT = 56 * 1024 * 1024


def _cp(sem, vmem=VMEM_LIMIT):
    return pltpu.CompilerParams(dimension_semantics=sem, vmem_limit_bytes=vmem)


def _dot(a, b, prec=None):
    return jnp.dot(a, b, preferred_element_type=F32, precision=prec)


def _dot_nt(a, b, prec=None):
    return lax.dot_general(a, b, (((1,), (1,)), ((), ())), preferred_element_type=F32, precision=prec)


def _dot_tn(a, b, prec=None):
    return lax.dot_general(a, b, (((0,), (0,)), ((), ())), preferred_element_type=F32, precision=prec)


def _sigmoid(x):
    return 1.0 / (1.0 + jnp.exp(-x))


def _silu(x):
    return x * _sigmoid(x)


def _softplus(x):
    return jnp.maximum(x, 0.0) + jnp.log1p(jnp.exp(-jnp.abs(x)))


def _iota(shape, dim):
    return lax.broadcasted_iota(I32, shape, dim)


def _split_bf16(x):
    hi = x.astype(BF16)
    return hi, (x - hi.astype(F32)).astype(BF16)


def _head_mean_sq(x, width):
    bd = jnp.where(_iota((width, width), 0) // HD == _iota((width, width), 1) // HD, 1.0 / HD, 0.0).astype(BF16)
    hi, lo = _split_bf16(x * x)
    return _dot(hi, bd) + _dot(lo, bd)


def _mod_body(c_ref, w_ref, b_ref, o_ref):
    o_ref[0] = _dot(_silu(c_ref[...]), w_ref[0], HI) + b_ref[0]


def _mod_call(c_all, w_mod, b_mod):
    bp = c_all.shape[0]
    return pl.pallas_call(
        _mod_body, name="adaln_mod",
        grid=(DEPTH, 6),
        in_specs=[pl.BlockSpec((bp, D), lambda l, j: (0, 0)),
                  pl.BlockSpec((1, D, D), lambda l, j: (l, 0, j)),
                  pl.BlockSpec((1, 1, D), lambda l, j: (l, 0, j))],
        out_specs=pl.BlockSpec((1, bp, D), lambda l, j: (l, 0, j)),
        out_shape=jax.ShapeDtypeStruct((DEPTH, bp, 6 * D), F32),
        compiler_params=_cp(("arbitrary", "arbitrary")),
    )(c_all, w_mod, b_mod.reshape(DEPTH, 1, 6 * D))


def _inproj_body(x_ref, g_ref, sc_ref, sh_ref, w_ref, wdt_ref, hg_ref, at_ref, ssd_ref, dtt_ref):
    x = x_ref[0]
    ms = jnp.mean(x * x, axis=-1, keepdims=True)
    y = x * lax.rsqrt(ms + EPS) * g_ref[...]
    h = (y * (1.0 + sc_ref[0]) + sh_ref[0]).astype(BF16)
    p = _dot(h, w_ref[...])
    hg_ref[0] = p[:, :HG_COLS]
    at_ref[0] = p[:, HG_COLS:HG_COLS + AT_COLS]
    ssd_ref[0] = p[:, HG_COLS + AT_COLS:]
    dtt_ref[0] = _dot_nt(wdt_ref[...], h)


def _inproj_call(x, norm_g, scale, shift, w_main, w_dtt):
    b, l, _ = x.shape
    tm = min(l, 512)
    row = lambda: pl.BlockSpec((1, tm, D), lambda bi, i: (bi, i, 0))
    per_b = lambda: pl.BlockSpec((1, 1, D), lambda bi, i: (bi, 0, 0))
    in_specs = [row(), pl.BlockSpec((1, D), lambda bi, i: (0, 0)), per_b(), per_b(),
                pl.BlockSpec((D, IN_PAD), lambda bi, i: (0, 0)),
                pl.BlockSpec((8, D), lambda bi, i: (0, 0))]
    args = [x, norm_g.reshape(1, D), scale, shift, w_main, w_dtt]
    out_specs = [pl.BlockSpec((1, tm, HG_COLS), lambda bi, i: (bi, i, 0)),
                 pl.BlockSpec((1, tm, AT_COLS), lambda bi, i: (bi, i, 0)),
                 pl.BlockSpec((1, tm, SSD_COLS), lambda bi, i: (bi, i, 0)),
                 pl.BlockSpec((1, 8, tm), lambda bi, i: (bi, 0, i))]
    out_shape = [jax.ShapeDtypeStruct((b, l, HG_COLS), F32),
                 jax.ShapeDtypeStruct((b, l, AT_COLS), F32),
                 jax.ShapeDtypeStruct((b, l, SSD_COLS), F32),
                 jax.ShapeDtypeStruct((b, 8, l), F32)]
    return pl.pallas_call(
        _inproj_body, name="norm_inproj",
        grid=(b, l // tm), in_specs=in_specs, out_specs=out_specs, out_shape=out_shape,
        compiler_params=_cp(("arbitrary", "arbitrary")),
    )(*args)


def _hgrn_body(bwd, nch, *refs):
    if bwd:
        x_ref, lb_ref, ng_ref, of_ref, o_ref, st_ref = refs
    else:
        x_ref, lb_ref, o_ref, st_ref = refs

    @pl.when(pl.program_id(1) == 0)
    def _():
        st_ref[...] = jnp.zeros_like(st_ref)

    q_chunk = HG_CHUNK
    lane_head = _iota((1, HG_W), 1) // HD
    tril = (_iota((q_chunk, q_chunk), 1) <= _iota((q_chunk, q_chunk), 0)).astype(F32)
    bd_mask = _iota((HG_W, HG_W), 0) // HD == _iota((HG_W, HG_W), 1) // HD
    row_q = _iota((q_chunk, 1), 0)
    att_t = _iota((q_chunk, q_chunk), 0) % HG_SUB
    att_s = _iota((q_chunk, q_chunk), 1)
    log_lb = lb_ref[0:1, :]
    log_1m_lb = lb_ref[1:2, :]
    one_m_lb = lb_ref[2:3, :]
    fcol = 3 * HG_W if bwd else 2 * HG_W

    def chunk(ci, carry):
        c = (nch - 1 - ci) if bwd else ci
        r0 = pl.multiple_of(c * q_chunk, q_chunk)
        rows = pl.ds(r0, q_chunk)
        q = _silu(x_ref[0, rows, 0:HG_W])
        v = x_ref[0, rows, HG_W:2 * HG_W]
        fr = x_ref[0, rows, fcol:fcol + HG_W]
        a2 = log_1m_lb - _softplus(-fr)
        mx = jnp.maximum(log_lb, a2)
        logf = mx + jnp.log1p(jnp.exp(-jnp.abs(log_lb - a2)))
        k = one_m_lb * _sigmoid(-fr)
        b_inc = _dot(tril, logf, HI)
        b_exc = b_inc - logf
        total = b_inc[q_chunk - 1:q_chunk, :]
        st = st_ref[...]
        if bwd:
            q_in = q * jnp.exp(total - b_exc)
            k_st = k * jnp.exp(b_exc)
        else:
            q_in = q * jnp.exp(b_inc)
            k_st = k * jnp.exp(total - b_inc)
        inter = _dot_nt(q_in.astype(BF16), st.astype(BF16))
        upd = _dot_tn(v.astype(BF16), k_st.astype(BF16))
        st_ref[...] = st * jnp.exp(total) + jnp.where(bd_mask, upd, 0.0)
        v16 = v.astype(BF16)
        pieces = []
        for i in range(q_chunk // HG_SUB):
            lo, hi = i * HG_SUB, (i + 1) * HG_SUB
            if bwd:
                ref = b_inc[hi - 1:hi, :]
                qt = q[lo:hi] * jnp.exp(ref - b_exc[lo:hi])
                kt = k * jnp.exp(jnp.where(row_q >= lo, b_exc - ref, 0.0))
                kt = jnp.where(row_q >= lo, kt, 0.0)
                amask = att_s >= att_t + lo
            else:
                ref = b_exc[lo:lo + 1, :]
                qt = q[lo:hi] * jnp.exp(b_inc[lo:hi] - ref)
                kt = k * jnp.exp(jnp.where(row_q < hi, ref - b_inc, 0.0))
                kt = jnp.where(row_q < hi, kt, 0.0)
                amask = att_s <= att_t + lo
            q4 = jnp.concatenate([jnp.where(lane_head == h, qt, 0.0) for h in range(4)], axis=0)
            att = _dot_nt(q4.astype(BF16), kt.astype(BF16))
            att = jnp.where(amask, att, 0.0)
            r = _dot(att.astype(BF16), v16)
            o_i = jnp.where(lane_head == 0, r[0:HG_SUB], 0.0)
            for h in range(1, 4):
                o_i = o_i + jnp.where(lane_head == h, r[h * HG_SUB:(h + 1) * HG_SUB], 0.0)
            pieces.append(o_i + inter[lo:hi])
        o = jnp.concatenate(pieces, axis=0)
        if bwd:
            tot = of_ref[0, rows, :] + o
            ms = _head_mean_sq(tot, HG_W)
            g = x_ref[0, rows, 4 * HG_W:5 * HG_W]
            y = tot * lax.rsqrt(ms + EPS) * ng_ref[...] * _silu(g)
            o_ref[0, rows, :] = y.astype(o_ref.dtype)
        else:
            o_ref[0, rows, :] = o
        return carry

    lax.fori_loop(0, nch, chunk, 0, unroll=min(nch, 8))


def _hgrn_call(hg_in, lb_rows, norm_g, o_fwd):
    b, l, _ = hg_in.shape
    bwd = o_fwd is not None
    t = min(l, 512)
    nb = l // t
    idx = (lambda bi, j: (bi, nb - 1 - j, 0)) if bwd else (lambda bi, j: (bi, j, 0))
    in_specs = [pl.BlockSpec((1, t, HG_COLS), idx), pl.BlockSpec((3, HG_W), lambda bi, j: (0, 0))]
    args = [hg_in, lb_rows]
    if bwd:
        in_specs += [pl.BlockSpec((1, HG_W), lambda bi, j: (0, 0)), pl.BlockSpec((1, t, HG_W), idx)]
        args += [norm_g, o_fwd]
    return pl.pallas_call(
        functools.partial(_hgrn_body, bwd, t // HG_CHUNK), name="hgrn_bwd" if bwd else "hgrn_fwd",
        grid=(b, nb), in_specs=in_specs,
        out_specs=pl.BlockSpec((1, t, HG_W), idx),
        out_shape=jax.ShapeDtypeStruct((b, l, HG_W), BF16 if bwd else F32),
        scratch_shapes=[pltpu.VMEM((HG_W, HG_W), F32)],
        compiler_params=_cp(("arbitrary", "arbitrary")),
    )(*args)


def _rope_swap(x):
    w = x.shape[-1]
    first = (_iota((1, w), 1) % 32) < 16
    return jnp.where(first, pltpu.roll(x, w - 16, 1), pltpu.roll(x, 16, 1))


def _attn_prep_body(x_ref, cos_ref, sin_ref, gq_ref, gk_ref, ek_ref, evt_ref, eye_ref, qt_ref, k_ref, vt_ref):
    x = x_ref[0]
    cos2 = cos_ref[...]
    sin2 = sin_ref[...]
    xq = x[:, :AT_W]
    qn = xq * lax.rsqrt(_head_mean_sq(xq, AT_W) + EPS) * gq_ref[...]
    cos_q = jnp.concatenate([cos2] * 4, axis=1)
    sin_q = jnp.concatenate([sin2] * 4, axis=1)
    q = ((qn * cos_q + _rope_swap(qn) * sin_q) * QK_SCALE_LOG2).astype(BF16)
    qt_ref[0] = _dot_nt(eye_ref[...], q).astype(BF16)
    xk = x[:, AT_W:AT_W + KV_W]
    kn = xk * lax.rsqrt(_head_mean_sq(xk, KV_W) + EPS) * gk_ref[...]
    kr = (kn * cos2 + _rope_swap(kn) * sin2).astype(BF16)
    k_ref[0, 0] = _dot(kr, ek_ref[...]).astype(BF16)
    xv = x[:, AT_W + KV_W:].astype(BF16)
    ones_row = ((_iota((2 * LANES, 1), 0) % LANES) >= HD).astype(F32)
    vt_ref[0, 0] = (_dot_nt(evt_ref[...], xv) + ones_row).astype(BF16)


def _attn_prep_call(at_in, cos2, sin2, gq, gk, ek, evt, eye):
    b, l, _ = at_in.shape
    tm = min(l, ATTN_KEY_BLK)
    const = lambda shape: pl.BlockSpec(shape, lambda bi, i: (0, 0))
    return pl.pallas_call(
        _attn_prep_body, name="attn_prep",
        grid=(b, l // tm),
        in_specs=[pl.BlockSpec((1, tm, AT_COLS), lambda bi, i: (bi, i, 0)),
                  pl.BlockSpec((tm, KV_W), lambda bi, i: (i, 0)),
                  pl.BlockSpec((tm, KV_W), lambda bi, i: (i, 0)),
                  const((1, AT_W)), const((1, KV_W)), const((KV_W, AT_W)), const((2 * LANES, KV_W)),
                  const((AT_W, AT_W))],
        out_specs=[pl.BlockSpec((1, AT_W, tm), lambda bi, i: (bi, 0, i)),
                   pl.BlockSpec((1, 1, tm, AT_W), lambda bi, i: (bi, i, 0, 0)),
                   pl.BlockSpec((1, 1, 2 * LANES, tm), lambda bi, i: (bi, i, 0, 0))],
        out_shape=[jax.ShapeDtypeStruct((b, AT_W, l), BF16),
                   jax.ShapeDtypeStruct((b, l // tm, tm, AT_W), BF16),
                   jax.ShapeDtypeStruct((b, l // tm, 2 * LANES, tm), BF16)],
        compiler_params=_cp(("arbitrary", "arbitrary")),
    )(at_in, cos2, sin2, gq, gk, ek, evt, eye)


def _attn_body(tq, nk, qt_ref, k_ref, vt_ref, place_ref, o_ref, qm_ref, s_ref, m_ref, acc_ref):
    qm_ref[...] = jnp.zeros_like(qm_ref)
    for g in range(AT_GROUP):
        qm_ref[g * HD:(g + 1) * HD, g * tq:(g + 1) * tq] = qt_ref[0, g * HD:(g + 1) * HD, :]
    m_ref[...] = jnp.full_like(m_ref, -jnp.inf)
    acc_ref[...] = jnp.zeros_like(acc_ref)

    def scores(slot, kk):
        s_ref[slot] = _dot(k_ref[0, kk], qm_ref[...]).astype(BF16)

    def consume(slot, kk):
        s = s_ref[slot]
        m_old = m_ref[...]
        m_new = jnp.maximum(m_old, jnp.max(s, axis=0, keepdims=True).astype(F32))
        alpha = jnp.exp2(m_old - m_new)
        p = jnp.exp2(s - m_new.astype(BF16))
        acc_ref[...] = acc_ref[...] * alpha + _dot(vt_ref[0, kk], p)
        m_ref[...] = m_new

    scores(0, 0)
    if nk > 1:
        def pair(i, carry):
            kk = 2 * i
            scores(1, kk + 1)
            consume(0, kk)
            scores(0, kk + 2)
            consume(1, kk + 1)
            return carry

        lax.fori_loop(0, nk // 2 - 1, pair, 0)
        scores(1, nk - 1)
        consume(0, nk - 2)
        consume(1, nk - 1)
    else:
        consume(0, 0)
    acc = acc_ref[...]
    o_t = (acc[0:HD] / acc[HD:HD + 1]).astype(BF16)
    out = _dot_tn(o_t[:, 0:tq], place_ref[0])
    for g in range(1, AT_GROUP):
        out = out + _dot_tn(o_t[:, g * tq:(g + 1) * tq], place_ref[g])
    o_ref[0] = out.astype(o_ref.dtype)


def _attn_call(qt, kt, vt, place):
    b, _, l = qt.shape
    _, nk, tk, _ = kt.shape
    assert nk == 1 or nk % 2 == 0
    tq = min(l, 512)
    return pl.pallas_call(
        functools.partial(_attn_body, tq, nk), name="flash_attn",
        grid=(b, 2, l // tq),
        in_specs=[pl.BlockSpec((1, 4 * HD, tq), lambda bi, j, i: (bi, j, i)),
                  pl.BlockSpec((1, nk, tk, 4 * HD), lambda bi, j, i: (bi, 0, 0, j)),
                  pl.BlockSpec((1, nk, LANES, tk), lambda bi, j, i: (bi, 0, j, 0)),
                  pl.BlockSpec((AT_GROUP, HD, 4 * HD), lambda bi, j, i: (0, 0, 0))],
        out_specs=pl.BlockSpec((1, tq, 4 * HD), lambda bi, j, i: (bi, i, j)),
        out_shape=jax.ShapeDtypeStruct((b, l, AT_W), BF16),
        scratch_shapes=[pltpu.VMEM((4 * HD, AT_GROUP * tq), BF16),
                        pltpu.VMEM((2, tk, AT_GROUP * tq), BF16),
                        pltpu.VMEM((1, AT_GROUP * tq), F32),
                        pltpu.VMEM((LANES, AT_GROUP * tq), F32)],
        compiler_params=_cp(("arbitrary",) * 3),
    )(qt, kt, vt, place)


def _ssd_conv_body(t, x_ref, prev_ref, next_ref, w_ref, b_ref, o_ref, ext_ref):
    j = pl.program_id(1)
    nb = pl.num_programs(1)
    lo, hi = SSD_W, SSD_W + SSD_XBC
    ext_ref[0:8, :] = jnp.where(j > 0, prev_ref[0, :, lo:hi], 0.0)
    ext_ref[8:8 + t, :] = x_ref[0, :, lo:hi]
    ext_ref[8 + t:16 + t, :] = jnp.where(j < nb - 1, next_ref[0, :, lo:hi], 0.0)
    pad = SSD_CONV // 2
    acc = b_ref[...] + w_ref[0:1, :] * ext_ref[pl.ds(8 - pad, t), :]
    for kk in range(1, SSD_CONV):
        acc = acc + w_ref[kk:kk + 1, :] * ext_ref[pl.ds(8 - pad + kk, t), :]
    o_ref[0] = _silu(acc)


def _ssd_conv_call(ssd_in, conv_w, conv_b):
    b, l, _ = ssd_in.shape
    t = min(l, 512)
    t8 = t // 8
    last8 = l // 8 - 1
    return pl.pallas_call(
        functools.partial(_ssd_conv_body, t), name="ssd_conv",
        grid=(b, l // t),
        in_specs=[pl.BlockSpec((1, t, SSD_COLS), lambda bi, j: (bi, j, 0)),
                  pl.BlockSpec((1, 8, SSD_COLS), lambda bi, j: (bi, jnp.maximum(j * t8 - 1, 0), 0)),
                  pl.BlockSpec((1, 8, SSD_COLS), lambda bi, j: (bi, jnp.minimum((j + 1) * t8, last8), 0)),
                  pl.BlockSpec((8, SSD_XBC), lambda bi, j: (0, 0)),
                  pl.BlockSpec((1, SSD_XBC), lambda bi, j: (0, 0))],
        out_specs=pl.BlockSpec((1, t, SSD_XBC), lambda bi, j: (bi, j, 0)),
        out_shape=jax.ShapeDtypeStruct((b, l, SSD_XBC), F32),
        scratch_shapes=[pltpu.VMEM((t + 16, SSD_XBC), F32)],
        compiler_params=_cp(("arbitrary", "arbitrary")),
    )(ssd_in, ssd_in, ssd_in, conv_w, conv_b)


def _ssd_body(bwd, nch, *refs):
    if bwd:
        xbc_ref, dtc_ref, dtr_ref, par_ref, parc_ref, z_ref, yf_ref, ng_ref, o_ref, st_ref = refs
    else:
        xbc_ref, dtc_ref, dtr_ref, par_ref, parc_ref, o_ref, st_ref = refs

    @pl.when(pl.program_id(1) == 0)
    def _():
        st_ref[...] = jnp.zeros_like(st_ref)

    qc = SSD_CHUNK
    dsel = 4 if bwd else 0
    lane_head = _iota((1, SSD_W), 1) // HD
    lane_grp = _iota((1, LANES), 1) // HD
    tril = (_iota((qc, qc), 1) <= _iota((qc, qc), 0)).astype(F32)
    triu = (_iota((qc, qc), 0) <= _iota((qc, qc), 1)).astype(F32)
    tt = _iota((qc, qc), 0)
    ss = _iota((qc, qc), 1)
    st_mask = (_iota((SSD_W, LANES), 0) // (2 * HD)) == (_iota((SSD_W, LANES), 1) // HD)
    bias_row = par_ref[0:1, 0:LANES]
    acoef_row = par_ref[1:2, 0:LANES]
    dskip_row = par_ref[2:3, :]
    bias_col = parc_ref[:, 0:1]
    acoef_col = parc_ref[:, 1:2]

    def expand(col_vals):
        out = jnp.broadcast_to(col_vals[:, dsel:dsel + 1], (qc, SSD_W))
        for h in range(1, 4):
            out = jnp.where(lane_head == h, jnp.broadcast_to(col_vals[:, dsel + h:dsel + h + 1], (qc, SSD_W)), out)
        return out

    def chunk(ci, carry):
        c = (nch - 1 - ci) if bwd else ci
        r0 = pl.multiple_of(c * qc, qc)
        rows = pl.ds(r0, qc)
        xs = xbc_ref[0, rows, 0:SSD_W]
        bm = xbc_ref[0, rows, SSD_W:SSD_W + LANES]
        cm = xbc_ref[0, rows, SSD_W + LANES:SSD_W + 2 * LANES]
        dt_c = _softplus(dtc_ref[0, rows, :] + bias_row)
        a_c = dt_c * acoef_row
        dt_r = _softplus(dtr_ref[0, :, rows] + bias_col)
        a_r = dt_r * acoef_col
        inc_c = _dot(tril, a_c, HI)
        inc_r = _dot(a_r, triu, HI)
        if bwd:
            cum_c, cum_r = inc_c - a_c, inc_r - a_r
        else:
            cum_c, cum_r = inc_c, inc_r
        total_c = inc_c[qc - 1:qc, :]
        xdt = xs * expand(dt_c)
        xdt16 = xdt.astype(BF16)
        bm16 = bm.astype(BF16)
        cm16 = cm.astype(BF16)
        gmat = [_dot_nt(jnp.where(lane_grp == g, cm, 0.0).astype(BF16), bm16) for g in range(2)]
        y = jnp.zeros((qc, SSD_W), F32)
        for h in range(4):
            col = cum_c[:, dsel + h:dsel + h + 1]
            rw = cum_r[dsel + h:dsel + h + 1, :]
            if bwd:
                dec = jnp.where(ss >= tt, jnp.exp(jnp.where(ss >= tt, rw - col, 0.0)), 0.0)
            else:
                dec = jnp.where(ss <= tt, jnp.exp(jnp.where(ss <= tt, col - rw, 0.0)), 0.0)
            yh = _dot((gmat[h // 2] * dec).astype(BF16), xdt16)
            y = jnp.where(lane_head == h, yh, y)
        st = st_ref[...]
        if bwd:
            out_dec = jnp.exp(total_c - cum_c)
            st_dec = jnp.exp(cum_c)
        else:
            out_dec = jnp.exp(cum_c)
            st_dec = jnp.exp(total_c - cum_c)
        y = y + _dot_nt(cm16, st.astype(BF16)) * expand(out_dec)
        upd = _dot_tn((xdt * expand(st_dec)).astype(BF16), bm16)
        tot_rows = jnp.broadcast_to(jnp.exp(total_c[:, dsel:dsel + 1]), (HD, LANES))
        decay_rows = jnp.concatenate(
            [tot_rows] + [jnp.broadcast_to(jnp.exp(total_c[:, dsel + h:dsel + h + 1]), (HD, LANES))
                          for h in range(1, 4)], axis=0)
        st_ref[...] = st * decay_rows + jnp.where(st_mask, upd, 0.0)
        if bwd:
            yy = yf_ref[0, rows, :] + y + xs * dskip_row
            yy = yy * _silu(z_ref[0, rows, :])
            ms = _head_mean_sq(yy, SSD_W)
            o_ref[0, rows, :] = (yy * lax.rsqrt(ms + EPS) * ng_ref[...]).astype(o_ref.dtype)
        else:
            o_ref[0, rows, :] = y
        return carry

    lax.fori_loop(0, nch, chunk, 0, unroll=min(nch, 8))


def _ssd_call(xbc, ssd_in, dtt, par_rows, par_cols, norm_g, y_fwd):
    b, l, _ = xbc.shape
    bwd = y_fwd is not None
    t = min(l, 512)
    nb = l // t
    blk = (lambda bi, j: (bi, nb - 1 - j)) if bwd else (lambda bi, j: (bi, j))
    rows3 = lambda lane_blk: (lambda bi, j: blk(bi, j) + (lane_blk,))
    in_specs = [pl.BlockSpec((1, t, SSD_XBC), rows3(0)),
                pl.BlockSpec((1, t, LANES), rows3((SSD_W + SSD_XBC) // LANES)),
                pl.BlockSpec((1, 8, t), lambda bi, j: (bi, 0, blk(bi, j)[1])),
                pl.BlockSpec((8, SSD_W), lambda bi, j: (0, 0)),
                pl.BlockSpec((8, LANES), lambda bi, j: (0, 0))]
    args = [xbc, ssd_in, dtt, par_rows, par_cols]
    if bwd:
        in_specs += [pl.BlockSpec((1, t, SSD_W), rows3(0)),
                     pl.BlockSpec((1, t, SSD_W), rows3(0)),
                     pl.BlockSpec((1, SSD_W), lambda bi, j: (0, 0))]
        args += [ssd_in, y_fwd, norm_g]
    return pl.pallas_call(
        functools.partial(_ssd_body, bwd, t // SSD_CHUNK), name="ssd_bwd" if bwd else "ssd_fwd",
        grid=(b, nb), in_specs=in_specs,
        out_specs=pl.BlockSpec((1, t, SSD_W), rows3(0)),
        out_shape=jax.ShapeDtypeStruct((b, l, SSD_W), BF16 if bwd else F32),
        scratch_shapes=[pltpu.VMEM((SSD_W, LANES), F32)],
        compiler_params=_cp(("arbitrary", "arbitrary")),
    )(*args)


def _outproj_body(x_ref, hg_ref, at_ref, ssd_ref, w1_ref, w2_ref, w3_ref, g1_ref, ng_ref, sc_ref, sh_ref,
                  wr_ref, x1_ref, h2_ref, aff_ref, afft_ref):
    mix = _dot(hg_ref[0], w1_ref[...]) + _dot(at_ref[0], w2_ref[...]) + _dot(ssd_ref[0], w3_ref[...])
    x1 = x_ref[0] + g1_ref[0] * mix
    x1_ref[0] = x1
    ms = jnp.mean(x1 * x1, axis=-1, keepdims=True)
    h2 = x1 * lax.rsqrt(ms + EPS) * ng_ref[...] * (1.0 + sc_ref[0]) + sh_ref[0]
    h2_ref[0] = h2.astype(h2_ref.dtype)
    h_hi, h_lo = _split_bf16(h2)
    logits = _dot(h_hi, wr_ref[0]) + _dot(h_lo, wr_ref[0]) + _dot(h_hi, wr_ref[1])
    valid = _iota((1, LANES), 1) < N_EXPERTS
    logits = jnp.where(valid, logits, -jnp.inf)
    e = jnp.exp(logits - jnp.max(logits, axis=-1, keepdims=True))
    aff = e / jnp.sum(e, axis=-1, keepdims=True)
    aff_ref[0] = aff
    afft_ref[...] = jnp.transpose(aff)[0:N_EXPERTS, :]


def _outproj_call(x, o_hg, o_at, o_ssd, w1, w2, w3, gate1, norm_g, scale, shift, w_r):
    b, l, _ = x.shape
    tm = min(l, 512)
    nbl = l // tm
    row = lambda w: pl.BlockSpec((1, tm, w), lambda bi, i: (bi, i, 0))
    per_b = lambda: pl.BlockSpec((1, 1, D), lambda bi, i: (bi, 0, 0))
    const = lambda shape: pl.BlockSpec(shape, lambda bi, i: (0, 0))
    return pl.pallas_call(
        _outproj_body, name="outproj_router",
        grid=(b, nbl),
        in_specs=[row(D), row(HG_W), row(AT_W), row(SSD_W), const((HG_W, D)), const((AT_W, D)), const((SSD_W, D)),
                  per_b(), const((1, D)), per_b(), per_b(),
                  pl.BlockSpec((2, D, LANES), lambda bi, i: (0, 0, 0))],
        out_specs=[row(D), row(D), row(LANES),
                   pl.BlockSpec((N_EXPERTS, tm), lambda bi, i: (0, bi * nbl + i))],
        out_shape=[jax.ShapeDtypeStruct((b, l, D), F32), jax.ShapeDtypeStruct((b, l, D), BF16),
                   jax.ShapeDtypeStruct((b, l, LANES), F32), jax.ShapeDtypeStruct((N_EXPERTS, b * l), F32)],
        compiler_params=_cp(("arbitrary", "arbitrary")),
    )(x, o_hg, o_at, o_ssd, w1, w2, w3, gate1, norm_g.reshape(1, D), scale, shift, w_r)


def _route_thr_body(cap, n_tok, afft_ref, o_ref, t_ref):
    lane_chunk = 2048 if n_tok % 2048 == 0 else ROUTE_BLK
    n_chunks = n_tok // lane_chunk

    def count_ge(cand):
        def body(i, acc):
            bits = lax.bitcast_convert_type(afft_ref[:, pl.ds(pl.multiple_of(i * lane_chunk, lane_chunk), lane_chunk)], I32)
            return acc + (bits >= cand).astype(F32)
        acc = lax.fori_loop(0, n_chunks, body, jnp.zeros((N_EXPERTS, lane_chunk), F32))
        return jnp.sum(acc, axis=1, keepdims=True)

    def bit_step(i, thr):
        cand = thr | jnp.left_shift(jnp.int32(1), 30 - i)
        return jnp.where(count_ge(cand) >= cap, cand, thr)

    thr = lax.fori_loop(0, 31, bit_step, jnp.zeros((N_EXPERTS, 1), I32))
    n_gt = count_ge(thr + 1)
    need = cap - n_gt

    nblk = n_tok // ROUTE_BLK
    nbp = o_ref.shape[2]
    lane = _iota((1, nbp), 1)

    def blk_counts(j, carry):
        gt_tab, eq_tab = carry
        bits = lax.bitcast_convert_type(afft_ref[:, pl.ds(pl.multiple_of(j * ROUTE_BLK, ROUTE_BLK), ROUTE_BLK)], I32)
        cg = jnp.sum((bits > thr).astype(F32), axis=1, keepdims=True)
        ce = jnp.sum((bits == thr).astype(F32), axis=1, keepdims=True)
        return (gt_tab + jnp.where(lane == j, cg, 0.0), eq_tab + jnp.where(lane == j, ce, 0.0))

    zeros = jnp.zeros((N_EXPERTS, nbp), F32)
    gt_tab, eq_tab = lax.fori_loop(0, nblk, blk_counts, (zeros, zeros))
    strict = (_iota((nbp, nbp), 0) < _iota((nbp, nbp), 1)).astype(F32)
    eq_before = _dot(eq_tab, strict, HI)
    eq_take = jnp.clip(need - eq_before, 0.0, eq_tab)
    cnt = gt_tab + eq_take
    cnt8 = jnp.floor((cnt + 7.0) * 0.125) * 8.0
    off = _dot(cnt8, strict, HI)
    o_ref[0] = cnt.astype(I32)
    o_ref[1] = off.astype(I32)
    o_ref[2] = jnp.broadcast_to(thr, (N_EXPERTS, nbp))
    o_ref[3] = jnp.broadcast_to(need.astype(I32), (N_EXPERTS, nbp))
    o_ref[4] = eq_before.astype(I32)
    ident = (_iota((N_EXPERTS, LANES), 0) == _iota((N_EXPERTS, LANES), 1)).astype(F32)
    flip = lambda tab: _dot_tn(tab, ident, HI).astype(I32)
    thr_b = jnp.broadcast_to(thr, (N_EXPERTS, nbp))
    thr_hi = flip(jnp.right_shift(thr_b, 15).astype(F32))
    thr_lo = flip(jnp.bitwise_and(thr_b, 0x7FFF).astype(F32))
    t_ref[0] = jnp.left_shift(thr_hi, 15) | thr_lo
    t_ref[1] = flip(jnp.broadcast_to(need, (N_EXPERTS, nbp)))
    t_ref[2] = flip(eq_before)


def _route_thr_call(afft, cap):
    n_tok = afft.shape[1]
    nbp = -(-(n_tok // ROUTE_BLK) // LANES) * LANES
    return pl.pallas_call(
        functools.partial(_route_thr_body, cap, n_tok), name="route_threshold",
        grid=(1,),
        in_specs=[pl.BlockSpec((N_EXPERTS, n_tok), lambda i: (0, 0))],
        out_specs=[pl.BlockSpec((5, N_EXPERTS, nbp), lambda i: (0, 0, 0)),
                   pl.BlockSpec((3, nbp, LANES), lambda i: (0, 0, 0))],
        out_shape=[jax.ShapeDtypeStruct((5, N_EXPERTS, nbp), I32),
                   jax.ShapeDtypeStruct((3, nbp, LANES), I32)],
        compiler_params=_cp(("arbitrary",)),
    )(afft)


def _route_select(aff_ref, tab_ref, tri_ref):
    j = pl.program_id(0)
    bits = lax.bitcast_convert_type(aff_ref[...], I32)
    thr = tab_ref[0, 0:1, :]
    need = tab_ref[1, 0:1, :].astype(F32)
    eq_before = tab_ref[2, pl.ds(j, 1), :].astype(F32)
    gt = bits > thr
    eq = bits == thr
    tri = tri_ref[...]
    eq_rank = _dot(tri, eq.astype(BF16)) + eq_before
    sel = gt | (eq & (eq_rank < need))
    rank = _dot(tri, sel.astype(BF16))
    return jnp.where(sel, rank, -1.0)


def _onehot_pair(sel_rank, e, wi):
    slot = (_iota((1, 2 * ROUTE_WIN), 1) % ROUTE_WIN + wi * ROUTE_WIN).astype(F32)
    return jnp.where(sel_rank[:, e:e + 1] == slot, 1.0, 0.0).astype(BF16)


def _gather_body(nblk, cnt_s, off_s, afft_ref, tab_ref, tri_ref, h2_ref, xg_in, xg_hbm, oh_ref, stage, sem):
    del xg_in
    j = pl.program_id(0)
    w = ROUTE_WIN
    aff = afft_ref[...]
    bits = lax.bitcast_convert_type(aff, I32)
    blk_lane = _iota((1, tab_ref.shape[2]), 1)
    thr = tab_ref[2, :, 0:1]
    need = tab_ref[3, :, 0:1].astype(F32)
    eq_before = jnp.sum(jnp.where(blk_lane == j, tab_ref[4], 0), axis=1, keepdims=True).astype(F32)
    gt = bits > thr
    eq = bits == thr
    tri = tri_ref[...]
    eq_rank = _dot_nt(eq.astype(BF16), tri) + eq_before
    sel = gt | (eq & (eq_rank < need))
    rank = _dot_nt(sel.astype(BF16), tri)
    sel_rank = jnp.where(sel, rank, -1.0)
    h2 = h2_ref[...]

    def onehot(e, wi):
        slot = (_iota((w, 1), 0) + wi * w).astype(F32)
        return sel_rank[e:e + 1] == slot

    def gate_lanes(e, oh):
        gate = jnp.sum(jnp.where(oh, aff[e:e + 1], 0.0), axis=1, keepdims=True)
        return jnp.broadcast_to(gate, (w, LANES))

    slot = j % 2

    def group_copy(sl, e, off, wi, r):
        return pltpu.make_async_copy(stage.at[sl, e, pl.ds(r * 8, 8)],
                                     xg_hbm.at[e, pl.ds(pl.multiple_of(off + wi * w + r * 8, 8), 8)], sem.at[sl, e])

    def groups(cnt, wi):
        return (jnp.minimum(cnt - wi * w, w) + 7) // 8

    def start_groups(sl, e, cnt, off, wi):
        lax.fori_loop(0, groups(cnt, wi), lambda r, c: (group_copy(sl, e, off, wi, r).start(), c)[1], 0)

    def wait_groups(sl, e, cnt, off, wi):
        lax.fori_loop(0, groups(cnt, wi), lambda r, c: (group_copy(sl, e, off, wi, r).wait(), c)[1], 0)

    def wait_last_window(sl, step):
        for e in range(N_EXPERTS):
            cnt = cnt_s[e * nblk + step]
            wait_groups(sl, e, cnt, off_s[e * nblk + step], jnp.maximum((cnt + (w - 1)) // w - 1, 0))

    for e in range(N_EXPERTS):
        oh = onehot(e, 0)
        oh_ref[e * w:(e + 1) * w, :] = jnp.where(oh, 1.0, 0.0).astype(BF16)
        stage[slot, e, :, D:] = gate_lanes(e, oh)
    stage[slot, :, :, 0:D] = _dot(oh_ref[...], h2).reshape(N_EXPERTS, w, D)
    for e in range(N_EXPERTS):
        start_groups(slot, e, cnt_s[e * nblk + j], off_s[e * nblk + j], 0)

    for e in range(N_EXPERTS):
        cnt = cnt_s[e * nblk + j]
        off = off_s[e * nblk + j]
        n_win = (cnt + (w - 1)) // w

        def window(wi, carry, e=e, cnt=cnt, off=off):
            wait_groups(slot, e, cnt, off, wi - 1)
            oh = onehot(e, wi)
            stage[slot, e, :, 0:D] = _dot(jnp.where(oh, 1.0, 0.0).astype(BF16), h2)
            stage[slot, e, :, D:] = gate_lanes(e, oh)
            start_groups(slot, e, cnt, off, wi)
            return carry

        lax.fori_loop(1, n_win, window, 0)

    @pl.when(j > 0)
    def _():
        wait_last_window(1 - slot, j - 1)

    @pl.when(j == nblk - 1)
    def _():
        wait_last_window(slot, j)


def _gather_call(cnt, off, afft, tab_e, tri, h2_flat, slots_alloc):
    n_tok = h2_flat.shape[0]
    nblk = n_tok // ROUTE_BLK
    xg0 = jnp.zeros((N_EXPERTS, slots_alloc, D + LANES), F32)
    grid_spec = pltpu.PrefetchScalarGridSpec(
        num_scalar_prefetch=2,
        grid=(nblk,),
        in_specs=[pl.BlockSpec((N_EXPERTS, ROUTE_BLK), lambda j, *_: (0, j)),
                  pl.BlockSpec(tab_e.shape, lambda j, *_: (0, 0, 0)),
                  pl.BlockSpec((ROUTE_BLK, ROUTE_BLK), lambda j, *_: (0, 0)),
                  pl.BlockSpec((ROUTE_BLK, D), lambda j, *_: (j, 0)),
                  pl.BlockSpec(memory_space=pl.ANY)],
        out_specs=pl.BlockSpec(memory_space=pl.ANY),
        scratch_shapes=[pltpu.VMEM((N_EXPERTS * ROUTE_WIN, ROUTE_BLK), BF16),
                        pltpu.VMEM((2, N_EXPERTS, ROUTE_WIN, D + LANES), F32),
                        pltpu.SemaphoreType.DMA((2, N_EXPERTS))],
    )
    return pl.pallas_call(
        functools.partial(_gather_body, nblk), name="expert_gather",
        grid_spec=grid_spec,
        out_shape=jax.ShapeDtypeStruct((N_EXPERTS, slots_alloc, D + LANES), F32),
        input_output_aliases={6: 0},
        compiler_params=_cp(("arbitrary",)),
    )(cnt, off, afft, tab_e, tri, h2_flat, xg0)


def _ffn_body(used_s, x_ref, wg_ref, wu_ref, wd_ref, y_ref):
    e = pl.program_id(0)
    i = pl.program_id(1)

    @pl.when(i * SLOT_TILE < used_s[e])
    def _():
        xg = x_ref[0, :, 0:D].astype(BF16)
        gate = x_ref[0, :, D:D + 1]
        hid = _silu(_dot(xg, wg_ref[0])) * _dot(xg, wu_ref[0])
        y_ref[0] = _dot(hid.astype(BF16), wd_ref[0]) * gate

    @pl.when(i * SLOT_TILE >= used_s[e])
    def _():
        y_ref[...] = jnp.zeros_like(y_ref)


def _ffn_call(used, xg, wg, wu, wd):
    _, slots_alloc, _ = xg.shape
    wspec = lambda: pl.BlockSpec((1, D, D), lambda e, i, used_s: (e, 0, 0))
    last_tile = lambda e, i, used_s: jnp.minimum(i, jnp.maximum(used_s[e] - 1, 0) // SLOT_TILE)
    grid_spec = pltpu.PrefetchScalarGridSpec(
        num_scalar_prefetch=1,
        grid=(N_EXPERTS, slots_alloc // SLOT_TILE),
        in_specs=[pl.BlockSpec((1, SLOT_TILE, D + LANES), lambda e, i, used_s: (e, last_tile(e, i, used_s), 0)),
                  wspec(), wspec(), wspec()],
        out_specs=pl.BlockSpec((1, SLOT_TILE, D), lambda e, i, used_s: (e, i, 0)),
    )
    return pl.pallas_call(
        _ffn_body, name="expert_ffn",
        grid_spec=grid_spec,
        out_shape=jax.ShapeDtypeStruct((N_EXPERTS, slots_alloc, D), F32),
        compiler_params=_cp(("arbitrary", "arbitrary")),
    )(used, xg, wg, wu, wd)


def _combine_body(final, nblk, cnt_s, off_s, aff_ref, tab_ref, tri_ref, x1_ref, g2_ref, fg_ref, y_hbm,
                  o_ref, ybuf, ysplit, oh_ref, acc_ref, sems):
    j = pl.program_id(0)
    sel_rank = _route_select(aff_ref, tab_ref, tri_ref)
    w = ROUTE_WIN

    slot = j % 2

    def win_copy(sl, e, off, wi):
        return pltpu.make_async_copy(y_hbm.at[e, pl.ds(pl.multiple_of(off + wi * w, 8), w)], ybuf.at[sl, e],
                                     sems.at[sl, e])

    split = _split_bf16

    @pl.when(j == 0)
    def _():
        for e in range(N_EXPERTS):
            win_copy(0, e, off_s[e * nblk], 0).start()

    @pl.when(j + 1 < nblk)
    def _():
        for e in range(N_EXPERTS):
            win_copy(1 - slot, e, off_s[e * nblk + j + 1], 0).start()

    for e in range(N_EXPERTS):
        oh_ref[:, e * 2 * w:(e + 1) * 2 * w] = _onehot_pair(sel_rank, e, 0)
    for e in range(N_EXPERTS):
        win_copy(slot, e, off_s[e * nblk + j], 0).wait()
        y_hi, y_lo = split(ybuf[slot, e])
        ysplit[(2 * e) * w:(2 * e + 1) * w, :] = y_hi
        ysplit[(2 * e + 1) * w:(2 * e + 2) * w, :] = y_lo
    acc_ref[...] = _dot(oh_ref[...], ysplit[...])
    for e in range(N_EXPERTS):
        cnt = cnt_s[e * nblk + j]
        off = off_s[e * nblk + j]

        def window(wi, carry, e=e, off=off):
            cp = win_copy(slot, e, off, wi)
            cp.start()
            cp.wait()
            y_hi, y_lo = split(ybuf[slot, e])
            acc_ref[...] += _dot(_onehot_pair(sel_rank, e, wi), jnp.concatenate([y_hi, y_lo], axis=0))
            return carry

        lax.fori_loop(1, (cnt + (w - 1)) // w, window, 0)
    x2 = x1_ref[...] + g2_ref[0] * acc_ref[...]
    if final:
        ms = jnp.mean(x2 * x2, axis=-1, keepdims=True)
        x2 = x2 * lax.rsqrt(ms + EPS) * fg_ref[...]
    o_ref[...] = x2


def _combine_call(cnt, off, aff2d, tab_t, tri, x1_flat, gate2, final_g, y, blocks_per_batch, final):
    n_tok = x1_flat.shape[0]
    nblk = n_tok // ROUTE_BLK
    grid_spec = pltpu.PrefetchScalarGridSpec(
        num_scalar_prefetch=2,
        grid=(nblk,),
        in_specs=[pl.BlockSpec((ROUTE_BLK, LANES), lambda j, *_: (j, 0)),
                  pl.BlockSpec(tab_t.shape, lambda j, *_: (0, 0, 0)),
                  pl.BlockSpec((ROUTE_BLK, ROUTE_BLK), lambda j, *_: (0, 0)),
                  pl.BlockSpec((ROUTE_BLK, D), lambda j, *_: (j, 0)),
                  pl.BlockSpec((1, 1, D), lambda j, *_: (j // blocks_per_batch, 0, 0)),
                  pl.BlockSpec((1, D), lambda j, *_: (0, 0)),
                  pl.BlockSpec(memory_space=pl.ANY)],
        out_specs=pl.BlockSpec((ROUTE_BLK, D), lambda j, *_: (j, 0)),
        scratch_shapes=[pltpu.VMEM((2, N_EXPERTS, ROUTE_WIN, D), F32),
                        pltpu.VMEM((2 * N_EXPERTS * ROUTE_WIN, D), BF16),
                        pltpu.VMEM((ROUTE_BLK, 2 * N_EXPERTS * ROUTE_WIN), BF16),
                        pltpu.VMEM((ROUTE_BLK, D), F32),
                        pltpu.SemaphoreType.DMA((2, N_EXPERTS))],
    )
    return pl.pallas_call(
        functools.partial(_combine_body, final, nblk), name="expert_combine",
        grid_spec=grid_spec,
        out_shape=jax.ShapeDtypeStruct((n_tok, D), F32),
        compiler_params=_cp(("arbitrary",)),
    )(cnt, off, aff2d, tab_t, tri, x1_flat, gate2, final_g.reshape(1, D), y)


def _rope_tables(l):
    quarter = HD // 4
    inv = ROPE_THETA ** (-jnp.arange(quarter, dtype=F32) / quarter)
    t = jnp.arange(l)
    pos = jnp.stack([(t // GRID_W).astype(F32), (t % GRID_W).astype(F32)], axis=1)
    lane = np.arange(KV_W)
    which = (lane % HD) // (HD // 2)
    ang = pos[:, which] * inv[lane % quarter][None, :]
    sign = np.where((lane % (HD // 2)) < quarter, -1.0, 1.0).astype(np.float32)
    return jnp.cos(ang), jnp.sin(ang) * sign[None, :]


def _split_bf16_stack(w):
    hi = w.astype(BF16)
    return jnp.stack([hi, (w - hi.astype(F32)).astype(BF16)])


def _prep_layer(l, p):
    w_in = p['w_in'][l]
    n_main = HG_COLS + AT_COLS + SSD_W + SSD_XBC
    w_main = jnp.concatenate([w_in[:, :n_main], jnp.pad(w_in[:, n_main:], ((0, 0), (0, 120)))], axis=1).astype(BF16)
    w_dtt = w_in[:, n_main:].T.astype(BF16)
    lb_all = jnp.cumsum(jax.nn.softmax(p['hg_lb'].astype(F32), axis=0), axis=0)
    lb = (lb_all - lb_all[:1])[l]
    lb_rows = jnp.stack([jnp.log(lb), jnp.log1p(-lb), 1.0 - lb], axis=1)
    a_coef = -jnp.exp(p['a_log'][l].astype(F32)).reshape(-1)
    dt_bias = p['dt_bias'][l].astype(F32).reshape(-1)
    par_rows = jnp.zeros((8, SSD_W), F32).at[0, :8].set(dt_bias).at[1, :8].set(a_coef)
    par_rows = par_rows.at[2, :].set(jnp.repeat(p['d_skip'][l].astype(F32), HD))
    par_cols = jnp.zeros((8, LANES), F32).at[:, 0].set(dt_bias).at[:, 1].set(a_coef)
    w_out = p['w_out'][l].astype(BF16)
    lane_k = np.arange(KV_W)
    ek = np.zeros((KV_W, AT_W), np.float32)
    for rep in range(AT_GROUP):
        ek[lane_k, (lane_k // HD) * (AT_GROUP * HD) + rep * HD + lane_k % HD] = 1.0
    evt = np.zeros((2 * LANES, KV_W), np.float32)
    evt[(lane_k // HD) * LANES + lane_k % HD, lane_k] = 1.0
    place = np.zeros((AT_GROUP, HD, AT_GROUP * HD), np.float32)
    for g in range(AT_GROUP):
        place[g, np.arange(HD), g * HD + np.arange(HD)] = 1.0
    return dict(
        w_main=w_main, w_dtt=w_dtt, lb_rows=lb_rows,
        hg_norm=jnp.tile(p['hg_norm_g'][l], 4).reshape(1, HG_W),
        gq=jnp.tile(p['q_norm_g'][l], 8).reshape(1, AT_W), gk=jnp.tile(p['k_norm_g'][l], 2).reshape(1, KV_W),
        ek=jnp.asarray(ek, BF16), evt=jnp.asarray(evt, BF16), place=jnp.asarray(place, BF16),
        eye=jnp.asarray(np.eye(AT_W, dtype=np.float32), BF16),
        conv_w=jnp.pad(p['conv_w'][l], ((0, 8 - SSD_CONV), (0, 0))), conv_b=p['conv_b'][l].reshape(1, SSD_XBC),
        par_rows=par_rows, par_cols=par_cols,
        ssd_norm=jnp.tile(p['ssd_norm_g'][l], 4).reshape(1, SSD_W),
        w_o1=w_out[:HG_W], w_o2=w_out[HG_W:HG_W + AT_W], w_o3=w_out[HG_W + AT_W:],
        w_r=_split_bf16_stack(jnp.pad(p['w_router'][l], ((0, 0), (0, LANES - N_EXPERTS)))),
        wg=p['w_gate'][l].astype(BF16), wu=p['w_up'][l].astype(BF16), wd=p['w_down'][l].astype(BF16),
    )


def _trunk(x, mod, layers, p):
    b, l, _ = x.shape
    n_tok = b * l
    cap = EC_CAPACITY * n_tok // N_EXPERTS
    nblk = n_tok // ROUTE_BLK
    slots = cap + 8 * nblk
    n_tiles = -(-slots // SLOT_TILE)
    slots_alloc = -(-(n_tiles * SLOT_TILE + ROUTE_WIN) // SLOT_TILE) * SLOT_TILE
    cos2, sin2 = _rope_tables(l)
    tri = (np.arange(ROUTE_BLK)[None, :] < np.arange(ROUTE_BLK)[:, None]).astype(np.float32)
    tri = jnp.asarray(tri, BF16)
    for li in range(DEPTH):
        w = layers[li]
        sh1, sc1, g1, sh2, sc2, g2 = [mod[li, :, i * D:(i + 1) * D].reshape(b, 1, D) for i in range(6)]
        hg_in, at_in, ssd_in, dtt = _inproj_call(x, p['norm1_g'][li], sc1, sh1, w['w_main'], w['w_dtt'])
        o_f = _hgrn_call(hg_in, w['lb_rows'][0], w['hg_norm'], None)
        o_hg = _hgrn_call(hg_in, w['lb_rows'][1], w['hg_norm'], o_f)
        qt, kt, vt = _attn_prep_call(at_in, cos2, sin2, w['gq'], w['gk'], w['ek'], w['evt'], w['eye'])
        o_at = _attn_call(qt, kt, vt, w['place'])
        xbc = _ssd_conv_call(ssd_in, w['conv_w'], w['conv_b'])
        y_f = _ssd_call(xbc, ssd_in, dtt, w['par_rows'], w['par_cols'], w['ssd_norm'], None)
        o_ssd = _ssd_call(xbc, ssd_in, dtt, w['par_rows'], w['par_cols'], w['ssd_norm'], y_f)
        x1, h2, aff, afft = _outproj_call(x, o_hg, o_at, o_ssd, w['w_o1'], w['w_o2'], w['w_o3'], g1,
                                          p['norm2_g'][li], sc2, sh2, w['w_r'])
        tab_e, tab_t = _route_thr_call(afft, cap)
        cnt = tab_e[0, :, :nblk].reshape(-1)
        off = tab_e[1, :, :nblk].reshape(-1)
        aff2d = aff.reshape(n_tok, LANES)
        xg = _gather_call(cnt, off, afft, tab_e, tri, h2.reshape(n_tok, D), slots_alloc)
        used = tab_e[1, :, nblk - 1] + (tab_e[0, :, nblk - 1] + 7) // 8 * 8
        y = _ffn_call(used, xg, w['wg'], w['wu'], w['wd'])
        x = _combine_call(cnt, off, aff2d, tab_t, tri, x1.reshape(n_tok, D), g2, p['final_g'], y,
                          l // ROUTE_BLK, li == DEPTH - 1).reshape(b, l, D)
    return x


def kernel(x_prompt, x_sample, c_prompt, c_sample, norm1_g, norm2_g, w_mod, b_mod, w_in, hg_lb, hg_norm_g,
           q_norm_g, k_norm_g, conv_w, conv_b, a_log, dt_bias, d_skip, ssd_norm_g, w_out, w_router,
           w_gate, w_up, w_down, final_g):
    p = dict(norm1_g=norm1_g, norm2_g=norm2_g, w_in=w_in, hg_lb=hg_lb, hg_norm_g=hg_norm_g, q_norm_g=q_norm_g,
             k_norm_g=k_norm_g, conv_w=conv_w, conv_b=conv_b, a_log=a_log, dt_bias=dt_bias, d_skip=d_skip,
             ssd_norm_g=ssd_norm_g, w_out=w_out, w_router=w_router, w_gate=w_gate, w_up=w_up, w_down=w_down,
             final_g=final_g)
    bp, bs = c_prompt.shape[0], c_sample.shape[0]
    rows = -(-(bp + bs) // 8) * 8
    c_all = jnp.pad(jnp.concatenate([c_prompt, c_sample], axis=0), ((0, rows - bp - bs), (0, 0)))
    mod = _mod_call(c_all, w_mod, b_mod)
    layers = [_prep_layer(li, p) for li in range(DEPTH)]
    y_prompt = _trunk(x_prompt, mod[:, :bp], layers, p)
    y_sample = _trunk(x_sample, mod[:, bp:bp + bs], layers, p)
    return (y_prompt, y_sample)
```

```python
import functools

import numpy as np
import jax
import jax.numpy as jnp
from jax import lax
from jax.experimental import pallas as pl
from jax.experimental.pallas import tpu as pltpu

F32 = jnp.float32
BF16 = jnp.bfloat16
I32 = jnp.int32
HI = lax.Precision.HIGHEST

D = 1024
DEPTH = 2
GRID_W = 64
EPS = 1e-6
HD = 64
HG_W = 256
HG_CHUNK = 64
HG_SUB = 16
AT_W = 512
KV_W = 128
AT_GROUP = 4
ROPE_THETA = 10000.0
QK_SCALE_LOG2 = (HD ** -0.5) * 1.4426950408889634
SSD_W = 256
SSD_XBC = 512
SSD_CHUNK = 128
SSD_CONV = 5
N_EXPERTS = 16
EC_CAPACITY = 2
HG_COLS = 5 * HG_W
AT_COLS = AT_W + 2 * KV_W
SSD_COLS = SSD_W + SSD_XBC + 128
IN_PAD = HG_COLS + AT_COLS + SSD_COLS
LANES = 128
ATTN_KEY_BLK = 1024
SLOT_TILE = 256
ROUTE_BLK = 256
ROUTE_WIN = 64
VMEM_LIMIT = 56 * 1024 * 1024


def _cp(sem, vmem=VMEM_LIMIT):
    return pltpu.CompilerParams(dimension_semantics=sem, vmem_limit_bytes=vmem)


def _dot(a, b, prec=None):
    return jnp.dot(a, b, preferred_element_type=F32, precision=prec)


def _dot_nt(a, b, prec=None):
    return lax.dot_general(a, b, (((1,), (1,)), ((), ())), preferred_element_type=F32, precision=prec)


def _dot_tn(a, b, prec=None):
    return lax.dot_general(a, b, (((0,), (0,)), ((), ())), preferred_element_type=F32, precision=prec)


def _sigmoid(x):
    return 1.0 / (1.0 + jnp.exp(-x))


def _silu(x):
    return x * _sigmoid(x)


def _softplus(x):
    return jnp.maximum(x, 0.0) + jnp.log1p(jnp.exp(-jnp.abs(x)))


def _iota(shape, dim):
    return lax.broadcasted_iota(I32, shape, dim)


def _split_bf16(x):
    hi = x.astype(BF16)
    return hi, (x - hi.astype(F32)).astype(BF16)


def _head_mean_sq(x, width):
    bd = jnp.where(_iota((width, width), 0) // HD == _iota((width, width), 1) // HD, 1.0 / HD, 0.0).astype(BF16)
    hi, lo = _split_bf16(x * x)
    return _dot(hi, bd) + _dot(lo, bd)


def _mod_body(c_ref, w_ref, b_ref, o_ref):
    o_ref[0] = _dot(_silu(c_ref[...]), w_ref[0], HI) + b_ref[0]


def _mod_call(c_all, w_mod, b_mod):
    bp = c_all.shape[0]
    return pl.pallas_call(
        _mod_body, name="adaln_mod",
        grid=(DEPTH, 6),
        in_specs=[pl.BlockSpec((bp, D), lambda l, j: (0, 0)),
                  pl.BlockSpec((1, D, D), lambda l, j: (l, 0, j)),
                  pl.BlockSpec((1, 1, D), lambda l, j: (l, 0, j))],
        out_specs=pl.BlockSpec((1, bp, D), lambda l, j: (l, 0, j)),
        out_shape=jax.ShapeDtypeStruct((DEPTH, bp, 6 * D), F32),
        compiler_params=_cp(("arbitrary", "arbitrary")),
    )(c_all, w_mod, b_mod.reshape(DEPTH, 1, 6 * D))


def _inproj_body(x_ref, g_ref, sc_ref, sh_ref, w_ref, wdt_ref, hg_ref, at_ref, ssd_ref, dtt_ref):
    x = x_ref[0]
    ms = jnp.mean(x * x, axis=-1, keepdims=True)
    y = x * lax.rsqrt(ms + EPS) * g_ref[...]
    h = (y * (1.0 + sc_ref[0]) + sh_ref[0]).astype(BF16)
    p = _dot(h, w_ref[...])
    hg_ref[0] = p[:, :HG_COLS]
    at_ref[0] = p[:, HG_COLS:HG_COLS + AT_COLS]
    ssd_ref[0] = p[:, HG_COLS + AT_COLS:]
    dtt_ref[0] = _dot_nt(wdt_ref[...], h)


def _inproj_call(x, norm_g, scale, shift, w_main, w_dtt):
    b, l, _ = x.shape
    tm = min(l, 512)
    row = lambda: pl.BlockSpec((1, tm, D), lambda bi, i: (bi, i, 0))
    per_b = lambda: pl.BlockSpec((1, 1, D), lambda bi, i: (bi, 0, 0))
    in_specs = [row(), pl.BlockSpec((1, D), lambda bi, i: (0, 0)), per_b(), per_b(),
                pl.BlockSpec((D, IN_PAD), lambda bi, i: (0, 0)),
                pl.BlockSpec((8, D), lambda bi, i: (0, 0))]
    args = [x, norm_g.reshape(1, D), scale, shift, w_main, w_dtt]
    out_specs = [pl.BlockSpec((1, tm, HG_COLS), lambda bi, i: (bi, i, 0)),
                 pl.BlockSpec((1, tm, AT_COLS), lambda bi, i: (bi, i, 0)),
                 pl.BlockSpec((1, tm, SSD_COLS), lambda bi, i: (bi, i, 0)),
                 pl.BlockSpec((1, 8, tm), lambda bi, i: (bi, 0, i))]
    out_shape = [jax.ShapeDtypeStruct((b, l, HG_COLS), F32),
                 jax.ShapeDtypeStruct((b, l, AT_COLS), F32),
                 jax.ShapeDtypeStruct((b, l, SSD_COLS), F32),
                 jax.ShapeDtypeStruct((b, 8, l), F32)]
    return pl.pallas_call(
        _inproj_body, name="norm_inproj",
        grid=(b, l // tm), in_specs=in_specs, out_specs=out_specs, out_shape=out_shape,
        compiler_params=_cp(("arbitrary", "arbitrary")),
    )(*args)


def _hgrn_body(bwd, nch, *refs):
    if bwd:
        x_ref, lb_ref, ng_ref, of_ref, o_ref, st_ref = refs
    else:
        x_ref, lb_ref, o_ref, st_ref = refs

    @pl.when(pl.program_id(1) == 0)
    def _():
        st_ref[...] = jnp.zeros_like(st_ref)

    q_chunk = HG_CHUNK
    lane_head = _iota((1, HG_W), 1) // HD
    tril = (_iota((q_chunk, q_chunk), 1) <= _iota((q_chunk, q_chunk), 0)).astype(F32)
    bd_mask = _iota((HG_W, HG_W), 0) // HD == _iota((HG_W, HG_W), 1) // HD
    row_q = _iota((q_chunk, 1), 0)
    att_t = _iota((q_chunk, q_chunk), 0) % HG_SUB
    att_s = _iota((q_chunk, q_chunk), 1)
    log_lb = lb_ref[0:1, :]
    log_1m_lb = lb_ref[1:2, :]
    one_m_lb = lb_ref[2:3, :]
    fcol = 3 * HG_W if bwd else 2 * HG_W

    def chunk(ci, carry):
        c = (nch - 1 - ci) if bwd else ci
        r0 = pl.multiple_of(c * q_chunk, q_chunk)
        rows = pl.ds(r0, q_chunk)
        q = _silu(x_ref[0, rows, 0:HG_W])
        v = x_ref[0, rows, HG_W:2 * HG_W]
        fr = x_ref[0, rows, fcol:fcol + HG_W]
        a2 = log_1m_lb - _softplus(-fr)
        mx = jnp.maximum(log_lb, a2)
        logf = mx + jnp.log1p(jnp.exp(-jnp.abs(log_lb - a2)))
        k = one_m_lb * _sigmoid(-fr)
        b_inc = _dot(tril, logf, HI)
        b_exc = b_inc - logf
        total = b_inc[q_chunk - 1:q_chunk, :]
        st = st_ref[...]
        if bwd:
            q_in = q * jnp.exp(total - b_exc)
            k_st = k * jnp.exp(b_exc)
        else:
            q_in = q * jnp.exp(b_inc)
            k_st = k * jnp.exp(total - b_inc)
        inter = _dot_nt(q_in.astype(BF16), st.astype(BF16))
        upd = _dot_tn(v.astype(BF16), k_st.astype(BF16))
        st_ref[...] = st * jnp.exp(total) + jnp.where(bd_mask, upd, 0.0)
        v16 = v.astype(BF16)
        pieces = []
        for i in range(q_chunk // HG_SUB):
            lo, hi = i * HG_SUB, (i + 1) * HG_SUB
            if bwd:
                ref = b_inc[hi - 1:hi, :]
                qt = q[lo:hi] * jnp.exp(ref - b_exc[lo:hi])
                kt = k * jnp.exp(jnp.where(row_q >= lo, b_exc - ref, 0.0))
                kt = jnp.where(row_q >= lo, kt, 0.0)
                amask = att_s >= att_t + lo
            else:
                ref = b_exc[lo:lo + 1, :]
                qt = q[lo:hi] * jnp.exp(b_inc[lo:hi] - ref)
                kt = k * jnp.exp(jnp.where(row_q < hi, ref - b_inc, 0.0))
                kt = jnp.where(row_q < hi, kt, 0.0)
                amask = att_s <= att_t + lo
            q4 = jnp.concatenate([jnp.where(lane_head == h, qt, 0.0) for h in range(4)], axis=0)
            att = _dot_nt(q4.astype(BF16), kt.astype(BF16))
            att = jnp.where(amask, att, 0.0)
            r = _dot(att.astype(BF16), v16)
            o_i = jnp.where(lane_head == 0, r[0:HG_SUB], 0.0)
            for h in range(1, 4):
                o_i = o_i + jnp.where(lane_head == h, r[h * HG_SUB:(h + 1) * HG_SUB], 0.0)
            pieces.append(o_i + inter[lo:hi])
        o = jnp.concatenate(pieces, axis=0)
        if bwd:
            tot = of_ref[0, rows, :] + o
            ms = _head_mean_sq(tot, HG_W)
            g = x_ref[0, rows, 4 * HG_W:5 * HG_W]
            y = tot * lax.rsqrt(ms + EPS) * ng_ref[...] * _silu(g)
            o_ref[0, rows, :] = y.astype(o_ref.dtype)
        else:
            o_ref[0, rows, :] = o
        return carry

    lax.fori_loop(0, nch, chunk, 0, unroll=min(nch, 8))


def _hgrn_call(hg_in, lb_rows, norm_g, o_fwd):
    b, l, _ = hg_in.shape
    bwd = o_fwd is not None
    t = min(l, 512)
    nb = l // t
    idx = (lambda bi, j: (bi, nb - 1 - j, 0)) if bwd else (lambda bi, j: (bi, j, 0))
    in_specs = [pl.BlockSpec((1, t, HG_COLS), idx), pl.BlockSpec((3, HG_W), lambda bi, j: (0, 0))]
    args = [hg_in, lb_rows]
    if bwd:
        in_specs += [pl.BlockSpec((1, HG_W), lambda bi, j: (0, 0)), pl.BlockSpec((1, t, HG_W), idx)]
        args += [norm_g, o_fwd]
    return pl.pallas_call(
        functools.partial(_hgrn_body, bwd, t // HG_CHUNK), name="hgrn_bwd" if bwd else "hgrn_fwd",
        grid=(b, nb), in_specs=in_specs,
        out_specs=pl.BlockSpec((1, t, HG_W), idx),
        out_shape=jax.ShapeDtypeStruct((b, l, HG_W), BF16 if bwd else F32),
        scratch_shapes=[pltpu.VMEM((HG_W, HG_W), F32)],
        compiler_params=_cp(("arbitrary", "arbitrary")),
    )(*args)


def _rope_swap(x):
    w = x.shape[-1]
    first = (_iota((1, w), 1) % 32) < 16
    return jnp.where(first, pltpu.roll(x, w - 16, 1), pltpu.roll(x, 16, 1))


def _attn_prep_body(x_ref, cos_ref, sin_ref, gq_ref, gk_ref, ek_ref, evt_ref, eye_ref, qt_ref, k_ref, vt_ref):
    x = x_ref[0]
    cos2 = cos_ref[...]
    sin2 = sin_ref[...]
    xq = x[:, :AT_W]
    qn = xq * lax.rsqrt(_head_mean_sq(xq, AT_W) + EPS) * gq_ref[...]
    cos_q = jnp.concatenate([cos2] * 4, axis=1)
    sin_q = jnp.concatenate([sin2] * 4, axis=1)
    q = ((qn * cos_q + _rope_swap(qn) * sin_q) * QK_SCALE_LOG2).astype(BF16)
    qt_ref[0] = _dot_nt(eye_ref[...], q).astype(BF16)
    xk = x[:, AT_W:AT_W + KV_W]
    kn = xk * lax.rsqrt(_head_mean_sq(xk, KV_W) + EPS) * gk_ref[...]
    kr = (kn * cos2 + _rope_swap(kn) * sin2).astype(BF16)
    k_ref[0, 0] = _dot(kr, ek_ref[...]).astype(BF16)
    xv = x[:, AT_W + KV_W:].astype(BF16)
    ones_row = ((_iota((2 * LANES, 1), 0) % LANES) >= HD).astype(F32)
    vt_ref[0, 0] = (_dot_nt(evt_ref[...], xv) + ones_row).astype(BF16)


def _attn_prep_call(at_in, cos2, sin2, gq, gk, ek, evt, eye):
    b, l, _ = at_in.shape
    tm = min(l, ATTN_KEY_BLK)
    const = lambda shape: pl.BlockSpec(shape, lambda bi, i: (0, 0))
    return pl.pallas_call(
        _attn_prep_body, name="attn_prep",
        grid=(b, l // tm),
        in_specs=[pl.BlockSpec((1, tm, AT_COLS), lambda bi, i: (bi, i, 0)),
                  pl.BlockSpec((tm, KV_W), lambda bi, i: (i, 0)),
                  pl.BlockSpec((tm, KV_W), lambda bi, i: (i, 0)),
                  const((1, AT_W)), const((1, KV_W)), const((KV_W, AT_W)), const((2 * LANES, KV_W)),
                  const((AT_W, AT_W))],
        out_specs=[pl.BlockSpec((1, AT_W, tm), lambda bi, i: (bi, 0, i)),
                   pl.BlockSpec((1, 1, tm, AT_W), lambda bi, i: (bi, i, 0, 0)),
                   pl.BlockSpec((1, 1, 2 * LANES, tm), lambda bi, i: (bi, i, 0, 0))],
        out_shape=[jax.ShapeDtypeStruct((b, AT_W, l), BF16),
                   jax.ShapeDtypeStruct((b, l // tm, tm, AT_W), BF16),
                   jax.ShapeDtypeStruct((b, l // tm, 2 * LANES, tm), BF16)],
        compiler_params=_cp(("arbitrary", "arbitrary")),
    )(at_in, cos2, sin2, gq, gk, ek, evt, eye)


def _attn_body(tq, nk, qt_ref, k_ref, vt_ref, place_ref, o_ref, qm_ref, s_ref, m_ref, acc_ref):
    qm_ref[...] = jnp.zeros_like(qm_ref)
    for g in range(AT_GROUP):
        qm_ref[g * HD:(g + 1) * HD, g * tq:(g + 1) * tq] = qt_ref[0, g * HD:(g + 1) * HD, :]
    m_ref[...] = jnp.full_like(m_ref, -jnp.inf)
    acc_ref[...] = jnp.zeros_like(acc_ref)

    def scores(slot, kk):
        s_ref[slot] = _dot(k_ref[0, kk], qm_ref[...]).astype(BF16)

    def consume(slot, kk):
        s = s_ref[slot]
        m_old = m_ref[...]
        m_new = jnp.maximum(m_old, jnp.max(s, axis=0, keepdims=True).astype(F32))
        alpha = jnp.exp2(m_old - m_new)
        p = jnp.exp2(s - m_new.astype(BF16))
        acc_ref[...] = acc_ref[...] * alpha + _dot(vt_ref[0, kk], p)
        m_ref[...] = m_new

    scores(0, 0)
    if nk > 1:
        def pair(i, carry):
            kk = 2 * i
            scores(1, kk + 1)
            consume(0, kk)
            scores(0, kk + 2)
            consume(1, kk + 1)
            return carry

        lax.fori_loop(0, nk // 2 - 1, pair, 0)
        scores(1, nk - 1)
        consume(0, nk - 2)
        consume(1, nk - 1)
    else:
        consume(0, 0)
    acc = acc_ref[...]
    o_t = (acc[0:HD] / acc[HD:HD + 1]).astype(BF16)
    out = _dot_tn(o_t[:, 0:tq], place_ref[0])
    for g in range(1, AT_GROUP):
        out = out + _dot_tn(o_t[:, g * tq:(g + 1) * tq], place_ref[g])
    o_ref[0] = out.astype(o_ref.dtype)


def _attn_call(qt, kt, vt, place):
    b, _, l = qt.shape
    _, nk, tk, _ = kt.shape
    assert nk == 1 or nk % 2 == 0
    tq = min(l, 512)
    return pl.pallas_call(
        functools.partial(_attn_body, tq, nk), name="flash_attn",
        grid=(b, 2, l // tq),
        in_specs=[pl.BlockSpec((1, 4 * HD, tq), lambda bi, j, i: (bi, j, i)),
                  pl.BlockSpec((1, nk, tk, 4 * HD), lambda bi, j, i: (bi, 0, 0, j)),
                  pl.BlockSpec((1, nk, LANES, tk), lambda bi, j, i: (bi, 0, j, 0)),
                  pl.BlockSpec((AT_GROUP, HD, 4 * HD), lambda bi, j, i: (0, 0, 0))],
        out_specs=pl.BlockSpec((1, tq, 4 * HD), lambda bi, j, i: (bi, i, j)),
        out_shape=jax.ShapeDtypeStruct((b, l, AT_W), BF16),
        scratch_shapes=[pltpu.VMEM((4 * HD, AT_GROUP * tq), BF16),
                        pltpu.VMEM((2, tk, AT_GROUP * tq), BF16),
                        pltpu.VMEM((1, AT_GROUP * tq), F32),
                        pltpu.VMEM((LANES, AT_GROUP * tq), F32)],
        compiler_params=_cp(("arbitrary",) * 3),
    )(qt, kt, vt, place)


def _ssd_conv_body(t, x_ref, prev_ref, next_ref, w_ref, b_ref, o_ref, ext_ref):
    j = pl.program_id(1)
    nb = pl.num_programs(1)
    lo, hi = SSD_W, SSD_W + SSD_XBC
    ext_ref[0:8, :] = jnp.where(j > 0, prev_ref[0, :, lo:hi], 0.0)
    ext_ref[8:8 + t, :] = x_ref[0, :, lo:hi]
    ext_ref[8 + t:16 + t, :] = jnp.where(j < nb - 1, next_ref[0, :, lo:hi], 0.0)
    pad = SSD_CONV // 2
    acc = b_ref[...] + w_ref[0:1, :] * ext_ref[pl.ds(8 - pad, t), :]
    for kk in range(1, SSD_CONV):
        acc = acc + w_ref[kk:kk + 1, :] * ext_ref[pl.ds(8 - pad + kk, t), :]
    o_ref[0] = _silu(acc)


def _ssd_conv_call(ssd_in, conv_w, conv_b):
    b, l, _ = ssd_in.shape
    t = min(l, 512)
    t8 = t // 8
    last8 = l // 8 - 1
    return pl.pallas_call(
        functools.partial(_ssd_conv_body, t), name="ssd_conv",
        grid=(b, l // t),
        in_specs=[pl.BlockSpec((1, t, SSD_COLS), lambda bi, j: (bi, j, 0)),
                  pl.BlockSpec((1, 8, SSD_COLS), lambda bi, j: (bi, jnp.maximum(j * t8 - 1, 0), 0)),
                  pl.BlockSpec((1, 8, SSD_COLS), lambda bi, j: (bi, jnp.minimum((j + 1) * t8, last8), 0)),
                  pl.BlockSpec((8, SSD_XBC), lambda bi, j: (0, 0)),
                  pl.BlockSpec((1, SSD_XBC), lambda bi, j: (0, 0))],
        out_specs=pl.BlockSpec((1, t, SSD_XBC), lambda bi, j: (bi, j, 0)),
        out_shape=jax.ShapeDtypeStruct((b, l, SSD_XBC), F32),
        scratch_shapes=[pltpu.VMEM((t + 16, SSD_XBC), F32)],
        compiler_params=_cp(("arbitrary", "arbitrary")),
    )(ssd_in, ssd_in, ssd_in, conv_w, conv_b)


def _ssd_body(bwd, nch, *refs):
    if bwd:
        xbc_ref, dtc_ref, dtr_ref, par_ref, parc_ref, z_ref, yf_ref, ng_ref, o_ref, st_ref = refs
    else:
        xbc_ref, dtc_ref, dtr_ref, par_ref, parc_ref, o_ref, st_ref = refs

    @pl.when(pl.program_id(1) == 0)
    def _():
        st_ref[...] = jnp.zeros_like(st_ref)

    qc = SSD_CHUNK
    dsel = 4 if bwd else 0
    lane_head = _iota((1, SSD_W), 1) // HD
    lane_grp = _iota((1, LANES), 1) // HD
    tril = (_iota((qc, qc), 1) <= _iota((qc, qc), 0)).astype(F32)
    triu = (_iota((qc, qc), 0) <= _iota((qc, qc), 1)).astype(F32)
    tt = _iota((qc, qc), 0)
    ss = _iota((qc, qc), 1)
    st_mask = (_iota((SSD_W, LANES), 0) // (2 * HD)) == (_iota((SSD_W, LANES), 1) // HD)
    bias_row = par_ref[0:1, 0:LANES]
    acoef_row = par_ref[1:2, 0:LANES]
    dskip_row = par_ref[2:3, :]
    bias_col = parc_ref[:, 0:1]
    acoef_col = parc_ref[:, 1:2]

    def expand(col_vals):
        out = jnp.broadcast_to(col_vals[:, dsel:dsel + 1], (qc, SSD_W))
        for h in range(1, 4):
            out = jnp.where(lane_head == h, jnp.broadcast_to(col_vals[:, dsel + h:dsel + h + 1], (qc, SSD_W)), out)
        return out

    def chunk(ci, carry):
        c = (nch - 1 - ci) if bwd else ci
        r0 = pl.multiple_of(c * qc, qc)
        rows = pl.ds(r0, qc)
        xs = xbc_ref[0, rows, 0:SSD_W]
        bm = xbc_ref[0, rows, SSD_W:SSD_W + LANES]
        cm = xbc_ref[0, rows, SSD_W + LANES:SSD_W + 2 * LANES]
        dt_c = _softplus(dtc_ref[0, rows, :] + bias_row)
        a_c = dt_c * acoef_row
        dt_r = _softplus(dtr_ref[0, :, rows] + bias_col)
        a_r = dt_r * acoef_col
        inc_c = _dot(tril, a_c, HI)
        inc_r = _dot(a_r, triu, HI)
        if bwd:
            cum_c, cum_r = inc_c - a_c, inc_r - a_r
        else:
            cum_c, cum_r = inc_c, inc_r
        total_c = inc_c[qc - 1:qc, :]
        xdt = xs * expand(dt_c)
        xdt16 = xdt.astype(BF16)
        bm16 = bm.astype(BF16)
        cm16 = cm.astype(BF16)
        gmat = [_dot_nt(jnp.where(lane_grp == g, cm, 0.0).astype(BF16), bm16) for g in range(2)]
        y = jnp.zeros((qc, SSD_W), F32)
        for h in range(4):
            col = cum_c[:, dsel + h:dsel + h + 1]
            rw = cum_r[dsel + h:dsel + h + 1, :]
            if bwd:
                dec = jnp.where(ss >= tt, jnp.exp(jnp.where(ss >= tt, rw - col, 0.0)), 0.0)
            else:
                dec = jnp.where(ss <= tt, jnp.exp(jnp.where(ss <= tt, col - rw, 0.0)), 0.0)
            yh = _dot((gmat[h // 2] * dec).astype(BF16), xdt16)
            y = jnp.where(lane_head == h, yh, y)
        st = st_ref[...]
        if bwd:
            out_dec = jnp.exp(total_c - cum_c)
            st_dec = jnp.exp(cum_c)
        else:
            out_dec = jnp.exp(cum_c)
            st_dec = jnp.exp(total_c - cum_c)
        y = y + _dot_nt(cm16, st.astype(BF16)) * expand(out_dec)
        upd = _dot_tn((xdt * expand(st_dec)).astype(BF16), bm16)
        tot_rows = jnp.broadcast_to(jnp.exp(total_c[:, dsel:dsel + 1]), (HD, LANES))
        decay_rows = jnp.concatenate(
            [tot_rows] + [jnp.broadcast_to(jnp.exp(total_c[:, dsel + h:dsel + h + 1]), (HD, LANES))
                          for h in range(1, 4)], axis=0)
        st_ref[...] = st * decay_rows + jnp.where(st_mask, upd, 0.0)
        if bwd:
            yy = yf_ref[0, rows, :] + y + xs * dskip_row
            yy = yy * _silu(z_ref[0, rows, :])
            ms = _head_mean_sq(yy, SSD_W)
            o_ref[0, rows, :] = (yy * lax.rsqrt(ms + EPS) * ng_ref[...]).astype(o_ref.dtype)
        else:
            o_ref[0, rows, :] = y
        return carry

    lax.fori_loop(0, nch, chunk, 0, unroll=min(nch, 8))


def _ssd_call(xbc, ssd_in, dtt, par_rows, par_cols, norm_g, y_fwd):
    b, l, _ = xbc.shape
    bwd = y_fwd is not None
    t = min(l, 512)
    nb = l // t
    blk = (lambda bi, j: (bi, nb - 1 - j)) if bwd else (lambda bi, j: (bi, j))
    rows3 = lambda lane_blk: (lambda bi, j: blk(bi, j) + (lane_blk,))
    in_specs = [pl.BlockSpec((1, t, SSD_XBC), rows3(0)),
                pl.BlockSpec((1, t, LANES), rows3((SSD_W + SSD_XBC) // LANES)),
                pl.BlockSpec((1, 8, t), lambda bi, j: (bi, 0, blk(bi, j)[1])),
                pl.BlockSpec((8, SSD_W), lambda bi, j: (0, 0)),
                pl.BlockSpec((8, LANES), lambda bi, j: (0, 0))]
    args = [xbc, ssd_in, dtt, par_rows, par_cols]
    if bwd:
        in_specs += [pl.BlockSpec((1, t, SSD_W), rows3(0)),
                     pl.BlockSpec((1, t, SSD_W), rows3(0)),
                     pl.BlockSpec((1, SSD_W), lambda bi, j: (0, 0))]
        args += [ssd_in, y_fwd, norm_g]
    return pl.pallas_call(
        functools.partial(_ssd_body, bwd, t // SSD_CHUNK), name="ssd_bwd" if bwd else "ssd_fwd",
        grid=(b, nb), in_specs=in_specs,
        out_specs=pl.BlockSpec((1, t, SSD_W), rows3(0)),
        out_shape=jax.ShapeDtypeStruct((b, l, SSD_W), BF16 if bwd else F32),
        scratch_shapes=[pltpu.VMEM((SSD_W, LANES), F32)],
        compiler_params=_cp(("arbitrary", "arbitrary")),
    )(*args)


def _outproj_body(x_ref, hg_ref, at_ref, ssd_ref, w1_ref, w2_ref, w3_ref, g1_ref, ng_ref, sc_ref, sh_ref,
                  wr_ref, x1_ref, h2_ref, aff_ref, afft_ref):
    mix = _dot(hg_ref[0], w1_ref[...]) + _dot(at_ref[0], w2_ref[...]) + _dot(ssd_ref[0], w3_ref[...])
    x1 = x_ref[0] + g1_ref[0] * mix
    x1_ref[0] = x1
    ms = jnp.mean(x1 * x1, axis=-1, keepdims=True)
    h2 = x1 * lax.rsqrt(ms + EPS) * ng_ref[...] * (1.0 + sc_ref[0]) + sh_ref[0]
    h2_ref[0] = h2.astype(h2_ref.dtype)
    h_hi, h_lo = _split_bf16(h2)
    logits = _dot(h_hi, wr_ref[0]) + _dot(h_lo, wr_ref[0]) + _dot(h_hi, wr_ref[1])
    valid = _iota((1, LANES), 1) < N_EXPERTS
    logits = jnp.where(valid, logits, -jnp.inf)
    e = jnp.exp(logits - jnp.max(logits, axis=-1, keepdims=True))
    aff = e / jnp.sum(e, axis=-1, keepdims=True)
    aff_ref[0] = aff
    afft_ref[...] = jnp.transpose(aff)[0:N_EXPERTS, :]


def _outproj_call(x, o_hg, o_at, o_ssd, w1, w2, w3, gate1, norm_g, scale, shift, w_r):
    b, l, _ = x.shape
    tm = min(l, 512)
    nbl = l // tm
    row = lambda w: pl.BlockSpec((1, tm, w), lambda bi, i: (bi, i, 0))
    per_b = lambda: pl.BlockSpec((1, 1, D), lambda bi, i: (bi, 0, 0))
    const = lambda shape: pl.BlockSpec(shape, lambda bi, i: (0, 0))
    return pl.pallas_call(
        _outproj_body, name="outproj_router",
        grid=(b, nbl),
        in_specs=[row(D), row(HG_W), row(AT_W), row(SSD_W), const((HG_W, D)), const((AT_W, D)), const((SSD_W, D)),
                  per_b(), const((1, D)), per_b(), per_b(),
                  pl.BlockSpec((2, D, LANES), lambda bi, i: (0, 0, 0))],
        out_specs=[row(D), row(D), row(LANES),
                   pl.BlockSpec((N_EXPERTS, tm), lambda bi, i: (0, bi * nbl + i))],
        out_shape=[jax.ShapeDtypeStruct((b, l, D), F32), jax.ShapeDtypeStruct((b, l, D), BF16),
                   jax.ShapeDtypeStruct((b, l, LANES), F32), jax.ShapeDtypeStruct((N_EXPERTS, b * l), F32)],
        compiler_params=_cp(("arbitrary", "arbitrary")),
    )(x, o_hg, o_at, o_ssd, w1, w2, w3, gate1, norm_g.reshape(1, D), scale, shift, w_r)


def _route_thr_body(cap, n_tok, afft_ref, o_ref, t_ref):
    lane_chunk = 2048 if n_tok % 2048 == 0 else ROUTE_BLK
    n_chunks = n_tok // lane_chunk

    def count_ge(cand):
        def body(i, acc):
            bits = lax.bitcast_convert_type(afft_ref[:, pl.ds(pl.multiple_of(i * lane_chunk, lane_chunk), lane_chunk)], I32)
            return acc + (bits >= cand).astype(F32)
        acc = lax.fori_loop(0, n_chunks, body, jnp.zeros((N_EXPERTS, lane_chunk), F32))
        return jnp.sum(acc, axis=1, keepdims=True)

    def bit_step(i, thr):
        cand = thr | jnp.left_shift(jnp.int32(1), 30 - i)
        return jnp.where(count_ge(cand) >= cap, cand, thr)

    thr = lax.fori_loop(0, 31, bit_step, jnp.zeros((N_EXPERTS, 1), I32))
    n_gt = count_ge(thr + 1)
    need = cap - n_gt

    nblk = n_tok // ROUTE_BLK
    nbp = o_ref.shape[2]
    lane = _iota((1, nbp), 1)

    def blk_counts(j, carry):
        gt_tab, eq_tab = carry
        bits = lax.bitcast_convert_type(afft_ref[:, pl.ds(pl.multiple_of(j * ROUTE_BLK, ROUTE_BLK), ROUTE_BLK)], I32)
        cg = jnp.sum((bits > thr).astype(F32), axis=1, keepdims=True)
        ce = jnp.sum((bits == thr).astype(F32), axis=1, keepdims=True)
        return (gt_tab + jnp.where(lane == j, cg, 0.0), eq_tab + jnp.where(lane == j, ce, 0.0))

    zeros = jnp.zeros((N_EXPERTS, nbp), F32)
    gt_tab, eq_tab = lax.fori_loop(0, nblk, blk_counts, (zeros, zeros))
    strict = (_iota((nbp, nbp), 0) < _iota((nbp, nbp), 1)).astype(F32)
    eq_before = _dot(eq_tab, strict, HI)
    eq_take = jnp.clip(need - eq_before, 0.0, eq_tab)
    cnt = gt_tab + eq_take
    cnt8 = jnp.floor((cnt + 7.0) * 0.125) * 8.0
    off = _dot(cnt8, strict, HI)
    o_ref[0] = cnt.astype(I32)
    o_ref[1] = off.astype(I32)
    o_ref[2] = jnp.broadcast_to(thr, (N_EXPERTS, nbp))
    o_ref[3] = jnp.broadcast_to(need.astype(I32), (N_EXPERTS, nbp))
    o_ref[4] = eq_before.astype(I32)
    ident = (_iota((N_EXPERTS, LANES), 0) == _iota((N_EXPERTS, LANES), 1)).astype(F32)
    flip = lambda tab: _dot_tn(tab, ident, HI).astype(I32)
    thr_b = jnp.broadcast_to(thr, (N_EXPERTS, nbp))
    thr_hi = flip(jnp.right_shift(thr_b, 15).astype(F32))
    thr_lo = flip(jnp.bitwise_and(thr_b, 0x7FFF).astype(F32))
    t_ref[0] = jnp.left_shift(thr_hi, 15) | thr_lo
    t_ref[1] = flip(jnp.broadcast_to(need, (N_EXPERTS, nbp)))
    t_ref[2] = flip(eq_before)


def _route_thr_call(afft, cap):
    n_tok = afft.shape[1]
    nbp = -(-(n_tok // ROUTE_BLK) // LANES) * LANES
    return pl.pallas_call(
        functools.partial(_route_thr_body, cap, n_tok), name="route_threshold",
        grid=(1,),
        in_specs=[pl.BlockSpec((N_EXPERTS, n_tok), lambda i: (0, 0))],
        out_specs=[pl.BlockSpec((5, N_EXPERTS, nbp), lambda i: (0, 0, 0)),
                   pl.BlockSpec((3, nbp, LANES), lambda i: (0, 0, 0))],
        out_shape=[jax.ShapeDtypeStruct((5, N_EXPERTS, nbp), I32),
                   jax.ShapeDtypeStruct((3, nbp, LANES), I32)],
        compiler_params=_cp(("arbitrary",)),
    )(afft)


def _route_select(aff_ref, tab_ref, tri_ref):
    j = pl.program_id(0)
    bits = lax.bitcast_convert_type(aff_ref[...], I32)
    thr = tab_ref[0, 0:1, :]
    need = tab_ref[1, 0:1, :].astype(F32)
    eq_before = tab_ref[2, pl.ds(j, 1), :].astype(F32)
    gt = bits > thr
    eq = bits == thr
    tri = tri_ref[...]
    eq_rank = _dot(tri, eq.astype(BF16)) + eq_before
    sel = gt | (eq & (eq_rank < need))
    rank = _dot(tri, sel.astype(BF16))
    return jnp.where(sel, rank, -1.0)


def _onehot_pair(sel_rank, e, wi):
    slot = (_iota((1, 2 * ROUTE_WIN), 1) % ROUTE_WIN + wi * ROUTE_WIN).astype(F32)
    return jnp.where(sel_rank[:, e:e + 1] == slot, 1.0, 0.0).astype(BF16)


def _gather_body(nblk, cnt_s, off_s, afft_ref, tab_ref, tri_ref, h2_ref, xg_hbm, oh_ref, stage, zero_ref, sem, zsem):
    j = pl.program_id(0)
    w = ROUTE_WIN
    aff = afft_ref[...]
    bits = lax.bitcast_convert_type(aff, I32)
    blk_lane = _iota((1, tab_ref.shape[2]), 1)
    thr = tab_ref[2, :, 0:1]
    need = tab_ref[3, :, 0:1].astype(F32)
    eq_before = jnp.sum(jnp.where(blk_lane == j, tab_ref[4], 0), axis=1, keepdims=True).astype(F32)
    gt = bits > thr
    eq = bits == thr
    tri = tri_ref[...]
    eq_rank = _dot_nt(eq.astype(BF16), tri) + eq_before
    sel = gt | (eq & (eq_rank < need))
    rank = _dot_nt(sel.astype(BF16), tri)
    sel_rank = jnp.where(sel, rank, -1.0)
    h2 = h2_ref[...]

    def onehot(e, wi):
        slot = (_iota((w, 1), 0) + wi * w).astype(F32)
        return sel_rank[e:e + 1] == slot

    def gate_lanes(e, oh):
        gate = jnp.sum(jnp.where(oh, aff[e:e + 1], 0.0), axis=1, keepdims=True)
        return jnp.broadcast_to(gate, (w, LANES))

    slot = j % 2

    def group_copy(sl, e, off, wi, r):
        return pltpu.make_async_copy(stage.at[sl, e, pl.ds(r * 8, 8)],
                                     xg_hbm.at[e, pl.ds(pl.multiple_of(off + wi * w + r * 8, 8), 8)], sem.at[sl, e])

    def groups(cnt, wi):
        return (jnp.minimum(cnt - wi * w, w) + 7) // 8

    def start_groups(sl, e, cnt, off, wi):
        lax.fori_loop(0, groups(cnt, wi), lambda r, c: (group_copy(sl, e, off, wi, r).start(), c)[1], 0)

    def wait_groups(sl, e, cnt, off, wi):
        lax.fori_loop(0, groups(cnt, wi), lambda r, c: (group_copy(sl, e, off, wi, r).wait(), c)[1], 0)

    def wait_last_window(sl, step):
        for e in range(N_EXPERTS):
            cnt = cnt_s[e * nblk + step]
            wait_groups(sl, e, cnt, off_s[e * nblk + step], jnp.maximum((cnt + (w - 1)) // w - 1, 0))

    for e in range(N_EXPERTS):
        oh = onehot(e, 0)
        oh_ref[e * w:(e + 1) * w, :] = jnp.where(oh, 1.0, 0.0).astype(BF16)
        stage[slot, e, :, D:] = gate_lanes(e, oh)
    stage[slot, :, :, 0:D] = _dot(oh_ref[...], h2).reshape(N_EXPERTS, w, D)
    for e in range(N_EXPERTS):
        start_groups(slot, e, cnt_s[e * nblk + j], off_s[e * nblk + j], 0)

    for e in range(N_EXPERTS):
        cnt = cnt_s[e * nblk + j]
        off = off_s[e * nblk + j]
        n_win = (cnt + (w - 1)) // w

        def window(wi, carry, e=e, cnt=cnt, off=off):
            wait_groups(slot, e, cnt, off, wi - 1)
            oh = onehot(e, wi)
            stage[slot, e, :, 0:D] = _dot(jnp.where(oh, 1.0, 0.0).astype(BF16), h2)
            stage[slot, e, :, D:] = gate_lanes(e, oh)
            start_groups(slot, e, cnt, off, wi)
            return carry

        lax.fori_loop(1, n_win, window, 0)

    @pl.when(j > 0)
    def _():
        wait_last_window(1 - slot, j - 1)

    @pl.when(j == nblk - 1)
    def _():
        wait_last_window(slot, j)
        zero_ref[...] = jnp.zeros_like(zero_ref)
        for e in range(N_EXPERTS):
            used = off_s[e * nblk + j] + (cnt_s[e * nblk + j] + 7) // 8 * 8
            n_fill = ((SLOT_TILE - used % SLOT_TILE) % SLOT_TILE) // 8

            def fill_copy(r, e=e, used=used):
                return pltpu.make_async_copy(zero_ref, xg_hbm.at[e, pl.ds(pl.multiple_of(used + r * 8, 8), 8)],
                                             zsem.at[e])

            lax.fori_loop(0, n_fill, lambda r, c, f=fill_copy: (f(r).start(), c)[1], 0)
            lax.fori_loop(0, n_fill, lambda r, c, f=fill_copy: (f(r).wait(), c)[1], 0)


def _gather_call(cnt, off, afft, tab_e, tri, h2_flat, slots_alloc):
    n_tok = h2_flat.shape[0]
    nblk = n_tok // ROUTE_BLK
    grid_spec = pltpu.PrefetchScalarGridSpec(
        num_scalar_prefetch=2,
        grid=(nblk,),
        in_specs=[pl.BlockSpec((N_EXPERTS, ROUTE_BLK), lambda j, *_: (0, j)),
                  pl.BlockSpec(tab_e.shape, lambda j, *_: (0, 0, 0)),
                  pl.BlockSpec((ROUTE_BLK, ROUTE_BLK), lambda j, *_: (0, 0)),
                  pl.BlockSpec((ROUTE_BLK, D), lambda j, *_: (j, 0))],
        out_specs=pl.BlockSpec(memory_space=pl.ANY),
        scratch_shapes=[pltpu.VMEM((N_EXPERTS * ROUTE_WIN, ROUTE_BLK), BF16),
                        pltpu.VMEM((2, N_EXPERTS, ROUTE_WIN, D + LANES), F32),
                        pltpu.VMEM((8, D + LANES), F32),
                        pltpu.SemaphoreType.DMA((2, N_EXPERTS)),
                        pltpu.SemaphoreType.DMA((N_EXPERTS,))],
    )
    return pl.pallas_call(
        functools.partial(_gather_body, nblk), name="expert_gather",
        grid_spec=grid_spec,
        out_shape=jax.ShapeDtypeStruct((N_EXPERTS, slots_alloc, D + LANES), F32),
        compiler_params=_cp(("arbitrary",)),
    )(cnt, off, afft, tab_e, tri, h2_flat)


def _ffn_body(used_s, x_ref, wg_ref, wu_ref, wd_ref, y_ref):
    e = pl.program_id(0)
    i = pl.program_id(1)

    @pl.when(i * SLOT_TILE < used_s[e])
    def _():
        xg = x_ref[0, :, 0:D].astype(BF16)
        gate = x_ref[0, :, D:D + 1]
        hid = _silu(_dot(xg, wg_ref[0])) * _dot(xg, wu_ref[0])
        y_ref[0] = _dot(hid.astype(BF16), wd_ref[0]) * gate

    @pl.when(i * SLOT_TILE >= used_s[e])
    def _():
        y_ref[...] = jnp.zeros_like(y_ref)


def _ffn_call(used, xg, wg, wu, wd):
    _, slots_alloc, _ = xg.shape
    wspec = lambda: pl.BlockSpec((1, D, D), lambda e, i, used_s: (e, 0, 0))
    last_tile = lambda e, i, used_s: jnp.minimum(i, jnp.maximum(used_s[e] - 1, 0) // SLOT_TILE)
    grid_spec = pltpu.PrefetchScalarGridSpec(
        num_scalar_prefetch=1,
        grid=(N_EXPERTS, slots_alloc // SLOT_TILE),
        in_specs=[pl.BlockSpec((1, SLOT_TILE, D + LANES), lambda e, i, used_s: (e, last_tile(e, i, used_s), 0)),
                  wspec(), wspec(), wspec()],
        out_specs=pl.BlockSpec((1, SLOT_TILE, D), lambda e, i, used_s: (e, i, 0)),
    )
    return pl.pallas_call(
        _ffn_body, name="expert_ffn",
        grid_spec=grid_spec,
        out_shape=jax.ShapeDtypeStruct((N_EXPERTS, slots_alloc, D), F32),
        compiler_params=_cp(("arbitrary", "arbitrary")),
    )(used, xg, wg, wu, wd)


def _combine_body(final, nblk, cnt_s, off_s, aff_ref, tab_ref, tri_ref, x1_ref, g2_ref, fg_ref, y_hbm,
                  o_ref, ybuf, ysplit, oh_ref, acc_ref, sems):
    j = pl.program_id(0)
    sel_rank = _route_select(aff_ref, tab_ref, tri_ref)
    w = ROUTE_WIN

    slot = j % 2

    def win_copy(sl, e, off, wi):
        return pltpu.make_async_copy(y_hbm.at[e, pl.ds(pl.multiple_of(off + wi * w, 8), w)], ybuf.at[sl, e],
                                     sems.at[sl, e])

    split = _split_bf16

    @pl.when(j == 0)
    def _():
        for e in range(N_EXPERTS):
            win_copy(0, e, off_s[e * nblk], 0).start()

    @pl.when(j + 1 < nblk)
    def _():
        for e in range(N_EXPERTS):
            win_copy(1 - slot, e, off_s[e * nblk + j + 1], 0).start()

    for e in range(N_EXPERTS):
        oh_ref[:, e * 2 * w:(e + 1) * 2 * w] = _onehot_pair(sel_rank, e, 0)
    for e in range(N_EXPERTS):
        win_copy(slot, e, off_s[e * nblk + j], 0).wait()
        y_hi, y_lo = split(ybuf[slot, e])
        ysplit[(2 * e) * w:(2 * e + 1) * w, :] = y_hi
        ysplit[(2 * e + 1) * w:(2 * e + 2) * w, :] = y_lo
    acc_ref[...] = _dot(oh_ref[...], ysplit[...])
    for e in range(N_EXPERTS):
        cnt = cnt_s[e * nblk + j]
        off = off_s[e * nblk + j]

        def window(wi, carry, e=e, off=off):
            cp = win_copy(slot, e, off, wi)
            cp.start()
            cp.wait()
            y_hi, y_lo = split(ybuf[slot, e])
            acc_ref[...] += _dot(_onehot_pair(sel_rank, e, wi), jnp.concatenate([y_hi, y_lo], axis=0))
            return carry

        lax.fori_loop(1, (cnt + (w - 1)) // w, window, 0)
    x2 = x1_ref[...] + g2_ref[0] * acc_ref[...]
    if final:
        ms = jnp.mean(x2 * x2, axis=-1, keepdims=True)
        x2 = x2 * lax.rsqrt(ms + EPS) * fg_ref[...]
    o_ref[...] = x2


def _combine_call(cnt, off, aff2d, tab_t, tri, x1_flat, gate2, final_g, y, blocks_per_batch, final):
    n_tok = x1_flat.shape[0]
    nblk = n_tok // ROUTE_BLK
    grid_spec = pltpu.PrefetchScalarGridSpec(
        num_scalar_prefetch=2,
        grid=(nblk,),
        in_specs=[pl.BlockSpec((ROUTE_BLK, LANES), lambda j, *_: (j, 0)),
                  pl.BlockSpec(tab_t.shape, lambda j, *_: (0, 0, 0)),
                  pl.BlockSpec((ROUTE_BLK, ROUTE_BLK), lambda j, *_: (0, 0)),
                  pl.BlockSpec((ROUTE_BLK, D), lambda j, *_: (j, 0)),
                  pl.BlockSpec((1, 1, D), lambda j, *_: (j // blocks_per_batch, 0, 0)),
                  pl.BlockSpec((1, D), lambda j, *_: (0, 0)),
                  pl.BlockSpec(memory_space=pl.ANY)],
        out_specs=pl.BlockSpec((ROUTE_BLK, D), lambda j, *_: (j, 0)),
        scratch_shapes=[pltpu.VMEM((2, N_EXPERTS, ROUTE_WIN, D), F32),
                        pltpu.VMEM((2 * N_EXPERTS * ROUTE_WIN, D), BF16),
                        pltpu.VMEM((ROUTE_BLK, 2 * N_EXPERTS * ROUTE_WIN), BF16),
                        pltpu.VMEM((ROUTE_BLK, D), F32),
                        pltpu.SemaphoreType.DMA((2, N_EXPERTS))],
    )
    return pl.pallas_call(
        functools.partial(_combine_body, final, nblk), name="expert_combine",
        grid_spec=grid_spec,
        out_shape=jax.ShapeDtypeStruct((n_tok, D), F32),
        compiler_params=_cp(("arbitrary",)),
    )(cnt, off, aff2d, tab_t, tri, x1_flat, gate2, final_g.reshape(1, D), y)


def _rope_tables(l):
    quarter = HD // 4
    inv = ROPE_THETA ** (-jnp.arange(quarter, dtype=F32) / quarter)
    t = jnp.arange(l)
    pos = jnp.stack([(t // GRID_W).astype(F32), (t % GRID_W).astype(F32)], axis=1)
    lane = np.arange(KV_W)
    which = (lane % HD) // (HD // 2)
    ang = pos[:, which] * inv[lane % quarter][None, :]
    sign = np.where((lane % (HD // 2)) < quarter, -1.0, 1.0).astype(np.float32)
    return jnp.cos(ang), jnp.sin(ang) * sign[None, :]


def _split_bf16_stack(w):
    hi = w.astype(BF16)
    return jnp.stack([hi, (w - hi.astype(F32)).astype(BF16)])


def _prep_layer(l, p):
    w_in = p['w_in'][l]
    n_main = HG_COLS + AT_COLS + SSD_W + SSD_XBC
    w_main = jnp.concatenate([w_in[:, :n_main], jnp.pad(w_in[:, n_main:], ((0, 0), (0, 120)))], axis=1).astype(BF16)
    w_dtt = w_in[:, n_main:].T.astype(BF16)
    lb_all = jnp.cumsum(jax.nn.softmax(p['hg_lb'].astype(F32), axis=0), axis=0)
    lb = (lb_all - lb_all[:1])[l]
    lb_rows = jnp.stack([jnp.log(lb), jnp.log1p(-lb), 1.0 - lb], axis=1)
    a_coef = -jnp.exp(p['a_log'][l].astype(F32)).reshape(-1)
    dt_bias = p['dt_bias'][l].astype(F32).reshape(-1)
    par_rows = jnp.zeros((8, SSD_W), F32).at[0, :8].set(dt_bias).at[1, :8].set(a_coef)
    par_rows = par_rows.at[2, :].set(jnp.repeat(p['d_skip'][l].astype(F32), HD))
    par_cols = jnp.zeros((8, LANES), F32).at[:, 0].set(dt_bias).at[:, 1].set(a_coef)
    w_out = p['w_out'][l].astype(BF16)
    lane_k = np.arange(KV_W)
    ek = np.zeros((KV_W, AT_W), np.float32)
    for rep in range(AT_GROUP):
        ek[lane_k, (lane_k // HD) * (AT_GROUP * HD) + rep * HD + lane_k % HD] = 1.0
    evt = np.zeros((2 * LANES, KV_W), np.float32)
    evt[(lane_k // HD) * LANES + lane_k % HD, lane_k] = 1.0
    place = np.zeros((AT_GROUP, HD, AT_GROUP * HD), np.float32)
    for g in range(AT_GROUP):
        place[g, np.arange(HD), g * HD + np.arange(HD)] = 1.0
    return dict(
        w_main=w_main, w_dtt=w_dtt, lb_rows=lb_rows,
        hg_norm=jnp.tile(p['hg_norm_g'][l], 4).reshape(1, HG_W),
        gq=jnp.tile(p['q_norm_g'][l], 8).reshape(1, AT_W), gk=jnp.tile(p['k_norm_g'][l], 2).reshape(1, KV_W),
        ek=jnp.asarray(ek, BF16), evt=jnp.asarray(evt, BF16), place=jnp.asarray(place, BF16),
        eye=jnp.asarray(np.eye(AT_W, dtype=np.float32), BF16),
        conv_w=jnp.pad(p['conv_w'][l], ((0, 8 - SSD_CONV), (0, 0))), conv_b=p['conv_b'][l].reshape(1, SSD_XBC),
        par_rows=par_rows, par_cols=par_cols,
        ssd_norm=jnp.tile(p['ssd_norm_g'][l], 4).reshape(1, SSD_W),
        w_o1=w_out[:HG_W], w_o2=w_out[HG_W:HG_W + AT_W], w_o3=w_out[HG_W + AT_W:],
        w_r=_split_bf16_stack(jnp.pad(p['w_router'][l], ((0, 0), (0, LANES - N_EXPERTS)))),
        wg=p['w_gate'][l].astype(BF16), wu=p['w_up'][l].astype(BF16), wd=p['w_down'][l].astype(BF16),
    )


def _trunk(x, mod, layers, p):
    b, l, _ = x.shape
    n_tok = b * l
    cap = EC_CAPACITY * n_tok // N_EXPERTS
    nblk = n_tok // ROUTE_BLK
    slots = cap + 8 * nblk
    n_tiles = -(-slots // SLOT_TILE)
    slots_alloc = -(-(n_tiles * SLOT_TILE + ROUTE_WIN) // SLOT_TILE) * SLOT_TILE
    cos2, sin2 = _rope_tables(l)
    tri = (np.arange(ROUTE_BLK)[None, :] < np.arange(ROUTE_BLK)[:, None]).astype(np.float32)
    tri = jnp.asarray(tri, BF16)
    for li in range(DEPTH):
        w = layers[li]
        sh1, sc1, g1, sh2, sc2, g2 = [mod[li, :, i * D:(i + 1) * D].reshape(b, 1, D) for i in range(6)]
        hg_in, at_in, ssd_in, dtt = _inproj_call(x, p['norm1_g'][li], sc1, sh1, w['w_main'], w['w_dtt'])
        o_f = _hgrn_call(hg_in, w['lb_rows'][0], w['hg_norm'], None)
        o_hg = _hgrn_call(hg_in, w['lb_rows'][1], w['hg_norm'], o_f)
        qt, kt, vt = _attn_prep_call(at_in, cos2, sin2, w['gq'], w['gk'], w['ek'], w['evt'], w['eye'])
        o_at = _attn_call(qt, kt, vt, w['place'])
        xbc = _ssd_conv_call(ssd_in, w['conv_w'], w['conv_b'])
        y_f = _ssd_call(xbc, ssd_in, dtt, w['par_rows'], w['par_cols'], w['ssd_norm'], None)
        o_ssd = _ssd_call(xbc, ssd_in, dtt, w['par_rows'], w['par_cols'], w['ssd_norm'], y_f)
        x1, h2, aff, afft = _outproj_call(x, o_hg, o_at, o_ssd, w['w_o1'], w['w_o2'], w['w_o3'], g1,
                                          p['norm2_g'][li], sc2, sh2, w['w_r'])
        tab_e, tab_t = _route_thr_call(afft, cap)
        cnt = tab_e[0, :, :nblk].reshape(-1)
        off = tab_e[1, :, :nblk].reshape(-1)
        aff2d = aff.reshape(n_tok, LANES)
        xg = _gather_call(cnt, off, afft, tab_e, tri, h2.reshape(n_tok, D), slots_alloc)
        used = tab_e[1, :, nblk - 1] + (tab_e[0, :, nblk - 1] + 7) // 8 * 8
        y = _ffn_call(used, xg, w['wg'], w['wu'], w['wd'])
        x = _combine_call(cnt, off, aff2d, tab_t, tri, x1.reshape(n_tok, D), g2, p['final_g'], y,
                          l // ROUTE_BLK, li == DEPTH - 1).reshape(b, l, D)
    return x


def kernel(x_prompt, x_sample, c_prompt, c_sample, norm1_g, norm2_g, w_mod, b_mod, w_in, hg_lb, hg_norm_g,
           q_norm_g, k_norm_g, conv_w, conv_b, a_log, dt_bias, d_skip, ssd_norm_g, w_out, w_router,
           w_gate, w_up, w_down, final_g):
    p = dict(norm1_g=norm1_g, norm2_g=norm2_g, w_in=w_in, hg_lb=hg_lb, hg_norm_g=hg_norm_g, q_norm_g=q_norm_g,
             k_norm_g=k_norm_g, conv_w=conv_w, conv_b=conv_b, a_log=a_log, dt_bias=dt_bias, d_skip=d_skip,
             ssd_norm_g=ssd_norm_g, w_out=w_out, w_router=w_router, w_gate=w_gate, w_up=w_up, w_down=w_down,
             final_g=final_g)
    bp, bs = c_prompt.shape[0], c_sample.shape[0]
    rows = -(-(bp + bs) // 8) * 8
    c_all = jnp.pad(jnp.concatenate([c_prompt, c_sample], axis=0), ((0, rows - bp - bs), (0, 0)))
    mod = _mod_call(c_all, w_mod, b_mod)
    layers = [_prep_layer(li, p) for li in range(DEPTH)]
    y_prompt = _trunk(x_prompt, mod[:, :bp], layers, p)
    y_sample = _trunk(x_sample, mod[:, bp:bp + bs], layers, p)
    return (y_prompt, y_sample)
```

```python
import functools

import numpy as np
import jax
import jax.numpy as jnp
from jax import lax
from jax.experimental import pallas as pl
from jax.experimental.pallas import tpu as pltpu

F32 = jnp.float32
BF16 = jnp.bfloat16
I32 = jnp.int32
HI = lax.Precision.HIGHEST

D = 1024
DEPTH = 2
GRID_W = 64
EPS = 1e-6
HD = 64
HG_W = 256
HG_CHUNK = 64
HG_SUB = 16
AT_W = 512
KV_W = 128
AT_GROUP = 4
ROPE_THETA = 10000.0
QK_SCALE_LOG2 = (HD ** -0.5) * 1.4426950408889634
SSD_W = 256
SSD_XBC = 512
SSD_CHUNK = 128
SSD_CONV = 5
N_EXPERTS = 16
EC_CAPACITY = 2
HG_COLS = 5 * HG_W
AT_COLS = AT_W + 2 * KV_W
SSD_COLS = SSD_W + SSD_XBC + 128
IN_PAD = HG_COLS + AT_COLS + SSD_COLS
LANES = 128
ATTN_KEY_BLK = 1024
SLOT_TILE = 256
ROUTE_BLK = 256
ROUTE_WIN = 64
VMEM_LIMIT = 56 * 1024 * 1024


def _cp(sem, vmem=VMEM_LIMIT):
    return pltpu.CompilerParams(dimension_semantics=sem, vmem_limit_bytes=vmem)


def _dot(a, b, prec=None):
    return jnp.dot(a, b, preferred_element_type=F32, precision=prec)


def _dot_nt(a, b, prec=None):
    return lax.dot_general(a, b, (((1,), (1,)), ((), ())), preferred_element_type=F32, precision=prec)


def _dot_tn(a, b, prec=None):
    return lax.dot_general(a, b, (((0,), (0,)), ((), ())), preferred_element_type=F32, precision=prec)


def _sigmoid(x):
    return 1.0 / (1.0 + jnp.exp(-x))


def _silu(x):
    return x * _sigmoid(x)


def _softplus(x):
    return jnp.maximum(x, 0.0) + jnp.log1p(jnp.exp(-jnp.abs(x)))


def _iota(shape, dim):
    return lax.broadcasted_iota(I32, shape, dim)


def _split_bf16(x):
    hi = x.astype(BF16)
    return hi, (x - hi.astype(F32)).astype(BF16)


def _head_mean_sq(x, width):
    bd = jnp.where(_iota((width, width), 0) // HD == _iota((width, width), 1) // HD, 1.0 / HD, 0.0).astype(BF16)
    hi, lo = _split_bf16(x * x)
    return _dot(hi, bd) + _dot(lo, bd)


def _mod_body(c_ref, w_ref, b_ref, o_ref):
    o_ref[0] = _dot(_silu(c_ref[...]), w_ref[0], HI) + b_ref[0]


def _mod_call(c_all, w_mod, b_mod):
    bp = c_all.shape[0]
    return pl.pallas_call(
        _mod_body, name="adaln_mod",
        grid=(DEPTH, 6),
        in_specs=[pl.BlockSpec((bp, D), lambda l, j: (0, 0)),
                  pl.BlockSpec((1, D, D), lambda l, j: (l, 0, j)),
                  pl.BlockSpec((1, 1, D), lambda l, j: (l, 0, j))],
        out_specs=pl.BlockSpec((1, bp, D), lambda l, j: (l, 0, j)),
        out_shape=jax.ShapeDtypeStruct((DEPTH, bp, 6 * D), F32),
        compiler_params=_cp(("arbitrary", "arbitrary")),
    )(c_all, w_mod, b_mod.reshape(DEPTH, 1, 6 * D))


def _inproj_body(x_ref, g_ref, sc_ref, sh_ref, w_ref, wdt_ref, hg_ref, at_ref, ssd_ref, dtt_ref):
    x = x_ref[0]
    ms = jnp.mean(x * x, axis=-1, keepdims=True)
    y = x * lax.rsqrt(ms + EPS) * g_ref[...]
    h = (y * (1.0 + sc_ref[0]) + sh_ref[0]).astype(BF16)
    p = _dot(h, w_ref[...])
    hg_ref[0] = p[:, :HG_COLS]
    at_ref[0] = p[:, HG_COLS:HG_COLS + AT_COLS]
    ssd_ref[0] = p[:, HG_COLS + AT_COLS:]
    dtt_ref[0] = _dot_nt(wdt_ref[...], h)


def _inproj_call(x, norm_g, scale, shift, w_main, w_dtt):
    b, l, _ = x.shape
    tm = min(l, 512)
    row = lambda: pl.BlockSpec((1, tm, D), lambda bi, i: (bi, i, 0))
    per_b = lambda: pl.BlockSpec((1, 1, D), lambda bi, i: (bi, 0, 0))
    in_specs = [row(), pl.BlockSpec((1, D), lambda bi, i: (0, 0)), per_b(), per_b(),
                pl.BlockSpec((D, IN_PAD), lambda bi, i: (0, 0)),
                pl.BlockSpec((8, D), lambda bi, i: (0, 0))]
    args = [x, norm_g.reshape(1, D), scale, shift, w_main, w_dtt]
    out_specs = [pl.BlockSpec((1, tm, HG_COLS), lambda bi, i: (bi, i, 0)),
                 pl.BlockSpec((1, tm, AT_COLS), lambda bi, i: (bi, i, 0)),
                 pl.BlockSpec((1, tm, SSD_COLS), lambda bi, i: (bi, i, 0)),
                 pl.BlockSpec((1, 8, tm), lambda bi, i: (bi, 0, i))]
    out_shape = [jax.ShapeDtypeStruct((b, l, HG_COLS), F32),
                 jax.ShapeDtypeStruct((b, l, AT_COLS), F32),
                 jax.ShapeDtypeStruct((b, l, SSD_COLS), F32),
                 jax.ShapeDtypeStruct((b, 8, l), F32)]
    return pl.pallas_call(
        _inproj_body, name="norm_inproj",
        grid=(b, l // tm), in_specs=in_specs, out_specs=out_specs, out_shape=out_shape,
        compiler_params=_cp(("arbitrary", "arbitrary")),
    )(*args)


def _hgrn_body(bwd, nch, *refs):
    if bwd:
        x_ref, lb_ref, ng_ref, of_ref, o_ref, st_ref = refs
    else:
        x_ref, lb_ref, o_ref, st_ref = refs

    @pl.when(pl.program_id(1) == 0)
    def _():
        st_ref[...] = jnp.zeros_like(st_ref)

    q_chunk = HG_CHUNK
    lane_head = _iota((1, HG_W), 1) // HD
    tril = (_iota((q_chunk, q_chunk), 1) <= _iota((q_chunk, q_chunk), 0)).astype(F32)
    bd_mask = _iota((HG_W, HG_W), 0) // HD == _iota((HG_W, HG_W), 1) // HD
    row_q = _iota((q_chunk, 1), 0)
    att_t = _iota((q_chunk, q_chunk), 0) % HG_SUB
    att_s = _iota((q_chunk, q_chunk), 1)
    log_lb = lb_ref[0:1, :]
    log_1m_lb = lb_ref[1:2, :]
    one_m_lb = lb_ref[2:3, :]
    fcol = 3 * HG_W if bwd else 2 * HG_W

    def chunk(ci, carry):
        c = (nch - 1 - ci) if bwd else ci
        r0 = pl.multiple_of(c * q_chunk, q_chunk)
        rows = pl.ds(r0, q_chunk)
        q = _silu(x_ref[0, rows, 0:HG_W])
        v = x_ref[0, rows, HG_W:2 * HG_W]
        fr = x_ref[0, rows, fcol:fcol + HG_W]
        a2 = log_1m_lb - _softplus(-fr)
        mx = jnp.maximum(log_lb, a2)
        logf = mx + jnp.log1p(jnp.exp(-jnp.abs(log_lb - a2)))
        k = one_m_lb * _sigmoid(-fr)
        b_inc = _dot(tril, logf, HI)
        b_exc = b_inc - logf
        total = b_inc[q_chunk - 1:q_chunk, :]
        st = st_ref[...]
        if bwd:
            q_in = q * jnp.exp(total - b_exc)
            k_st = k * jnp.exp(b_exc)
        else:
            q_in = q * jnp.exp(b_inc)
            k_st = k * jnp.exp(total - b_inc)
        inter = _dot_nt(q_in.astype(BF16), st.astype(BF16))
        upd = _dot_tn(v.astype(BF16), k_st.astype(BF16))
        st_ref[...] = st * jnp.exp(total) + jnp.where(bd_mask, upd, 0.0)
        v16 = v.astype(BF16)
        pieces = []
        for i in range(q_chunk // HG_SUB):
            lo, hi = i * HG_SUB, (i + 1) * HG_SUB
            if bwd:
                ref = b_inc[hi - 1:hi, :]
                qt = q[lo:hi] * jnp.exp(ref - b_exc[lo:hi])
                kt = k * jnp.exp(jnp.where(row_q >= lo, b_exc - ref, 0.0))
                kt = jnp.where(row_q >= lo, kt, 0.0)
                amask = att_s >= att_t + lo
            else:
                ref = b_exc[lo:lo + 1, :]
                qt = q[lo:hi] * jnp.exp(b_inc[lo:hi] - ref)
                kt = k * jnp.exp(jnp.where(row_q < hi, ref - b_inc, 0.0))
                kt = jnp.where(row_q < hi, kt, 0.0)
                amask = att_s <= att_t + lo
            q4 = jnp.concatenate([jnp.where(lane_head == h, qt, 0.0) for h in range(4)], axis=0)
            att = _dot_nt(q4.astype(BF16), kt.astype(BF16))
            att = jnp.where(amask, att, 0.0)
            r = _dot(att.astype(BF16), v16)
            o_i = jnp.where(lane_head == 0, r[0:HG_SUB], 0.0)
            for h in range(1, 4):
                o_i = o_i + jnp.where(lane_head == h, r[h * HG_SUB:(h + 1) * HG_SUB], 0.0)
            pieces.append(o_i + inter[lo:hi])
        o = jnp.concatenate(pieces, axis=0)
        if bwd:
            tot = of_ref[0, rows, :] + o
            ms = _head_mean_sq(tot, HG_W)
            g = x_ref[0, rows, 4 * HG_W:5 * HG_W]
            y = tot * lax.rsqrt(ms + EPS) * ng_ref[...] * _silu(g)
            o_ref[0, rows, :] = y.astype(o_ref.dtype)
        else:
            o_ref[0, rows, :] = o
        return carry

    lax.fori_loop(0, nch, chunk, 0, unroll=min(nch, 8))


def _hgrn_call(hg_in, lb_rows, norm_g, o_fwd):
    b, l, _ = hg_in.shape
    bwd = o_fwd is not None
    t = min(l, 512)
    nb = l // t
    idx = (lambda bi, j: (bi, nb - 1 - j, 0)) if bwd else (lambda bi, j: (bi, j, 0))
    in_specs = [pl.BlockSpec((1, t, HG_COLS), idx), pl.BlockSpec((3, HG_W), lambda bi, j: (0, 0))]
    args = [hg_in, lb_rows]
    if bwd:
        in_specs += [pl.BlockSpec((1, HG_W), lambda bi, j: (0, 0)), pl.BlockSpec((1, t, HG_W), idx)]
        args += [norm_g, o_fwd]
    return pl.pallas_call(
        functools.partial(_hgrn_body, bwd, t // HG_CHUNK), name="hgrn_bwd" if bwd else "hgrn_fwd",
        grid=(b, nb), in_specs=in_specs,
        out_specs=pl.BlockSpec((1, t, HG_W), idx),
        out_shape=jax.ShapeDtypeStruct((b, l, HG_W), BF16 if bwd else F32),
        scratch_shapes=[pltpu.VMEM((HG_W, HG_W), F32)],
        compiler_params=_cp(("arbitrary", "arbitrary")),
    )(*args)


def _rope_swap(x):
    w = x.shape[-1]
    first = (_iota((1, w), 1) % 32) < 16
    return jnp.where(first, pltpu.roll(x, w - 16, 1), pltpu.roll(x, 16, 1))


def _attn_prep_body(x_ref, cos_ref, sin_ref, gq_ref, gk_ref, ek_ref, evt_ref, eye_ref, qt_ref, k_ref, vt_ref):
    x = x_ref[0]
    cos2 = cos_ref[...]
    sin2 = sin_ref[...]
    xq = x[:, :AT_W]
    qn = xq * lax.rsqrt(_head_mean_sq(xq, AT_W) + EPS) * gq_ref[...]
    cos_q = jnp.concatenate([cos2] * 4, axis=1)
    sin_q = jnp.concatenate([sin2] * 4, axis=1)
    q = ((qn * cos_q + _rope_swap(qn) * sin_q) * QK_SCALE_LOG2).astype(BF16)
    qt_ref[0] = _dot_nt(eye_ref[...], q).astype(BF16)
    xk = x[:, AT_W:AT_W + KV_W]
    kn = xk * lax.rsqrt(_head_mean_sq(xk, KV_W) + EPS) * gk_ref[...]
    kr = (kn * cos2 + _rope_swap(kn) * sin2).astype(BF16)
    k_ref[0, 0] = _dot(kr, ek_ref[...]).astype(BF16)
    xv = x[:, AT_W + KV_W:].astype(BF16)
    ones_row = ((_iota((2 * LANES, 1), 0) % LANES) >= HD).astype(F32)
    vt_ref[0, 0] = (_dot_nt(evt_ref[...], xv) + ones_row).astype(BF16)


def _attn_prep_call(at_in, cos2, sin2, gq, gk, ek, evt, eye):
    b, l, _ = at_in.shape
    tm = min(l, ATTN_KEY_BLK)
    const = lambda shape: pl.BlockSpec(shape, lambda bi, i: (0, 0))
    return pl.pallas_call(
        _attn_prep_body, name="attn_prep",
        grid=(b, l // tm),
        in_specs=[pl.BlockSpec((1, tm, AT_COLS), lambda bi, i: (bi, i, 0)),
                  pl.BlockSpec((tm, KV_W), lambda bi, i: (i, 0)),
                  pl.BlockSpec((tm, KV_W), lambda bi, i: (i, 0)),
                  const((1, AT_W)), const((1, KV_W)), const((KV_W, AT_W)), const((2 * LANES, KV_W)),
                  const((AT_W, AT_W))],
        out_specs=[pl.BlockSpec((1, AT_W, tm), lambda bi, i: (bi, 0, i)),
                   pl.BlockSpec((1, 1, tm, AT_W), lambda bi, i: (bi, i, 0, 0)),
                   pl.BlockSpec((1, 1, 2 * LANES, tm), lambda bi, i: (bi, i, 0, 0))],
        out_shape=[jax.ShapeDtypeStruct((b, AT_W, l), BF16),
                   jax.ShapeDtypeStruct((b, l // tm, tm, AT_W), BF16),
                   jax.ShapeDtypeStruct((b, l // tm, 2 * LANES, tm), BF16)],
        compiler_params=_cp(("arbitrary", "arbitrary")),
    )(at_in, cos2, sin2, gq, gk, ek, evt, eye)


def _attn_body(tq, nk, qt_ref, k_ref, vt_ref, place_ref, o_ref, qm_ref, s_ref, m_ref, acc_ref):
    qm_ref[...] = jnp.zeros_like(qm_ref)
    for g in range(AT_GROUP):
        qm_ref[g * HD:(g + 1) * HD, g * tq:(g + 1) * tq] = qt_ref[0, g * HD:(g + 1) * HD, :]
    m_ref[...] = jnp.full_like(m_ref, -jnp.inf)
    acc_ref[...] = jnp.zeros_like(acc_ref)

    def scores(slot, kk):
        s_ref[slot] = _dot(k_ref[0, kk], qm_ref[...]).astype(BF16)

    def consume(slot, kk):
        s = s_ref[slot]
        m_old = m_ref[...]
        m_new = jnp.maximum(m_old, jnp.max(s, axis=0, keepdims=True).astype(F32))
        alpha = jnp.exp2(m_old - m_new)
        p = jnp.exp2(s - m_new.astype(BF16))
        acc_ref[...] = acc_ref[...] * alpha + _dot(vt_ref[0, kk], p)
        m_ref[...] = m_new

    scores(0, 0)
    if nk > 1:
        def pair(i, carry):
            kk = 2 * i
            scores(1, kk + 1)
            consume(0, kk)
            scores(0, kk + 2)
            consume(1, kk + 1)
            return carry

        lax.fori_loop(0, nk // 2 - 1, pair, 0)
        scores(1, nk - 1)
        consume(0, nk - 2)
        consume(1, nk - 1)
    else:
        consume(0, 0)
    acc = acc_ref[...]
    o_t = (acc[0:HD] / acc[HD:HD + 1]).astype(BF16)
    out = _dot_tn(o_t[:, 0:tq], place_ref[0])
    for g in range(1, AT_GROUP):
        out = out + _dot_tn(o_t[:, g * tq:(g + 1) * tq], place_ref[g])
    o_ref[0] = out.astype(o_ref.dtype)


def _attn_call(qt, kt, vt, place):
    b, _, l = qt.shape
    _, nk, tk, _ = kt.shape
    assert nk == 1 or nk % 2 == 0
    tq = min(l, 512)
    return pl.pallas_call(
        functools.partial(_attn_body, tq, nk), name="flash_attn",
        grid=(b, 2, l // tq),
        in_specs=[pl.BlockSpec((1, 4 * HD, tq), lambda bi, j, i: (bi, j, i)),
                  pl.BlockSpec((1, nk, tk, 4 * HD), lambda bi, j, i: (bi, 0, 0, j)),
                  pl.BlockSpec((1, nk, LANES, tk), lambda bi, j, i: (bi, 0, j, 0)),
                  pl.BlockSpec((AT_GROUP, HD, 4 * HD), lambda bi, j, i: (0, 0, 0))],
        out_specs=pl.BlockSpec((1, tq, 4 * HD), lambda bi, j, i: (bi, i, j)),
        out_shape=jax.ShapeDtypeStruct((b, l, AT_W), BF16),
        scratch_shapes=[pltpu.VMEM((4 * HD, AT_GROUP * tq), BF16),
                        pltpu.VMEM((2, tk, AT_GROUP * tq), BF16),
                        pltpu.VMEM((1, AT_GROUP * tq), F32),
                        pltpu.VMEM((LANES, AT_GROUP * tq), F32)],
        compiler_params=_cp(("arbitrary",) * 3),
    )(qt, kt, vt, place)


def _ssd_conv_body(t, x_ref, prev_ref, next_ref, w_ref, b_ref, o_ref, ext_ref):
    j = pl.program_id(1)
    nb = pl.num_programs(1)
    lo, hi = SSD_W, SSD_W + SSD_XBC
    ext_ref[0:8, :] = jnp.where(j > 0, prev_ref[0, :, lo:hi], 0.0)
    ext_ref[8:8 + t, :] = x_ref[0, :, lo:hi]
    ext_ref[8 + t:16 + t, :] = jnp.where(j < nb - 1, next_ref[0, :, lo:hi], 0.0)
    pad = SSD_CONV // 2
    acc = b_ref[...] + w_ref[0:1, :] * ext_ref[pl.ds(8 - pad, t), :]
    for kk in range(1, SSD_CONV):
        acc = acc + w_ref[kk:kk + 1, :] * ext_ref[pl.ds(8 - pad + kk, t), :]
    o_ref[0] = _silu(acc)


def _ssd_conv_call(ssd_in, conv_w, conv_b):
    b, l, _ = ssd_in.shape
    t = min(l, 512)
    t8 = t // 8
    last8 = l // 8 - 1
    return pl.pallas_call(
        functools.partial(_ssd_conv_body, t), name="ssd_conv",
        grid=(b, l // t),
        in_specs=[pl.BlockSpec((1, t, SSD_COLS), lambda bi, j: (bi, j, 0)),
                  pl.BlockSpec((1, 8, SSD_COLS), lambda bi, j: (bi, jnp.maximum(j * t8 - 1, 0), 0)),
                  pl.BlockSpec((1, 8, SSD_COLS), lambda bi, j: (bi, jnp.minimum((j + 1) * t8, last8), 0)),
                  pl.BlockSpec((8, SSD_XBC), lambda bi, j: (0, 0)),
                  pl.BlockSpec((1, SSD_XBC), lambda bi, j: (0, 0))],
        out_specs=pl.BlockSpec((1, t, SSD_XBC), lambda bi, j: (bi, j, 0)),
        out_shape=jax.ShapeDtypeStruct((b, l, SSD_XBC), F32),
        scratch_shapes=[pltpu.VMEM((t + 16, SSD_XBC), F32)],
        compiler_params=_cp(("arbitrary", "arbitrary")),
    )(ssd_in, ssd_in, ssd_in, conv_w, conv_b)


def _ssd_body(bwd, nch, *refs):
    if bwd:
        xbc_ref, dtc_ref, dtr_ref, par_ref, parc_ref, z_ref, yf_ref, ng_ref, o_ref, st_ref = refs
    else:
        xbc_ref, dtc_ref, dtr_ref, par_ref, parc_ref, o_ref, st_ref = refs

    @pl.when(pl.program_id(1) == 0)
    def _():
        st_ref[...] = jnp.zeros_like(st_ref)

    qc = SSD_CHUNK
    dsel = 4 if bwd else 0
    lane_head = _iota((1, SSD_W), 1) // HD
    lane_grp = _iota((1, LANES), 1) // HD
    tril = (_iota((qc, qc), 1) <= _iota((qc, qc), 0)).astype(F32)
    triu = (_iota((qc, qc), 0) <= _iota((qc, qc), 1)).astype(F32)
    tt = _iota((qc, qc), 0)
    ss = _iota((qc, qc), 1)
    st_mask = (_iota((SSD_W, LANES), 0) // (2 * HD)) == (_iota((SSD_W, LANES), 1) // HD)
    bias_row = par_ref[0:1, 0:LANES]
    acoef_row = par_ref[1:2, 0:LANES]
    dskip_row = par_ref[2:3, :]
    bias_col = parc_ref[:, 0:1]
    acoef_col = parc_ref[:, 1:2]

    def expand(col_vals):
        out = jnp.broadcast_to(col_vals[:, dsel:dsel + 1], (qc, SSD_W))
        for h in range(1, 4):
            out = jnp.where(lane_head == h, jnp.broadcast_to(col_vals[:, dsel + h:dsel + h + 1], (qc, SSD_W)), out)
        return out

    def chunk(ci, carry):
        c = (nch - 1 - ci) if bwd else ci
        r0 = pl.multiple_of(c * qc, qc)
        rows = pl.ds(r0, qc)
        xs = xbc_ref[0, rows, 0:SSD_W]
        bm = xbc_ref[0, rows, SSD_W:SSD_W + LANES]
        cm = xbc_ref[0, rows, SSD_W + LANES:SSD_W + 2 * LANES]
        dt_c = _softplus(dtc_ref[0, rows, :] + bias_row)
        a_c = dt_c * acoef_row
        dt_r = _softplus(dtr_ref[0, :, rows] + bias_col)
        a_r = dt_r * acoef_col
        inc_c = _dot(tril, a_c, HI)
        inc_r = _dot(a_r, triu, HI)
        if bwd:
            cum_c, cum_r = inc_c - a_c, inc_r - a_r
        else:
            cum_c, cum_r = inc_c, inc_r
        total_c = inc_c[qc - 1:qc, :]
        xdt = xs * expand(dt_c)
        xdt16 = xdt.astype(BF16)
        bm16 = bm.astype(BF16)
        cm16 = cm.astype(BF16)
        gmat = [_dot_nt(jnp.where(lane_grp == g, cm, 0.0).astype(BF16), bm16) for g in range(2)]
        y = jnp.zeros((qc, SSD_W), F32)
        for h in range(4):
            col = cum_c[:, dsel + h:dsel + h + 1]
            rw = cum_r[dsel + h:dsel + h + 1, :]
            if bwd:
                dec = jnp.where(ss >= tt, jnp.exp(jnp.where(ss >= tt, rw - col, 0.0)), 0.0)
            else:
                dec = jnp.where(ss <= tt, jnp.exp(jnp.where(ss <= tt, col - rw, 0.0)), 0.0)
            yh = _dot((gmat[h // 2] * dec).astype(BF16), xdt16)
            y = jnp.where(lane_head == h, yh, y)
        st = st_ref[...]
        if bwd:
            out_dec = jnp.exp(total_c - cum_c)
            st_dec = jnp.exp(cum_c)
        else:
            out_dec = jnp.exp(cum_c)
            st_dec = jnp.exp(total_c - cum_c)
        y = y + _dot_nt(cm16, st.astype(BF16)) * expand(out_dec)
        upd = _dot_tn((xdt * expand(st_dec)).astype(BF16), bm16)
        tot_rows = jnp.broadcast_to(jnp.exp(total_c[:, dsel:dsel + 1]), (HD, LANES))
        decay_rows = jnp.concatenate(
            [tot_rows] + [jnp.broadcast_to(jnp.exp(total_c[:, dsel + h:dsel + h + 1]), (HD, LANES))
                          for h in range(1, 4)], axis=0)
        st_ref[...] = st * decay_rows + jnp.where(st_mask, upd, 0.0)
        if bwd:
            yy = yf_ref[0, rows, :] + y + xs * dskip_row
            yy = yy * _silu(z_ref[0, rows, :])
            ms = _head_mean_sq(yy, SSD_W)
            o_ref[0, rows, :] = (yy * lax.rsqrt(ms + EPS) * ng_ref[...]).astype(o_ref.dtype)
        else:
            o_ref[0, rows, :] = y
        return carry

    lax.fori_loop(0, nch, chunk, 0, unroll=min(nch, 8))


def _ssd_call(xbc, ssd_in, dtt, par_rows, par_cols, norm_g, y_fwd):
    b, l, _ = xbc.shape
    bwd = y_fwd is not None
    t = min(l, 512)
    nb = l // t
    blk = (lambda bi, j: (bi, nb - 1 - j)) if bwd else (lambda bi, j: (bi, j))
    rows3 = lambda lane_blk: (lambda bi, j: blk(bi, j) + (lane_blk,))
    in_specs = [pl.BlockSpec((1, t, SSD_XBC), rows3(0)),
                pl.BlockSpec((1, t, LANES), rows3((SSD_W + SSD_XBC) // LANES)),
                pl.BlockSpec((1, 8, t), lambda bi, j: (bi, 0, blk(bi, j)[1])),
                pl.BlockSpec((8, SSD_W), lambda bi, j: (0, 0)),
                pl.BlockSpec((8, LANES), lambda bi, j: (0, 0))]
    args = [xbc, ssd_in, dtt, par_rows, par_cols]
    if bwd:
        in_specs += [pl.BlockSpec((1, t, SSD_W), rows3(0)),
                     pl.BlockSpec((1, t, SSD_W), rows3(0)),
                     pl.BlockSpec((1, SSD_W), lambda bi, j: (0, 0))]
        args += [ssd_in, y_fwd, norm_g]
    return pl.pallas_call(
        functools.partial(_ssd_body, bwd, t // SSD_CHUNK), name="ssd_bwd" if bwd else "ssd_fwd",
        grid=(b, nb), in_specs=in_specs,
        out_specs=pl.BlockSpec((1, t, SSD_W), rows3(0)),
        out_shape=jax.ShapeDtypeStruct((b, l, SSD_W), BF16 if bwd else F32),
        scratch_shapes=[pltpu.VMEM((SSD_W, LANES), F32)],
        compiler_params=_cp(("arbitrary", "arbitrary")),
    )(*args)


def _outproj_body(x_ref, hg_ref, at_ref, ssd_ref, w1_ref, w2_ref, w3_ref, g1_ref, ng_ref, sc_ref, sh_ref,
                  wr_ref, x1_ref, h2_ref, aff_ref, afft_ref):
    mix = _dot(hg_ref[0], w1_ref[...]) + _dot(at_ref[0], w2_ref[...]) + _dot(ssd_ref[0], w3_ref[...])
    x1 = x_ref[0] + g1_ref[0] * mix
    x1_ref[0] = x1
    ms = jnp.mean(x1 * x1, axis=-1, keepdims=True)
    h2 = x1 * lax.rsqrt(ms + EPS) * ng_ref[...] * (1.0 + sc_ref[0]) + sh_ref[0]
    h2_ref[0] = h2.astype(h2_ref.dtype)
    h_hi, h_lo = _split_bf16(h2)
    logits = _dot(h_hi, wr_ref[0]) + _dot(h_lo, wr_ref[0]) + _dot(h_hi, wr_ref[1])
    valid = _iota((1, LANES), 1) < N_EXPERTS
    logits = jnp.where(valid, logits, -jnp.inf)
    e = jnp.exp(logits - jnp.max(logits, axis=-1, keepdims=True))
    aff = e / jnp.sum(e, axis=-1, keepdims=True)
    aff_ref[0] = aff
    afft_ref[...] = jnp.transpose(aff)[0:N_EXPERTS, :]


def _outproj_call(x, o_hg, o_at, o_ssd, w1, w2, w3, gate1, norm_g, scale, shift, w_r):
    b, l, _ = x.shape
    tm = min(l, 512)
    nbl = l // tm
    row = lambda w: pl.BlockSpec((1, tm, w), lambda bi, i: (bi, i, 0))
    per_b = lambda: pl.BlockSpec((1, 1, D), lambda bi, i: (bi, 0, 0))
    const = lambda shape: pl.BlockSpec(shape, lambda bi, i: (0, 0))
    return pl.pallas_call(
        _outproj_body, name="outproj_router",
        grid=(b, nbl),
        in_specs=[row(D), row(HG_W), row(AT_W), row(SSD_W), const((HG_W, D)), const((AT_W, D)), const((SSD_W, D)),
                  per_b(), const((1, D)), per_b(), per_b(),
                  pl.BlockSpec((2, D, LANES), lambda bi, i: (0, 0, 0))],
        out_specs=[row(D), row(D), row(LANES),
                   pl.BlockSpec((N_EXPERTS, tm), lambda bi, i: (0, bi * nbl + i))],
        out_shape=[jax.ShapeDtypeStruct((b, l, D), F32), jax.ShapeDtypeStruct((b, l, D), BF16),
                   jax.ShapeDtypeStruct((b, l, LANES), F32), jax.ShapeDtypeStruct((N_EXPERTS, b * l), F32)],
        compiler_params=_cp(("arbitrary", "arbitrary")),
    )(x, o_hg, o_at, o_ssd, w1, w2, w3, gate1, norm_g.reshape(1, D), scale, shift, w_r)


def _route_thr_body(cap, n_tok, afft_ref, o_ref, t_ref):
    lane_chunk = 2048 if n_tok % 2048 == 0 else ROUTE_BLK
    n_chunks = n_tok // lane_chunk

    def count_ge(cand):
        def body(i, acc):
            bits = lax.bitcast_convert_type(afft_ref[:, pl.ds(pl.multiple_of(i * lane_chunk, lane_chunk), lane_chunk)], I32)
            return acc + (bits >= cand).astype(F32)
        acc = lax.fori_loop(0, n_chunks, body, jnp.zeros((N_EXPERTS, lane_chunk), F32))
        return jnp.sum(acc, axis=1, keepdims=True)

    def bit_step(i, thr):
        cand = thr | jnp.left_shift(jnp.int32(1), 30 - i)
        return jnp.where(count_ge(cand) >= cap, cand, thr)

    thr = lax.fori_loop(0, 31, bit_step, jnp.zeros((N_EXPERTS, 1), I32))
    n_gt = count_ge(thr + 1)
    need = cap - n_gt

    nblk = n_tok // ROUTE_BLK
    nbp = o_ref.shape[2]
    lane = _iota((1, nbp), 1)

    def blk_counts(j, carry):
        gt_tab, eq_tab = carry
        bits = lax.bitcast_convert_type(afft_ref[:, pl.ds(pl.multiple_of(j * ROUTE_BLK, ROUTE_BLK), ROUTE_BLK)], I32)
        cg = jnp.sum((bits > thr).astype(F32), axis=1, keepdims=True)
        ce = jnp.sum((bits == thr).astype(F32), axis=1, keepdims=True)
        return (gt_tab + jnp.where(lane == j, cg, 0.0), eq_tab + jnp.where(lane == j, ce, 0.0))

    zeros = jnp.zeros((N_EXPERTS, nbp), F32)
    gt_tab, eq_tab = lax.fori_loop(0, nblk, blk_counts, (zeros, zeros))
    strict = (_iota((nbp, nbp), 0) < _iota((nbp, nbp), 1)).astype(F32)
    eq_before = _dot(eq_tab, strict, HI)
    eq_take = jnp.clip(need - eq_before, 0.0, eq_tab)
    cnt = gt_tab + eq_take
    cnt8 = jnp.floor((cnt + 7.0) * 0.125) * 8.0
    off = _dot(cnt8, strict, HI)
    o_ref[0] = cnt.astype(I32)
    o_ref[1] = off.astype(I32)
    o_ref[2] = jnp.broadcast_to(thr, (N_EXPERTS, nbp))
    o_ref[3] = jnp.broadcast_to(need.astype(I32), (N_EXPERTS, nbp))
    o_ref[4] = eq_before.astype(I32)
    ident = (_iota((N_EXPERTS, LANES), 0) == _iota((N_EXPERTS, LANES), 1)).astype(F32)
    flip = lambda tab: _dot_tn(tab, ident, HI).astype(I32)
    thr_b = jnp.broadcast_to(thr, (N_EXPERTS, nbp))
    thr_hi = flip(jnp.right_shift(thr_b, 15).astype(F32))
    thr_lo = flip(jnp.bitwise_and(thr_b, 0x7FFF).astype(F32))
    t_ref[0] = jnp.left_shift(thr_hi, 15) | thr_lo
    t_ref[1] = flip(jnp.broadcast_to(need, (N_EXPERTS, nbp)))
    t_ref[2] = flip(eq_before)


def _route_thr_call(afft, cap):
    n_tok = afft.shape[1]
    nbp = -(-(n_tok // ROUTE_BLK) // LANES) * LANES
    return pl.pallas_call(
        functools.partial(_route_thr_body, cap, n_tok), name="route_threshold",
        grid=(1,),
        in_specs=[pl.BlockSpec((N_EXPERTS, n_tok), lambda i: (0, 0))],
        out_specs=[pl.BlockSpec((5, N_EXPERTS, nbp), lambda i: (0, 0, 0)),
                   pl.BlockSpec((3, nbp, LANES), lambda i: (0, 0, 0))],
        out_shape=[jax.ShapeDtypeStruct((5, N_EXPERTS, nbp), I32),
                   jax.ShapeDtypeStruct((3, nbp, LANES), I32)],
        compiler_params=_cp(("arbitrary",)),
    )(afft)


def _route_select(aff_ref, tab_ref, tri_ref):
    j = pl.program_id(0)
    bits = lax.bitcast_convert_type(aff_ref[...], I32)
    thr = tab_ref[0, 0:1, :]
    need = tab_ref[1, 0:1, :].astype(F32)
    eq_before = tab_ref[2, pl.ds(j, 1), :].astype(F32)
    gt = bits > thr
    eq = bits == thr
    tri = tri_ref[...]
    eq_rank = _dot(tri, eq.astype(BF16)) + eq_before
    sel = gt | (eq & (eq_rank < need))
    rank = _dot(tri, sel.astype(BF16))
    return jnp.where(sel, rank, -1.0)


def _onehot_pair(sel_rank, e, wi):
    slot = (_iota((1, 2 * ROUTE_WIN), 1) % ROUTE_WIN + wi * ROUTE_WIN).astype(F32)
    return jnp.where(sel_rank[:, e:e + 1] == slot, 1.0, 0.0).astype(BF16)


def _gather_body(nblk, cnt_s, off_s, afft_ref, tab_ref, tri_ref, h2_ref, xg_hbm, oh_ref, stage, zero_ref, sem, zsem):
    j = pl.program_id(0)
    w = ROUTE_WIN
    aff = afft_ref[...]
    bits = lax.bitcast_convert_type(aff, I32)
    blk_lane = _iota((1, tab_ref.shape[2]), 1)
    thr = tab_ref[2, :, 0:1]
    need = tab_ref[3, :, 0:1].astype(F32)
    eq_before = jnp.sum(jnp.where(blk_lane == j, tab_ref[4], 0), axis=1, keepdims=True).astype(F32)
    gt = bits > thr
    eq = bits == thr
    tri = tri_ref[...]
    eq_rank = _dot_nt(eq.astype(BF16), tri) + eq_before
    sel = gt | (eq & (eq_rank < need))
    rank = _dot_nt(sel.astype(BF16), tri)
    sel_rank = jnp.where(sel, rank, -1.0)
    h2 = h2_ref[...]

    def onehot(e, wi):
        slot = (_iota((w, 1), 0) + wi * w).astype(F32)
        return sel_rank[e:e + 1] == slot

    def gate_lanes(e, oh):
        gate = jnp.sum(jnp.where(oh, aff[e:e + 1], 0.0), axis=1, keepdims=True)
        return jnp.broadcast_to(gate, (w, LANES))

    slot = j % 2

    def group_copy(sl, e, off, wi, r):
        return pltpu.make_async_copy(stage.at[sl, e, pl.ds(r * 8, 8)],
                                     xg_hbm.at[e, pl.ds(pl.multiple_of(off + wi * w + r * 8, 8), 8)], sem.at[sl, e])

    def groups(cnt, wi):
        return (jnp.minimum(cnt - wi * w, w) + 7) // 8

    def start_groups(sl, e, cnt, off, wi):
        lax.fori_loop(0, groups(cnt, wi), lambda r, c: (group_copy(sl, e, off, wi, r).start(priority=e % 2), c)[1], 0)

    def wait_groups(sl, e, cnt, off, wi):
        lax.fori_loop(0, groups(cnt, wi), lambda r, c: (group_copy(sl, e, off, wi, r).wait(), c)[1], 0)

    def wait_last_window(sl, step):
        for e in range(N_EXPERTS):
            cnt = cnt_s[e * nblk + step]
            wait_groups(sl, e, cnt, off_s[e * nblk + step], jnp.maximum((cnt + (w - 1)) // w - 1, 0))

    for e in range(N_EXPERTS):
        oh = onehot(e, 0)
        oh_ref[e * w:(e + 1) * w, :] = jnp.where(oh, 1.0, 0.0).astype(BF16)
        stage[slot, e, :, D:] = gate_lanes(e, oh)
    stage[slot, :, :, 0:D] = _dot(oh_ref[...], h2).reshape(N_EXPERTS, w, D)
    for e in range(N_EXPERTS):
        start_groups(slot, e, cnt_s[e * nblk + j], off_s[e * nblk + j], 0)

    for e in range(N_EXPERTS):
        cnt = cnt_s[e * nblk + j]
        off = off_s[e * nblk + j]
        n_win = (cnt + (w - 1)) // w

        def window(wi, carry, e=e, cnt=cnt, off=off):
            wait_groups(slot, e, cnt, off, wi - 1)
            oh = onehot(e, wi)
            stage[slot, e, :, 0:D] = _dot(jnp.where(oh, 1.0, 0.0).astype(BF16), h2)
            stage[slot, e, :, D:] = gate_lanes(e, oh)
            start_groups(slot, e, cnt, off, wi)
            return carry

        lax.fori_loop(1, n_win, window, 0)

    @pl.when(j > 0)
    def _():
        wait_last_window(1 - slot, j - 1)

    @pl.when(j == nblk - 1)
    def _():
        wait_last_window(slot, j)
        zero_ref[...] = jnp.zeros_like(zero_ref)
        for e in range(N_EXPERTS):
            used = off_s[e * nblk + j] + (cnt_s[e * nblk + j] + 7) // 8 * 8
            n_fill = ((SLOT_TILE - used % SLOT_TILE) % SLOT_TILE) // 8

            def fill_copy(r, e=e, used=used):
                return pltpu.make_async_copy(zero_ref, xg_hbm.at[e, pl.ds(pl.multiple_of(used + r * 8, 8), 8)],
                                             zsem.at[e])

            lax.fori_loop(0, n_fill, lambda r, c, f=fill_copy: (f(r).start(), c)[1], 0)
            lax.fori_loop(0, n_fill, lambda r, c, f=fill_copy: (f(r).wait(), c)[1], 0)


def _gather_call(cnt, off, afft, tab_e, tri, h2_flat, slots_alloc):
    n_tok = h2_flat.shape[0]
    nblk = n_tok // ROUTE_BLK
    grid_spec = pltpu.PrefetchScalarGridSpec(
        num_scalar_prefetch=2,
        grid=(nblk,),
        in_specs=[pl.BlockSpec((N_EXPERTS, ROUTE_BLK), lambda j, *_: (0, j)),
                  pl.BlockSpec(tab_e.shape, lambda j, *_: (0, 0, 0)),
                  pl.BlockSpec((ROUTE_BLK, ROUTE_BLK), lambda j, *_: (0, 0)),
                  pl.BlockSpec((ROUTE_BLK, D), lambda j, *_: (j, 0))],
        out_specs=pl.BlockSpec(memory_space=pl.ANY),
        scratch_shapes=[pltpu.VMEM((N_EXPERTS * ROUTE_WIN, ROUTE_BLK), BF16),
                        pltpu.VMEM((2, N_EXPERTS, ROUTE_WIN, D + LANES), F32),
                        pltpu.VMEM((8, D + LANES), F32),
                        pltpu.SemaphoreType.DMA((2, N_EXPERTS)),
                        pltpu.SemaphoreType.DMA((N_EXPERTS,))],
    )
    return pl.pallas_call(
        functools.partial(_gather_body, nblk), name="expert_gather",
        grid_spec=grid_spec,
        out_shape=jax.ShapeDtypeStruct((N_EXPERTS, slots_alloc, D + LANES), F32),
        compiler_params=_cp(("arbitrary",)),
    )(cnt, off, afft, tab_e, tri, h2_flat)


def _ffn_body(used_s, x_ref, wg_ref, wu_ref, wd_ref, y_ref):
    e = pl.program_id(0)
    i = pl.program_id(1)

    @pl.when(i * SLOT_TILE < used_s[e])
    def _():
        xg = x_ref[0, :, 0:D].astype(BF16)
        gate = x_ref[0, :, D:D + 1]
        hid = _silu(_dot(xg, wg_ref[0])) * _dot(xg, wu_ref[0])
        y_ref[0] = _dot(hid.astype(BF16), wd_ref[0]) * gate

    @pl.when(i * SLOT_TILE >= used_s[e])
    def _():
        y_ref[...] = jnp.zeros_like(y_ref)


def _ffn_call(used, xg, wg, wu, wd):
    _, slots_alloc, _ = xg.shape
    wspec = lambda: pl.BlockSpec((1, D, D), lambda e, i, used_s: (e, 0, 0))
    last_tile = lambda e, i, used_s: jnp.minimum(i, jnp.maximum(used_s[e] - 1, 0) // SLOT_TILE)
    grid_spec = pltpu.PrefetchScalarGridSpec(
        num_scalar_prefetch=1,
        grid=(N_EXPERTS, slots_alloc // SLOT_TILE),
        in_specs=[pl.BlockSpec((1, SLOT_TILE, D + LANES), lambda e, i, used_s: (e, last_tile(e, i, used_s), 0)),
                  wspec(), wspec(), wspec()],
        out_specs=pl.BlockSpec((1, SLOT_TILE, D), lambda e, i, used_s: (e, i, 0)),
    )
    return pl.pallas_call(
        _ffn_body, name="expert_ffn",
        grid_spec=grid_spec,
        out_shape=jax.ShapeDtypeStruct((N_EXPERTS, slots_alloc, D), F32),
        compiler_params=_cp(("arbitrary", "arbitrary")),
    )(used, xg, wg, wu, wd)


def _combine_body(final, nblk, cnt_s, off_s, aff_ref, tab_ref, tri_ref, x1_ref, g2_ref, fg_ref, y_hbm,
                  o_ref, ybuf, ysplit, oh_ref, acc_ref, sems):
    j = pl.program_id(0)
    sel_rank = _route_select(aff_ref, tab_ref, tri_ref)
    w = ROUTE_WIN

    slot = j % 2

    def win_copy(sl, e, off, wi):
        return pltpu.make_async_copy(y_hbm.at[e, pl.ds(pl.multiple_of(off + wi * w, 8), w)], ybuf.at[sl, e],
                                     sems.at[sl, e])

    split = _split_bf16

    @pl.when(j == 0)
    def _():
        for e in range(N_EXPERTS):
            win_copy(0, e, off_s[e * nblk], 0).start()

    @pl.when(j + 1 < nblk)
    def _():
        for e in range(N_EXPERTS):
            win_copy(1 - slot, e, off_s[e * nblk + j + 1], 0).start()

    for e in range(N_EXPERTS):
        oh_ref[:, e * 2 * w:(e + 1) * 2 * w] = _onehot_pair(sel_rank, e, 0)
    for e in range(N_EXPERTS):
        win_copy(slot, e, off_s[e * nblk + j], 0).wait()
        y_hi, y_lo = split(ybuf[slot, e])
        ysplit[(2 * e) * w:(2 * e + 1) * w, :] = y_hi
        ysplit[(2 * e + 1) * w:(2 * e + 2) * w, :] = y_lo
    acc_ref[...] = _dot(oh_ref[...], ysplit[...])
    for e in range(N_EXPERTS):
        cnt = cnt_s[e * nblk + j]
        off = off_s[e * nblk + j]

        def window(wi, carry, e=e, off=off):
            cp = win_copy(slot, e, off, wi)
            cp.start()
            cp.wait()
            y_hi, y_lo = split(ybuf[slot, e])
            acc_ref[...] += _dot(_onehot_pair(sel_rank, e, wi), jnp.concatenate([y_hi, y_lo], axis=0))
            return carry

        lax.fori_loop(1, (cnt + (w - 1)) // w, window, 0)
    x2 = x1_ref[...] + g2_ref[0] * acc_ref[...]
    if final:
        ms = jnp.mean(x2 * x2, axis=-1, keepdims=True)
        x2 = x2 * lax.rsqrt(ms + EPS) * fg_ref[...]
    o_ref[...] = x2


def _combine_call(cnt, off, aff2d, tab_t, tri, x1_flat, gate2, final_g, y, blocks_per_batch, final):
    n_tok = x1_flat.shape[0]
    nblk = n_tok // ROUTE_BLK
    grid_spec = pltpu.PrefetchScalarGridSpec(
        num_scalar_prefetch=2,
        grid=(nblk,),
        in_specs=[pl.BlockSpec((ROUTE_BLK, LANES), lambda j, *_: (j, 0)),
                  pl.BlockSpec(tab_t.shape, lambda j, *_: (0, 0, 0)),
                  pl.BlockSpec((ROUTE_BLK, ROUTE_BLK), lambda j, *_: (0, 0)),
                  pl.BlockSpec((ROUTE_BLK, D), lambda j, *_: (j, 0)),
                  pl.BlockSpec((1, 1, D), lambda j, *_: (j // blocks_per_batch, 0, 0)),
                  pl.BlockSpec((1, D), lambda j, *_: (0, 0)),
                  pl.BlockSpec(memory_space=pl.ANY)],
        out_specs=pl.BlockSpec((ROUTE_BLK, D), lambda j, *_: (j, 0)),
        scratch_shapes=[pltpu.VMEM((2, N_EXPERTS, ROUTE_WIN, D), F32),
                        pltpu.VMEM((2 * N_EXPERTS * ROUTE_WIN, D), BF16),
                        pltpu.VMEM((ROUTE_BLK, 2 * N_EXPERTS * ROUTE_WIN), BF16),
                        pltpu.VMEM((ROUTE_BLK, D), F32),
                        pltpu.SemaphoreType.DMA((2, N_EXPERTS))],
    )
    return pl.pallas_call(
        functools.partial(_combine_body, final, nblk), name="expert_combine",
        grid_spec=grid_spec,
        out_shape=jax.ShapeDtypeStruct((n_tok, D), F32),
        compiler_params=_cp(("arbitrary",)),
    )(cnt, off, aff2d, tab_t, tri, x1_flat, gate2, final_g.reshape(1, D), y)


def _rope_tables(l):
    quarter = HD // 4
    inv = ROPE_THETA ** (-jnp.arange(quarter, dtype=F32) / quarter)
    t = jnp.arange(l)
    pos = jnp.stack([(t // GRID_W).astype(F32), (t % GRID_W).astype(F32)], axis=1)
    lane = np.arange(KV_W)
    which = (lane % HD) // (HD // 2)
    ang = pos[:, which] * inv[lane % quarter][None, :]
    sign = np.where((lane % (HD // 2)) < quarter, -1.0, 1.0).astype(np.float32)
    return jnp.cos(ang), jnp.sin(ang) * sign[None, :]


def _split_bf16_stack(w):
    hi = w.astype(BF16)
    return jnp.stack([hi, (w - hi.astype(F32)).astype(BF16)])


def _prep_layer(l, p):
    w_in = p['w_in'][l]
    n_main = HG_COLS + AT_COLS + SSD_W + SSD_XBC
    w_main = jnp.concatenate([w_in[:, :n_main], jnp.pad(w_in[:, n_main:], ((0, 0), (0, 120)))], axis=1).astype(BF16)
    w_dtt = w_in[:, n_main:].T.astype(BF16)
    lb_all = jnp.cumsum(jax.nn.softmax(p['hg_lb'].astype(F32), axis=0), axis=0)
    lb = (lb_all - lb_all[:1])[l]
    lb_rows = jnp.stack([jnp.log(lb), jnp.log1p(-lb), 1.0 - lb], axis=1)
    a_coef = -jnp.exp(p['a_log'][l].astype(F32)).reshape(-1)
    dt_bias = p['dt_bias'][l].astype(F32).reshape(-1)
    par_rows = jnp.zeros((8, SSD_W), F32).at[0, :8].set(dt_bias).at[1, :8].set(a_coef)
    par_rows = par_rows.at[2, :].set(jnp.repeat(p['d_skip'][l].astype(F32), HD))
    par_cols = jnp.zeros((8, LANES), F32).at[:, 0].set(dt_bias).at[:, 1].set(a_coef)
    w_out = p['w_out'][l].astype(BF16)
    lane_k = np.arange(KV_W)
    ek = np.zeros((KV_W, AT_W), np.float32)
    for rep in range(AT_GROUP):
        ek[lane_k, (lane_k // HD) * (AT_GROUP * HD) + rep * HD + lane_k % HD] = 1.0
    evt = np.zeros((2 * LANES, KV_W), np.float32)
    evt[(lane_k // HD) * LANES + lane_k % HD, lane_k] = 1.0
    place = np.zeros((AT_GROUP, HD, AT_GROUP * HD), np.float32)
    for g in range(AT_GROUP):
        place[g, np.arange(HD), g * HD + np.arange(HD)] = 1.0
    return dict(
        w_main=w_main, w_dtt=w_dtt, lb_rows=lb_rows,
        hg_norm=jnp.tile(p['hg_norm_g'][l], 4).reshape(1, HG_W),
        gq=jnp.tile(p['q_norm_g'][l], 8).reshape(1, AT_W), gk=jnp.tile(p['k_norm_g'][l], 2).reshape(1, KV_W),
        ek=jnp.asarray(ek, BF16), evt=jnp.asarray(evt, BF16), place=jnp.asarray(place, BF16),
        eye=jnp.asarray(np.eye(AT_W, dtype=np.float32), BF16),
        conv_w=jnp.pad(p['conv_w'][l], ((0, 8 - SSD_CONV), (0, 0))), conv_b=p['conv_b'][l].reshape(1, SSD_XBC),
        par_rows=par_rows, par_cols=par_cols,
        ssd_norm=jnp.tile(p['ssd_norm_g'][l], 4).reshape(1, SSD_W),
        w_o1=w_out[:HG_W], w_o2=w_out[HG_W:HG_W + AT_W], w_o3=w_out[HG_W + AT_W:],
        w_r=_split_bf16_stack(jnp.pad(p['w_router'][l], ((0, 0), (0, LANES - N_EXPERTS)))),
        wg=p['w_gate'][l].astype(BF16), wu=p['w_up'][l].astype(BF16), wd=p['w_down'][l].astype(BF16),
    )


def _trunk(x, mod, layers, p):
    b, l, _ = x.shape
    n_tok = b * l
    cap = EC_CAPACITY * n_tok // N_EXPERTS
    nblk = n_tok // ROUTE_BLK
    slots = cap + 8 * nblk
    n_tiles = -(-slots // SLOT_TILE)
    slots_alloc = -(-(n_tiles * SLOT_TILE + ROUTE_WIN) // SLOT_TILE) * SLOT_TILE
    cos2, sin2 = _rope_tables(l)
    tri = (np.arange(ROUTE_BLK)[None, :] < np.arange(ROUTE_BLK)[:, None]).astype(np.float32)
    tri = jnp.asarray(tri, BF16)
    for li in range(DEPTH):
        w = layers[li]
        sh1, sc1, g1, sh2, sc2, g2 = [mod[li, :, i * D:(i + 1) * D].reshape(b, 1, D) for i in range(6)]
        hg_in, at_in, ssd_in, dtt = _inproj_call(x, p['norm1_g'][li], sc1, sh1, w['w_main'], w['w_dtt'])
        o_f = _hgrn_call(hg_in, w['lb_rows'][0], w['hg_norm'], None)
        o_hg = _hgrn_call(hg_in, w['lb_rows'][1], w['hg_norm'], o_f)
        qt, kt, vt = _attn_prep_call(at_in, cos2, sin2, w['gq'], w['gk'], w['ek'], w['evt'], w['eye'])
        o_at = _attn_call(qt, kt, vt, w['place'])
        xbc = _ssd_conv_call(ssd_in, w['conv_w'], w['conv_b'])
        y_f = _ssd_call(xbc, ssd_in, dtt, w['par_rows'], w['par_cols'], w['ssd_norm'], None)
        o_ssd = _ssd_call(xbc, ssd_in, dtt, w['par_rows'], w['par_cols'], w['ssd_norm'], y_f)
        x1, h2, aff, afft = _outproj_call(x, o_hg, o_at, o_ssd, w['w_o1'], w['w_o2'], w['w_o3'], g1,
                                          p['norm2_g'][li], sc2, sh2, w['w_r'])
        tab_e, tab_t = _route_thr_call(afft, cap)
        cnt = tab_e[0, :, :nblk].reshape(-1)
        off = tab_e[1, :, :nblk].reshape(-1)
        aff2d = aff.reshape(n_tok, LANES)
        xg = _gather_call(cnt, off, afft, tab_e, tri, h2.reshape(n_tok, D), slots_alloc)
        used = tab_e[1, :, nblk - 1] + (tab_e[0, :, nblk - 1] + 7) // 8 * 8
        y = _ffn_call(used, xg, w['wg'], w['wu'], w['wd'])
        x = _combine_call(cnt, off, aff2d, tab_t, tri, x1.reshape(n_tok, D), g2, p['final_g'], y,
                          l // ROUTE_BLK, li == DEPTH - 1).reshape(b, l, D)
    return x


def kernel(x_prompt, x_sample, c_prompt, c_sample, norm1_g, norm2_g, w_mod, b_mod, w_in, hg_lb, hg_norm_g,
           q_norm_g, k_norm_g, conv_w, conv_b, a_log, dt_bias, d_skip, ssd_norm_g, w_out, w_router,
           w_gate, w_up, w_down, final_g):
    p = dict(norm1_g=norm1_g, norm2_g=norm2_g, w_in=w_in, hg_lb=hg_lb, hg_norm_g=hg_norm_g, q_norm_g=q_norm_g,
             k_norm_g=k_norm_g, conv_w=conv_w, conv_b=conv_b, a_log=a_log, dt_bias=dt_bias, d_skip=d_skip,
             ssd_norm_g=ssd_norm_g, w_out=w_out, w_router=w_router, w_gate=w_gate, w_up=w_up, w_down=w_down,
             final_g=final_g)
    bp, bs = c_prompt.shape[0], c_sample.shape[0]
    rows = -(-(bp + bs) // 8) * 8
    c_all = jnp.pad(jnp.concatenate([c_prompt, c_sample], axis=0), ((0, rows - bp - bs), (0, 0)))
    mod = _mod_call(c_all, w_mod, b_mod)
    layers = [_prep_layer(li, p) for li in range(DEPTH)]
    y_prompt = _trunk(x_prompt, mod[:, :bp], layers, p)
    y_sample = _trunk(x_sample, mod[:, bp:bp + bs], layers, p)
    return (y_prompt, y_sample)
```
